```python
import math
import jax, jax.numpy as jnp
from jax import lax
import numpy as np

D_MODEL = 4096
BATCH = 1
SEQ = 8192
DEPTH = 1
DEC_BATCH = 4
DEC_SEQ = 4096
PAST_LEN = 128

HEAD_DIM = 128
N_HEADS_TOTAL = D_MODEL // HEAD_DIM
HA = N_HEADS_TOTAL // 2
KVA = HA // 4
GA = HA // KVA
HB = N_HEADS_TOTAL // 2
KVB = HB // 4
GB = HB // KVB
QA_W = HA * HEAD_DIM
KVA_W = KVA * HEAD_DIM
QB_W = HB * HEAD_DIM
KVB_W = KVB * HEAD_DIM
IN_W = QA_W + 2 * KVA_W + QB_W + 2 * KVB_W + 2 * D_MODEL
SPLITS = [int(v) for v in np.cumsum([QA_W, KVA_W, KVA_W, QB_W, KVB_W, KVB_W, D_MODEL])]
D_FF = ((8 * D_MODEL // 3 + 255) // 256) * 256
GRID_W = 64
Q_BLOCK = 128
WINDOW = 128
HALF_ROT = HEAD_DIM // 2
ROPE_THETA = 10000.0
N_MOD = 6
ALPHA = (2.0 * DEPTH) ** 0.25
BETA = (8.0 * DEPTH) ** -0.25
LN_EPS = 1e-5
RMS_EPS = 1e-6

kernel_name = "hybrid_axial_window_gqa_encoder"


def layer_norm(x, g=None, b=None):
    xf = x.astype(jnp.float32)
    mu = jnp.mean(xf, axis=-1, keepdims=True)
    xc = xf - mu
    y = xc * lax.rsqrt(jnp.mean(xc * xc, axis=-1, keepdims=True) + LN_EPS)
    if g is not None:
        y = y * g.astype(jnp.float32) + b.astype(jnp.float32)
    return y.astype(x.dtype)


def rms_norm(x, g):
    xf = x.astype(jnp.float32)
    y = xf * lax.rsqrt(jnp.mean(xf * xf, axis=-1, keepdims=True) + RMS_EPS) * g.astype(jnp.float32)
    return y.astype(x.dtype)


def axial_rope_tables(T):
    rows = T // GRID_W
    row = jnp.repeat(jnp.arange(rows, dtype=jnp.float32), GRID_W)
    col = jnp.tile(jnp.arange(GRID_W, dtype=jnp.float32), rows)
    inv = 1.0 / (ROPE_THETA ** (jnp.arange(0, HALF_ROT, 2, dtype=jnp.float32) / HALF_ROT))
    ang_r = row[:, None] * inv[None, :]
    ang_c = col[:, None] * inv[None, :]
    return jnp.cos(ang_r), jnp.sin(ang_r), jnp.cos(ang_c), jnp.sin(ang_c)


def rotate(x, cos, sin):
    x1, x2 = jnp.split(x, 2, axis=-1)
    c = cos[None, :, None, :].astype(x.dtype)
    s = sin[None, :, None, :].astype(x.dtype)
    return jnp.concatenate([x1 * c - x2 * s, x2 * c + x1 * s], axis=-1)


def axial_rope(x, cr, sr, cc, sc):
    return jnp.concatenate([rotate(x[..., :HALF_ROT], cr, sr), rotate(x[..., HALF_ROT:], cc, sc)], axis=-1)


def alibi_slopes(n):
    return 2.0 ** (-8.0 * jnp.arange(1, n + 1, dtype=jnp.float32) / n)


def global_attention(q, k, v):
    B, T, KV, G, hd = q.shape
    nb = T // Q_BLOCK
    qb = (q * (1.0 / math.sqrt(hd))).reshape(B, nb, Q_BLOCK, KV, G, hd).transpose(1, 0, 2, 3, 4, 5)

    def block(qi):
        s = jnp.einsum('bqkgd,bskd->bkgqs', qi, k).astype(jnp.float32)
        p = jax.nn.softmax(s, axis=-1).astype(v.dtype)
        return jnp.einsum('bkgqs,bskd->bqkgd', p, v)

    o = lax.map(block, qb)
    return o.transpose(1, 0, 2, 3, 4, 5).reshape(B, T, KV * G * hd)


def window_attention(q, k, v, sink):
    B, T, KV, G, hd = q.shape
    nb = T // Q_BLOCK
    pad = ((0, 0), (Q_BLOCK, Q_BLOCK), (0, 0), (0, 0))
    kp = jnp.pad(k, pad).reshape(B, nb + 2, Q_BLOCK, KV, hd)
    vp = jnp.pad(v, pad).reshape(B, nb + 2, Q_BLOCK, KV, hd)
    kw = jnp.concatenate([kp[:, :-2], kp[:, 1:-1], kp[:, 2:]], axis=2)
    vw = jnp.concatenate([vp[:, :-2], vp[:, 1:-1], vp[:, 2:]], axis=2)
    qb = (q * (1.0 / math.sqrt(hd))).reshape(B, nb, Q_BLOCK, KV, G, hd)
    s = jnp.einsum('bnqkgd,bnskd->bnkgqs', qb, kw).astype(jnp.float32)
    qpos = jnp.arange(T).reshape(nb, Q_BLOCK)
    kpos = jnp.arange(nb)[:, None] * Q_BLOCK - Q_BLOCK + jnp.arange(3 * Q_BLOCK)[None, :]
    dist = jnp.abs(qpos[:, :, None] - kpos[:, None, :])
    valid = (dist <= WINDOW) & (kpos >= 0)[:, None, :] & (kpos < T)[:, None, :]
    slopes = alibi_slopes(KV * G).reshape(KV, G)
    bias = -slopes[None, None, :, :, None, None] * dist.astype(jnp.float32)[None, :, None, None, :, :]
    s = jnp.where(valid[None, :, None, None, :, :], s + bias, -1e30)
    sink_col = jnp.broadcast_to(sink.astype(jnp.float32)[None, None, :, :, None, None], s.shape[:-1] + (1,))
    p = jax.nn.softmax(jnp.concatenate([s, sink_col], axis=-1), axis=-1)[..., :-1].astype(v.dtype)
    o = jnp.einsum('bnkgqs,bnskd->bnqkgd', p, vw)
    return o.reshape(B, T, KV * G * hd)


def encoder_layer(x, c, w_ada, b_ada, w_in, q_norm_a, k_norm_a, sink_b, w_br_a, w_br_b, w_o,
                  ln1_g, ln1_b, w_ffn_gate, w_ffn_up, w_ffn_down, ln2_g, ln2_b):
    B, T, _ = x.shape
    mod = (jax.nn.silu(c) @ w_ada + b_ada).reshape(B, N_MOD, D_MODEL)[:, :, None, :]
    sh1, sc1, g1, sh2, sc2, g2 = [mod[:, i] for i in range(N_MOD)]

    h = layer_norm(x) * (1 + sc1) + sh1
    proj = h @ w_in
    qa, ka, va, qb, kb, vb, ga, gb = jnp.split(proj, SPLITS, axis=-1)

    cr, sr, cc, sc = axial_rope_tables(T)
    qa = axial_rope(rms_norm(qa.reshape(B, T, HA, HEAD_DIM), q_norm_a), cr, sr, cc, sc)
    ka = axial_rope(rms_norm(ka.reshape(B, T, KVA, HEAD_DIM), k_norm_a), cr, sr, cc, sc)
    ya = global_attention(qa.reshape(B, T, KVA, GA, HEAD_DIM), ka, va.reshape(B, T, KVA, HEAD_DIM))

    yb = window_attention(qb.reshape(B, T, KVB, GB, HEAD_DIM), kb.reshape(B, T, KVB, HEAD_DIM),
                          vb.reshape(B, T, KVB, HEAD_DIM), sink_b.reshape(KVB, GB))

    merged = jax.nn.sigmoid(ga) * (ya @ w_br_a) + jax.nn.sigmoid(gb) * (yb @ w_br_b)
    x = layer_norm(ALPHA * x + g1 * (merged @ w_o), ln1_g, ln1_b)

    h = layer_norm(x) * (1 + sc2) + sh2
    f = (jax.nn.silu(h @ w_ffn_gate) * (h @ w_ffn_up)) @ w_ffn_down
    return layer_norm(ALPHA * x + g2 * f, ln2_g, ln2_b)


def setup_inputs(seed: int = 0) -> dict:
    key = jax.random.key(seed)
    ks = jax.random.split(key, 24)
    f32 = jnp.float32

    def w(k, shape, fan_in, scale=1.0):
        return jax.random.normal(k, shape, f32) * (scale * fan_in ** -0.5)

    col_scale = jnp.concatenate([
        jnp.ones((QA_W + KVA_W,), f32), jnp.full((KVA_W,), BETA, f32),
        jnp.ones((QB_W + KVB_W,), f32), jnp.full((KVB_W,), BETA, f32),
        jnp.ones((2 * D_MODEL,), f32)])
    return {
        "x_prompt": jax.random.normal(ks[0], (BATCH, SEQ, D_MODEL), f32),
        "x_sample": jax.random.normal(ks[1], (DEC_BATCH, DEC_SEQ, D_MODEL), f32),
        "c_prompt": jax.random.normal(ks[2], (BATCH, D_MODEL), f32),
        "c_sample": jax.random.normal(ks[3], (DEC_BATCH, D_MODEL), f32),
        "w_ada": w(ks[4], (DEPTH, D_MODEL, N_MOD * D_MODEL), D_MODEL, 0.5),
        "b_ada": 0.02 * jax.random.normal(ks[5], (DEPTH, N_MOD * D_MODEL), f32),
        "w_in": w(ks[6], (DEPTH, D_MODEL, IN_W), D_MODEL) * col_scale,
        "q_norm_a": 1.0 + 0.05 * jax.random.normal(ks[7], (DEPTH, HEAD_DIM), f32),
        "k_norm_a": 1.0 + 0.05 * jax.random.normal(ks[8], (DEPTH, HEAD_DIM), f32),
        "sink_b": 0.5 * jax.random.normal(ks[9], (DEPTH, HB), f32),
        "w_br_a": w(ks[10], (DEPTH, QA_W, D_MODEL), QA_W, BETA),
        "w_br_b": w(ks[11], (DEPTH, QB_W, D_MODEL), QB_W, BETA),
        "w_o": w(ks[12], (DEPTH, D_MODEL, D_MODEL), D_MODEL, BETA),
        "ln1_g": 1.0 + 0.05 * jax.random.normal(ks[13], (DEPTH, D_MODEL), f32),
        "ln1_b": 0.02 * jax.random.normal(ks[14], (DEPTH, D_MODEL), f32),
        "w_ffn_gate": w(ks[15], (DEPTH, D_MODEL, D_FF), D_MODEL),
        "w_ffn_up": w(ks[16], (DEPTH, D_MODEL, D_FF), D_MODEL, BETA),
        "w_ffn_down": w(ks[17], (DEPTH, D_FF, D_MODEL), D_FF, BETA),
        "ln2_g": 1.0 + 0.05 * jax.random.normal(ks[18], (DEPTH, D_MODEL), f32),
        "ln2_b": 0.02 * jax.random.normal(ks[19], (DEPTH, D_MODEL), f32),
    }


def reference(x_prompt, x_sample, c_prompt, c_sample, w_ada, b_ada, w_in, q_norm_a, k_norm_a, sink_b,
              w_br_a, w_br_b, w_o, ln1_g, ln1_b, w_ffn_gate, w_ffn_up, w_ffn_down, ln2_g, ln2_b):
    y_prompt = x_prompt
    y_sample = x_sample
    for l in range(DEPTH):
        p = (w_ada[l], b_ada[l], w_in[l], q_norm_a[l], k_norm_a[l], sink_b[l], w_br_a[l], w_br_b[l], w_o[l],
             ln1_g[l], ln1_b[l], w_ffn_gate[l], w_ffn_up[l], w_ffn_down[l], ln2_g[l], ln2_b[l])
        y_prompt = encoder_layer(y_prompt, c_prompt, *p)
        y_sample = encoder_layer(y_sample, c_sample, *p)
    return (y_prompt, y_sample)
```

```python
import functools
import math

import jax
import jax.numpy as jnp
from jax import lax
from jax.experimental import pallas as pl
from jax.experimental.pallas import tpu as pltpu

F32 = jnp.float32
BF16 = jnp.bfloat16

HEAD_DIM = 128
GRID_W = 64
Q_BLOCK = 128
WINDOW = 128
HALF_ROT = HEAD_DIM // 2
ROPE_THETA = 10000.0
N_MOD = 6
LN_EPS = 1e-5
RMS_EPS = 1e-6
KV_GROUP = 4
MASK_VALUE = -1e30

LANE = 128
VMEM_CAP_BYTES = 60 * 1024 * 1024
FF_QUANTUM = 1024


def _tile(pref, *dims):
    t = (min(pref, *dims) // LANE) * LANE
    while t >= LANE:
        if all(d % t == 0 for d in dims):
            return t
        t -= LANE
    return min(dims)


def _params(semantics, vmem_bytes):
    limit = int(min(max(vmem_bytes, 16 * 1024 * 1024), VMEM_CAP_BYTES))
    return pltpu.CompilerParams(dimension_semantics=semantics, vmem_limit_bytes=limit)


def _ln(x):
    mu = jnp.mean(x, axis=-1, keepdims=True)
    xc = x - mu
    return xc * lax.rsqrt(jnp.mean(xc * xc, axis=-1, keepdims=True) + LN_EPS)


def _ada_kernel(c_ref, w_ref, b_ref, o_ref):
    c = c_ref[...]
    a = (c * jax.nn.sigmoid(c)).astype(BF16)
    o_ref[...] = jnp.dot(a, w_ref[...].astype(BF16), preferred_element_type=F32) + b_ref[...]


def _ada(c_pad, w_ada, b_ada):
    rows, d = c_pad.shape
    n = w_ada.shape[1]
    tn = _tile(512, n)
    return pl.pallas_call(
        _ada_kernel,
        grid=(n // tn,),
        in_specs=[pl.BlockSpec((rows, d), lambda j: (0, 0)),
                  pl.BlockSpec((d, tn), lambda j: (0, j)),
                  pl.BlockSpec((1, tn), lambda j: (0, j))],
        out_specs=pl.BlockSpec((rows, tn), lambda j: (0, j)),
        out_shape=jax.ShapeDtypeStruct((rows, n), F32),
        compiler_params=_params(("parallel",), 2 * d * tn * 4 + d * tn * 2 + (4 << 20)),
        name="ada_mod",
    )(c_pad, w_ada, b_ada.reshape(1, n))


def _ln_mod_kernel(x_ref, sc_ref, sh_ref, o_ref):
    y = _ln(x_ref[...])
    o_ref[...] = (y * (1.0 + sc_ref[0]) + sh_ref[0]).astype(o_ref.dtype)


def _ln_mod(x2, mod3, t_seq, i_scale, i_shift):
    n, d = x2.shape
    tm = _tile(256, t_seq)
    per = t_seq // tm
    return pl.pallas_call(
        _ln_mod_kernel,
        grid=(n // tm,),
        in_specs=[pl.BlockSpec((tm, d), lambda i: (i, 0)),
                  pl.BlockSpec((1, 1, d), lambda i: ((i // per) * N_MOD + i_scale, 0, 0)),
                  pl.BlockSpec((1, 1, d), lambda i: ((i // per) * N_MOD + i_shift, 0, 0))],
        out_specs=pl.BlockSpec((tm, d), lambda i: (i, 0)),
        out_shape=jax.ShapeDtypeStruct((n, d), BF16),
        compiler_params=_params(("parallel",), 2 * tm * d * 6 + 4 * tm * d * 4),
        name="ln_mod",
    )(x2, mod3, mod3)


def _qk_kernel(h_ref, w_ref, g_ref, s_ref, cos_ref, sin_ref, o_ref):
    acc = jnp.dot(h_ref[...], w_ref[...], preferred_element_type=F32)
    tm, tn = acc.shape
    cos = cos_ref[...]
    sin = sin_ref[...]
    lane = lax.broadcasted_iota(jnp.int32, (tm, HEAD_DIM), 1)
    first_half = (lane % HALF_ROT) < (HALF_ROT // 2)
    for hh in range(tn // HEAD_DIM):
        sl = slice(hh * HEAD_DIM, (hh + 1) * HEAD_DIM)
        x = acc[:, sl]
        y = x * lax.rsqrt(jnp.mean(x * x, axis=-1, keepdims=True) + RMS_EPS) * g_ref[:, sl]
        partner = jnp.where(first_half,
                            pltpu.roll(y, HEAD_DIM - HALF_ROT // 2, 1),
                            pltpu.roll(y, HALF_ROT // 2, 1))
        o_ref[:, sl] = ((y * cos + partner * sin) * s_ref[:, sl]).astype(o_ref.dtype)


def _qk_proj(h, w_qk, gain, scale, cos_t, sin_t, t_seq):
    n, d = h.shape
    nw = w_qk.shape[1]
    tm = _tile(1024, t_seq)
    tn = _tile(512, nw)
    per = t_seq // tm
    return pl.pallas_call(
        _qk_kernel,
        grid=(n // tm, nw // tn),
        in_specs=[pl.BlockSpec((tm, d), lambda i, j: (i, 0)),
                  pl.BlockSpec((d, tn), lambda i, j: (0, j)),
                  pl.BlockSpec((1, tn), lambda i, j: (0, j)),
                  pl.BlockSpec((1, tn), lambda i, j: (0, j)),
                  pl.BlockSpec((tm, HEAD_DIM), lambda i, j: (i % per, 0)),
                  pl.BlockSpec((tm, HEAD_DIM), lambda i, j: (i % per, 0))],
        out_specs=pl.BlockSpec((tm, tn), lambda i, j: (i, j)),
        out_shape=jax.ShapeDtypeStruct((n, nw), BF16),
        compiler_params=_params(("parallel", "arbitrary"),
                                2 * (tm * d * 2 + d * tn * 2 + tm * tn * 2 + 2 * tm * HEAD_DIM * 4)
                                + 4 * tm * tn * 4),
        name="qk_proj_rope",
    )(h, w_qk, gain, scale, cos_t, sin_t)


def _mm_kernel(x_ref, w_ref, o_ref):
    o_ref[...] = jnp.dot(x_ref[...], w_ref[...], preferred_element_type=F32).astype(o_ref.dtype)


def _proj(x, w, tn_pref=512):
    n, d = x.shape
    nw = w.shape[1]
    tm = _tile(1024, n)
    tn = _tile(tn_pref, nw)
    return pl.pallas_call(
        _mm_kernel,
        grid=(n // tm, nw // tn),
        in_specs=[pl.BlockSpec((tm, d), lambda i, j: (i, 0)),
                  pl.BlockSpec((d, tn), lambda i, j: (0, j))],
        out_specs=pl.BlockSpec((tm, tn), lambda i, j: (i, j)),
        out_shape=jax.ShapeDtypeStruct((n, nw), BF16),
        compiler_params=_params(("parallel", "arbitrary"),
                                2 * (tm * d * 2 + d * tn * 2 + tm * tn * 2) + 2 * tm * tn * 4),
        name="proj",
    )(x, w)


def _flash_kernel(q_ref, k_ref, v_ref, o_ref, qs_ref, m_ref, l_ref, acc_ref, *, tq, tk, nk):
    for g in range(KV_GROUP):
        qs_ref[g * tq:(g + 1) * tq, :] = q_ref[:, g * HEAD_DIM:(g + 1) * HEAD_DIM]
    m_ref[...] = jnp.full(m_ref.shape, -jnp.inf, F32)
    l_ref[...] = jnp.zeros(l_ref.shape, F32)
    acc_ref[...] = jnp.zeros(acc_ref.shape, F32)

    def body(kk, carry):
        start = pl.multiple_of(kk * tk, tk)
        k = k_ref[pl.ds(start, tk), :]
        v = v_ref[pl.ds(start, tk), :]
        s = lax.dot_general(qs_ref[...], k, (((1,), (1,)), ((), ())), preferred_element_type=F32)
        m_prev = m_ref[...]
        m_new = jnp.maximum(m_prev, jnp.max(s, axis=-1, keepdims=True))
        alpha = jnp.exp(m_prev - m_new)
        p = jnp.exp(s - m_new)
        l_ref[...] = alpha * l_ref[...] + jnp.sum(p, axis=-1, keepdims=True)
        acc_ref[...] = alpha * acc_ref[...] + jnp.dot(p.astype(BF16), v, preferred_element_type=F32)
        m_ref[...] = m_new
        return carry

    lax.fori_loop(0, nk, body, 0)
    out = acc_ref[...] / l_ref[...]
    for g in range(KV_GROUP):
        o_ref[:, g * HEAD_DIM:(g + 1) * HEAD_DIM] = out[g * tq:(g + 1) * tq, :].astype(o_ref.dtype)


def _global_attention(qk, rest, batch, t_seq, n_q_heads, v_col_block):
    n = qk.shape[0]
    n_kv = n_q_heads // KV_GROUP
    tq = _tile(256, t_seq)
    tk = _tile(512, t_seq)
    nq = t_seq // tq
    gw = KV_GROUP * HEAD_DIM
    kern = functools.partial(_flash_kernel, tq=tq, tk=tk, nk=t_seq // tk)
    rows = KV_GROUP * tq
    return pl.pallas_call(
        kern,
        grid=(batch, n_kv, nq),
        in_specs=[pl.BlockSpec((tq, gw), lambda b, h, i: (b * nq + i, h)),
                  pl.BlockSpec((t_seq, HEAD_DIM), lambda b, h, i: (b, n_q_heads + h)),
                  pl.BlockSpec((t_seq, HEAD_DIM), lambda b, h, i: (b, v_col_block + h))],
        out_specs=pl.BlockSpec((tq, gw), lambda b, h, i: (b * nq + i, h)),
        out_shape=jax.ShapeDtypeStruct((n, n_q_heads * HEAD_DIM), BF16),
        scratch_shapes=[pltpu.VMEM((rows, HEAD_DIM), BF16),
                        pltpu.VMEM((rows, 1), F32),
                        pltpu.VMEM((rows, 1), F32),
                        pltpu.VMEM((rows, HEAD_DIM), F32)],
        compiler_params=_params(("parallel", "parallel", "arbitrary"),
                                2 * (2 * tq * gw * 2 + 2 * t_seq * HEAD_DIM * 2)
                                + rows * HEAD_DIM * 6 + 2 * rows * LANE * 4 + 6 * rows * tk * 4),
        name="global_attention",
    )(qk, qk, rest)


def _window_kernel(sink_ref, q_ref, kp_ref, kc_ref, kn_ref, vp_ref, vc_ref, vn_ref, o_ref, *,
                   nb, n_heads, slopes):
    blk = pl.program_id(1)
    tq = Q_BLOCK
    r = lax.broadcasted_iota(jnp.int32, (tq, 3 * tq), 0)
    c = lax.broadcasted_iota(jnp.int32, (tq, 3 * tq), 1)
    dist_i = jnp.abs(r - (c - tq))
    valid = (dist_i <= WINDOW) & ((c >= tq) | (blk > 0)) & ((c < 2 * tq) | (blk < nb - 1))
    dist = dist_i.astype(F32)
    scale = 1.0 / math.sqrt(HEAD_DIM)
    for kv in range(n_heads // KV_GROUP):
        ks = slice(kv * HEAD_DIM, (kv + 1) * HEAD_DIM)
        kcat = jnp.concatenate([kp_ref[:, ks], kc_ref[:, ks], kn_ref[:, ks]], axis=0)
        vcat = jnp.concatenate([vp_ref[:, ks], vc_ref[:, ks], vn_ref[:, ks]], axis=0)
        for g in range(KV_GROUP):
            hd = kv * KV_GROUP + g
            qs = slice(hd * HEAD_DIM, (hd + 1) * HEAD_DIM)
            s = lax.dot_general(q_ref[:, qs], kcat, (((1,), (1,)), ((), ())),
                                preferred_element_type=F32)
            s = jnp.where(valid, s * scale - slopes[hd] * dist, MASK_VALUE)
            sink = sink_ref[hd]
            m = jnp.maximum(jnp.max(s, axis=-1, keepdims=True), sink)
            e = jnp.exp(s - m)
            den = jnp.sum(e, axis=-1, keepdims=True) + jnp.exp(sink - m)
            p = (e / den).astype(BF16)
            o_ref[:, qs] = jnp.dot(p, vcat, preferred_element_type=F32).astype(o_ref.dtype)


def _window_attention(rest, sink, batch, t_seq, n_heads, k_col_block, v_col_block):
    n = rest.shape[0]
    nb = t_seq // Q_BLOCK
    qw = n_heads * HEAD_DIM
    kw = (n_heads // KV_GROUP) * HEAD_DIM
    slopes = tuple(float(2.0 ** (-8.0 * (i + 1) / n_heads)) for i in range(n_heads))
    kern = functools.partial(_window_kernel, nb=nb, n_heads=n_heads, slopes=slopes)

    def prev(b, i):
        return b * nb + jnp.maximum(i - 1, 0)

    def cur(b, i):
        return b * nb + i

    def nxt(b, i):
        return b * nb + jnp.minimum(i + 1, nb - 1)

    kv_specs = [pl.BlockSpec((Q_BLOCK, kw), functools.partial(lambda b, i, f, col: (f(b, i), col), f=f, col=col))
                for col in (k_col_block, v_col_block) for f in (prev, cur, nxt)]
    return pl.pallas_call(
        kern,
        grid=(batch, nb),
        in_specs=[pl.BlockSpec(memory_space=pltpu.SMEM),
                  pl.BlockSpec((Q_BLOCK, qw), lambda b, i: (b * nb + i, 0))] + kv_specs,
        out_specs=pl.BlockSpec((Q_BLOCK, qw), lambda b, i: (b * nb + i, 0)),
        out_shape=jax.ShapeDtypeStruct((n, qw), BF16),
        compiler_params=_params(("parallel", "arbitrary"), 32 * 1024 * 1024),
        name="window_attention",
    )(sink, rest, rest, rest, rest, rest, rest, rest)


def _merge_kernel(ya_ref, yb_ref, wa_ref, wb_ref, ga_ref, gb_ref, o_ref):
    a = jnp.dot(ya_ref[...], wa_ref[...], preferred_element_type=F32)
    b = jnp.dot(yb_ref[...], wb_ref[...], preferred_element_type=F32)
    ga = jax.nn.sigmoid(ga_ref[...].astype(F32))
    gb = jax.nn.sigmoid(gb_ref[...].astype(F32))
    o_ref[...] = (ga * a + gb * b).astype(o_ref.dtype)


def _merge(ya, yb, w_a, w_b, rest, gate_col):
    n, ka = ya.shape
    kb = yb.shape[1]
    d = w_a.shape[1]
    tm = _tile(1024, n)
    tn = _tile(512, d, gate_col)
    ga_blk = gate_col // tn
    gb_blk = (gate_col + d) // tn
    return pl.pallas_call(
        _merge_kernel,
        grid=(n // tm, d // tn),
        in_specs=[pl.BlockSpec((tm, ka), lambda i, j: (i, 0)),
                  pl.BlockSpec((tm, kb), lambda i, j: (i, 0)),
                  pl.BlockSpec((ka, tn), lambda i, j: (0, j)),
                  pl.BlockSpec((kb, tn), lambda i, j: (0, j)),
                  pl.BlockSpec((tm, tn), lambda i, j: (i, ga_blk + j)),
                  pl.BlockSpec((tm, tn), lambda i, j: (i, gb_blk + j))],
        out_specs=pl.BlockSpec((tm, tn), lambda i, j: (i, j)),
        out_shape=jax.ShapeDtypeStruct((n, d), BF16),
        compiler_params=_params(("parallel", "arbitrary"),
                                2 * (tm * (ka + kb) * 2 + (ka + kb) * tn * 2 + 3 * tm * tn * 2)
                                + 6 * tm * tn * 4),
        name="branch_merge",
    )(ya, yb, w_a, w_b, rest, rest)


def _resid_kernel(m_ref, w_ref, x_ref, g_ref, o_ref, *, alpha):
    acc = jnp.dot(m_ref[...], w_ref[...], preferred_element_type=F32)
    o_ref[...] = alpha * x_ref[...] + g_ref[0] * acc


def _out_proj(merged, w_o, x2, mod3, t_seq, i_gate, alpha):
    n, d = x2.shape
    k = merged.shape[1]
    tm = _tile(1024, t_seq)
    tn = _tile(512, d)
    per = t_seq // tm
    return pl.pallas_call(
        functools.partial(_resid_kernel, alpha=alpha),
        grid=(n // tm, d // tn),
        in_specs=[pl.BlockSpec((tm, k), lambda i, j: (i, 0)),
                  pl.BlockSpec((k, tn), lambda i, j: (0, j)),
                  pl.BlockSpec((tm, tn), lambda i, j: (i, j)),
                  pl.BlockSpec((1, 1, tn), lambda i, j: ((i // per) * N_MOD + i_gate, 0, j))],
        out_specs=pl.BlockSpec((tm, tn), lambda i, j: (i, j)),
        out_shape=jax.ShapeDtypeStruct((n, d), F32),
        compiler_params=_params(("parallel", "arbitrary"),
                                2 * (tm * k * 2 + k * tn * 2 + 2 * tm * tn * 4) + 2 * tm * tn * 4),
        name="out_proj_residual",
    )(merged, w_o, x2, mod3)


def _ln_pair_kernel(z_ref, g_ref, b_ref, sc_ref, sh_ref, x1_ref, h_ref):
    x1 = _ln(z_ref[...]) * g_ref[...] + b_ref[...]
    x1_ref[...] = x1
    h_ref[...] = (_ln(x1) * (1.0 + sc_ref[0]) + sh_ref[0]).astype(h_ref.dtype)


def _ln_pair(z, ln_g, ln_b, mod3, t_seq, i_scale, i_shift):
    n, d = z.shape
    tm = _tile(256, t_seq)
    per = t_seq // tm
    return pl.pallas_call(
        _ln_pair_kernel,
        grid=(n // tm,),
        in_specs=[pl.BlockSpec((tm, d), lambda i: (i, 0)),
                  pl.BlockSpec((1, d), lambda i: (0, 0)),
                  pl.BlockSpec((1, d), lambda i: (0, 0)),
                  pl.BlockSpec((1, 1, d), lambda i: ((i // per) * N_MOD + i_scale, 0, 0)),
                  pl.BlockSpec((1, 1, d), lambda i: ((i // per) * N_MOD + i_shift, 0, 0))],
        out_specs=[pl.BlockSpec((tm, d), lambda i: (i, 0)),
                   pl.BlockSpec((tm, d), lambda i: (i, 0))],
        out_shape=[jax.ShapeDtypeStruct((n, d), F32), jax.ShapeDtypeStruct((n, d), BF16)],
        compiler_params=_params(("parallel",), 2 * tm * d * 10 + 6 * tm * d * 4),
        name="ln1_ln2mod",
    )(z, ln_g.reshape(1, d), ln_b.reshape(1, d), mod3, mod3)


def _ffn_up_kernel(h_ref, w_ref, o_ref):
    acc = jnp.dot(h_ref[...], w_ref[...], preferred_element_type=F32)
    tn = o_ref.shape[1]
    g = acc[:, :tn]
    u = acc[:, tn:]
    o_ref[...] = (g * jax.nn.sigmoid(g) * u).astype(o_ref.dtype)


def _ffn_up(h, w_gu, tn):
    n, d = h.shape
    ffp = w_gu.shape[1] // 2
    tm = _tile(1024, n)
    return pl.pallas_call(
        _ffn_up_kernel,
        grid=(n // tm, ffp // tn),
        in_specs=[pl.BlockSpec((tm, d), lambda i, j: (i, 0)),
                  pl.BlockSpec((d, 2 * tn), lambda i, j: (0, j))],
        out_specs=pl.BlockSpec((tm, tn), lambda i, j: (i, j)),
        out_shape=jax.ShapeDtypeStruct((n, ffp), BF16),
        compiler_params=_params(("parallel", "arbitrary"),
                                2 * (tm * d * 2 + d * 2 * tn * 2 + tm * tn * 2) + 3 * tm * 2 * tn * 4),
        name="ffn_up",
    )(h, w_gu)


def _ffn_down_kernel(a_ref, w_ref, x_ref, gate_ref, g_ref, b_ref, o_ref, *, alpha, nk, rows):
    k = pl.program_id(1)
    part = jnp.dot(a_ref[...], w_ref[...], preferred_element_type=F32)

    @pl.when(k == 0)
    def _():
        o_ref[...] = part

    @pl.when(k > 0)
    def _():
        o_ref[...] += part

    @pl.when(k == nk - 1)
    def _():
        def chunk(r, carry):
            sl = pl.ds(pl.multiple_of(r * rows, rows), rows)
            z = alpha * x_ref[sl, :] + gate_ref[0] * o_ref[sl, :]
            o_ref[sl, :] = _ln(z) * g_ref[...] + b_ref[...]
            return carry

        lax.fori_loop(0, o_ref.shape[0] // rows, chunk, 0)


def _ffn_down(a, w_d, x1, ln_g, ln_b, mod3, t_seq, i_gate, alpha):
    n, ffp = a.shape
    d = w_d.shape[1]
    tm = _tile(512, t_seq)
    tk = _tile(512, ffp)
    nk = ffp // tk
    per = t_seq // tm
    return pl.pallas_call(
        functools.partial(_ffn_down_kernel, alpha=alpha, nk=nk, rows=_tile(LANE, tm)),
        grid=(n // tm, nk),
        in_specs=[pl.BlockSpec((tm, tk), lambda i, k: (i, k)),
                  pl.BlockSpec((tk, d), lambda i, k: (k, 0)),
                  pl.BlockSpec((tm, d), lambda i, k: (i, 0), pipeline_mode=pl.Buffered(1)),
                  pl.BlockSpec((1, 1, d), lambda i, k: ((i // per) * N_MOD + i_gate, 0, 0)),
                  pl.BlockSpec((1, d), lambda i, k: (0, 0)),
                  pl.BlockSpec((1, d), lambda i, k: (0, 0))],
        out_specs=pl.BlockSpec((tm, d), lambda i, k: (i, 0)),
        out_shape=jax.ShapeDtypeStruct((n, d), F32),
        compiler_params=_params(("parallel", "arbitrary"),
                                2 * (tm * tk * 2 + tk * d * 2 + tm * d * 4) + tm * d * 4
                                + 2 * tm * d * 4),
        name="ffn_down_ln",
    )(a, w_d, x1, mod3, ln_g.reshape(1, d), ln_b.reshape(1, d))


def _rope_tables(t_max):
    rows = t_max // GRID_W
    row = jnp.repeat(jnp.arange(rows, dtype=F32), GRID_W)
    col = jnp.tile(jnp.arange(GRID_W, dtype=F32), rows)
    inv = 1.0 / (ROPE_THETA ** (jnp.arange(0, HALF_ROT, 2, dtype=F32) / HALF_ROT))
    ang_r = row[:, None] * inv[None, :]
    ang_c = col[:, None] * inv[None, :]
    cr, sr, cc, sc = jnp.cos(ang_r), jnp.sin(ang_r), jnp.cos(ang_c), jnp.sin(ang_c)
    cos_t = jnp.concatenate([cr, cr, cc, cc], axis=-1)
    sin_t = jnp.concatenate([-sr, sr, -sc, sc], axis=-1)
    return cos_t, sin_t


def _prep_layer(w_in, q_norm, k_norm, w_br_a, w_br_b, w_o, w_gate, w_up, w_down):
    d = w_in.shape[0]
    n_heads = d // HEAD_DIM
    ha = n_heads // 2
    qa_w = ha * HEAD_DIM
    kv_w = (ha // KV_GROUP) * HEAD_DIM
    o_ka = qa_w
    o_va = o_ka + kv_w
    o_qb = o_va + kv_w
    o_kb = o_qb + qa_w
    o_vb = o_kb + kv_w
    o_ga = o_vb + kv_w
    w_qk = w_in[:, :o_va].astype(BF16)
    w_rest = jnp.concatenate([w_in[:, o_qb:o_kb], w_in[:, o_va:o_qb], w_in[:, o_kb:o_ga], w_in[:, o_ga:]],
                             axis=1).astype(BF16)
    gain = jnp.concatenate([jnp.tile(q_norm, ha), jnp.tile(k_norm, ha // KV_GROUP)]).reshape(1, -1)
    scale = jnp.concatenate([jnp.full((qa_w,), 1.0 / math.sqrt(HEAD_DIM), F32),
                             jnp.ones((kv_w,), F32)]).reshape(1, -1)
    ff = w_gate.shape[1]
    ffp = -(-ff // FF_QUANTUM) * FF_QUANTUM
    tn = _tile(512, ffp)
    pad = ((0, 0), (0, ffp - ff))
    wg = jnp.pad(w_gate, pad).astype(BF16).reshape(d, ffp // tn, 1, tn)
    wu = jnp.pad(w_up, pad).astype(BF16).reshape(d, ffp // tn, 1, tn)
    w_gu = jnp.concatenate([wg, wu], axis=2).reshape(d, 2 * ffp)
    w_d = jnp.pad(w_down, ((0, ffp - ff), (0, 0))).astype(BF16)
    return dict(w_qk=w_qk, w_rest=w_rest, gain=gain, scale=scale, w_a=w_br_a.astype(BF16),
                w_b=w_br_b.astype(BF16), w_o=w_o.astype(BF16), w_gu=w_gu, w_d=w_d, ff_tn=tn,
                ha=ha, kv_w=kv_w, qa_w=qa_w)


def _encoder_layer(x, mod, p, sink, ln1_g, ln1_b, ln2_g, ln2_b, cos_t, sin_t, alpha):
    b, t, d = x.shape
    x2 = x.reshape(b * t, d)
    mod3 = mod.reshape(b * N_MOD, 1, d)
    ha, kv_w, qa_w = p["ha"], p["kv_w"], p["qa_w"]
    kvh = kv_w // HEAD_DIM

    h = _ln_mod(x2, mod3, t, 1, 0)
    qk = _qk_proj(h, p["w_qk"], p["gain"], p["scale"], cos_t, sin_t, t)
    rest = _proj(h, p["w_rest"])
    ya = _global_attention(qk, rest, b, t, ha, v_col_block=ha)
    yb = _window_attention(rest, sink, b, t, ha, k_col_block=(qa_w + kv_w) // kv_w,
                           v_col_block=(qa_w + 2 * kv_w) // kv_w)
    merged = _merge(ya, yb, p["w_a"], p["w_b"], rest, gate_col=qa_w + 3 * kv_w)
    z = _out_proj(merged, p["w_o"], x2, mod3, t, 2, alpha)
    x1, h2 = _ln_pair(z, ln1_g, ln1_b, mod3, t, 4, 3)
    a = _ffn_up(h2, p["w_gu"], p["ff_tn"])
    y = _ffn_down(a, p["w_d"], x1, ln2_g, ln2_b, mod3, t, 5, alpha)
    del kvh
    return y.reshape(b, t, d)


def kernel(x_prompt, x_sample, c_prompt, c_sample, w_ada, b_ada, w_in, q_norm_a, k_norm_a, sink_b,
           w_br_a, w_br_b, w_o, ln1_g, ln1_b, w_ffn_gate, w_ffn_up, w_ffn_down, ln2_g, ln2_b):
    depth = w_ada.shape[0]
    alpha = float((2.0 * depth) ** 0.25)
    d = x_prompt.shape[-1]
    bp, bs = c_prompt.shape[0], c_sample.shape[0]
    rows = -(-(bp + bs) // 8) * 8
    cos_t, sin_t = _rope_tables(max(x_prompt.shape[1], x_sample.shape[1]))
    y_p, y_s = x_prompt, x_sample
    c_all = jnp.concatenate([c_prompt, c_sample, jnp.zeros((rows - bp - bs, d), F32)], axis=0)
    for l in range(depth):
        mod = _ada(c_all, w_ada[l], b_ada[l]).reshape(rows, N_MOD, d)
        p = _prep_layer(w_in[l], q_norm_a[l], k_norm_a[l], w_br_a[l], w_br_b[l], w_o[l],
                        w_ffn_gate[l], w_ffn_up[l], w_ffn_down[l])
        args = (p, sink_b[l], ln1_g[l], ln1_b[l], ln2_g[l], ln2_b[l], cos_t, sin_t, alpha)
        y_p = _encoder_layer(y_p, mod[:bp], *args)
        y_s = _encoder_layer(y_s, mod[bp:bp + bs], *args)
    return (y_p, y_s)
```

```python
import functools
import math

import jax
import jax.numpy as jnp
from jax import lax
from jax.experimental import pallas as pl
from jax.experimental.pallas import tpu as pltpu

F32 = jnp.float32
BF16 = jnp.bfloat16

HEAD_DIM = 128
GRID_W = 64
Q_BLOCK = 128
WINDOW = 128
HALF_ROT = HEAD_DIM // 2
ROPE_THETA = 10000.0
N_MOD = 6
LN_EPS = 1e-5
RMS_EPS = 1e-6
KV_GROUP = 4
MASK_VALUE = -1e30
LOG2_E = 1.4426950408889634

LANE = 128
VMEM_CAP_BYTES = 60 * 1024 * 1024
FF_QUANTUM = 1024


def _tile(pref, *dims):
    t = (min(pref, *dims) // LANE) * LANE
    while t >= LANE:
        if all(d % t == 0 for d in dims):
            return t
        t -= LANE
    return min(dims)


def _params(semantics, vmem_bytes):
    limit = int(min(max(vmem_bytes, 16 * 1024 * 1024), VMEM_CAP_BYTES))
    return pltpu.CompilerParams(dimension_semantics=semantics, vmem_limit_bytes=limit)


def _ln(x):
    mu = jnp.mean(x, axis=-1, keepdims=True)
    xc = x - mu
    return xc * lax.rsqrt(jnp.mean(xc * xc, axis=-1, keepdims=True) + LN_EPS)


def _ada_kernel(c_ref, w_ref, b_ref, o_ref):
    c = c_ref[...]
    a = (c * jax.nn.sigmoid(c)).astype(BF16)
    o_ref[...] = jnp.dot(a, w_ref[...].astype(BF16), preferred_element_type=F32) + b_ref[...]


def _ada(c_pad, w_ada, b_ada):
    rows, d = c_pad.shape
    n = w_ada.shape[1]
    tn = _tile(512, n)
    return pl.pallas_call(
        _ada_kernel,
        grid=(n // tn,),
        in_specs=[pl.BlockSpec((rows, d), lambda j: (0, 0)),
                  pl.BlockSpec((d, tn), lambda j: (0, j)),
                  pl.BlockSpec((1, tn), lambda j: (0, j))],
        out_specs=pl.BlockSpec((rows, tn), lambda j: (0, j)),
        out_shape=jax.ShapeDtypeStruct((rows, n), F32),
        compiler_params=_params(("parallel",), 2 * d * tn * 4 + d * tn * 2 + (4 << 20)),
        name="ada_mod",
    )(c_pad, w_ada, b_ada.reshape(1, n))


def _ln_mod_kernel(x_ref, sc_ref, sh_ref, o_ref):
    y = _ln(x_ref[...])
    o_ref[...] = (y * (1.0 + sc_ref[0]) + sh_ref[0]).astype(o_ref.dtype)


def _ln_mod(x2, mod3, t_seq, i_scale, i_shift):
    n, d = x2.shape
    tm = _tile(256, t_seq)
    per = t_seq // tm
    return pl.pallas_call(
        _ln_mod_kernel,
        grid=(n // tm,),
        in_specs=[pl.BlockSpec((tm, d), lambda i: (i, 0)),
                  pl.BlockSpec((1, 1, d), lambda i: ((i // per) * N_MOD + i_scale, 0, 0)),
                  pl.BlockSpec((1, 1, d), lambda i: ((i // per) * N_MOD + i_shift, 0, 0))],
        out_specs=pl.BlockSpec((tm, d), lambda i: (i, 0)),
        out_shape=jax.ShapeDtypeStruct((n, d), BF16),
        compiler_params=_params(("parallel",), 2 * tm * d * 6 + 4 * tm * d * 4),
        name="ln_mod",
    )(x2, mod3, mod3)


def _qk_kernel(h_ref, w_ref, g_ref, s_ref, cos_ref, sin_ref, o_ref):
    acc = jnp.dot(h_ref[...], w_ref[...], preferred_element_type=F32)
    tm, tn = acc.shape
    cos = cos_ref[...]
    sin = sin_ref[...]
    lane = lax.broadcasted_iota(jnp.int32, (tm, HEAD_DIM), 1)
    first_half = (lane % HALF_ROT) < (HALF_ROT // 2)
    for hh in range(tn // HEAD_DIM):
        sl = slice(hh * HEAD_DIM, (hh + 1) * HEAD_DIM)
        x = acc[:, sl]
        y = x * lax.rsqrt(jnp.mean(x * x, axis=-1, keepdims=True) + RMS_EPS) * g_ref[:, sl]
        partner = jnp.where(first_half,
                            pltpu.roll(y, HEAD_DIM - HALF_ROT // 2, 1),
                            pltpu.roll(y, HALF_ROT // 2, 1))
        o_ref[:, sl] = ((y * cos + partner * sin) * s_ref[:, sl]).astype(o_ref.dtype)


def _qk_proj(h, w_qk, gain, scale, cos_t, sin_t, t_seq):
    n, d = h.shape
    nw = w_qk.shape[1]
    tm = _tile(1024, t_seq)
    tn = _tile(512, nw)
    per = t_seq // tm
    return pl.pallas_call(
        _qk_kernel,
        grid=(n // tm, nw // tn),
        in_specs=[pl.BlockSpec((tm, d), lambda i, j: (i, 0)),
                  pl.BlockSpec((d, tn), lambda i, j: (0, j)),
                  pl.BlockSpec((1, tn), lambda i, j: (0, j)),
                  pl.BlockSpec((1, tn), lambda i, j: (0, j)),
                  pl.BlockSpec((tm, HEAD_DIM), lambda i, j: (i % per, 0)),
                  pl.BlockSpec((tm, HEAD_DIM), lambda i, j: (i % per, 0))],
        out_specs=pl.BlockSpec((tm, tn), lambda i, j: (i, j)),
        out_shape=jax.ShapeDtypeStruct((n, nw), BF16),
        compiler_params=_params(("parallel", "arbitrary"),
                                2 * (tm * d * 2 + d * tn * 2 + tm * tn * 2 + 2 * tm * HEAD_DIM * 4)
                                + 4 * tm * tn * 4),
        name="qk_proj_rope",
    )(h, w_qk, gain, scale, cos_t, sin_t)


def _mm_kernel(x_ref, w_ref, o_ref):
    o_ref[...] = jnp.dot(x_ref[...], w_ref[...], preferred_element_type=F32).astype(o_ref.dtype)


def _proj(x, w, tn_pref=512):
    n, d = x.shape
    nw = w.shape[1]
    tm = _tile(1024, n)
    tn = _tile(tn_pref, nw)
    return pl.pallas_call(
        _mm_kernel,
        grid=(n // tm, nw // tn),
        in_specs=[pl.BlockSpec((tm, d), lambda i, j: (i, 0)),
                  pl.BlockSpec((d, tn), lambda i, j: (0, j))],
        out_specs=pl.BlockSpec((tm, tn), lambda i, j: (i, j)),
        out_shape=jax.ShapeDtypeStruct((n, nw), BF16),
        compiler_params=_params(("parallel", "arbitrary"),
                                2 * (tm * d * 2 + d * tn * 2 + tm * tn * 2) + 2 * tm * tn * 4),
        name="proj",
    )(x, w)


def _mm_t_kernel(wt_ref, x_ref, o_ref):
    o_ref[...] = lax.dot_general(wt_ref[...], x_ref[...], (((1,), (1,)), ((), ())),
                                 preferred_element_type=F32).astype(o_ref.dtype)


def _proj_t(x, w_t):
    n, d = x.shape
    nw = w_t.shape[0]
    tm = _tile(1024, n)
    return pl.pallas_call(
        _mm_t_kernel,
        grid=(n // tm,),
        in_specs=[pl.BlockSpec((nw, d), lambda i: (0, 0)),
                  pl.BlockSpec((tm, d), lambda i: (i, 0))],
        out_specs=pl.BlockSpec((nw, tm), lambda i: (0, i)),
        out_shape=jax.ShapeDtypeStruct((nw, n), BF16),
        compiler_params=_params(("parallel",),
                                2 * (tm * d * 2 + nw * d * 2 + nw * tm * 2) + 2 * nw * tm * 4),
        name="proj_t",
    )(w_t, x)


def _flash_kernel(q_ref, k_ref, vt_ref, o_ref, qs_ref, m_ref, l_ref, acc_ref, *, tq, tk, nk, unroll):
    for g in range(KV_GROUP):
        qs_ref[g * tq:(g + 1) * tq, :] = q_ref[:, g * HEAD_DIM:(g + 1) * HEAD_DIM]
    m_ref[...] = jnp.full(m_ref.shape, -jnp.inf, F32)
    l_ref[...] = jnp.zeros(l_ref.shape, F32)
    acc_ref[...] = jnp.zeros(acc_ref.shape, F32)

    def body(kk, carry):
        start = pl.multiple_of(kk * tk, tk)
        k = k_ref[pl.ds(start, tk), :]
        vt = vt_ref[:, pl.ds(start, tk)]
        s = lax.dot_general(k, qs_ref[...], (((1,), (1,)), ((), ())), preferred_element_type=F32)
        m_prev = m_ref[...]
        m_new = jnp.maximum(m_prev, jnp.max(s, axis=0, keepdims=True))
        alpha = jnp.exp2(m_prev - m_new)
        p = jnp.exp2(s - m_new)
        l_ref[...] = alpha * l_ref[...] + jnp.sum(p, axis=0, keepdims=True)
        acc_ref[...] = alpha * acc_ref[...] + jnp.dot(vt, p.astype(BF16), preferred_element_type=F32)
        m_ref[...] = m_new
        return carry

    lax.fori_loop(0, nk, body, 0, unroll=unroll)
    out = (acc_ref[...] / l_ref[...]).T
    for g in range(KV_GROUP):
        o_ref[:, g * HEAD_DIM:(g + 1) * HEAD_DIM] = out[g * tq:(g + 1) * tq, :].astype(o_ref.dtype)


def _global_attention(qk, v_t, batch, t_seq, n_q_heads):
    n = qk.shape[0]
    n_kv = n_q_heads // KV_GROUP
    tq = _tile(256, t_seq)
    tk = _tile(512, t_seq)
    nq = t_seq // tq
    nk = t_seq // tk
    gw = KV_GROUP * HEAD_DIM
    kern = functools.partial(_flash_kernel, tq=tq, tk=tk, nk=nk, unroll=2 if nk % 2 == 0 else 1)
    rows = KV_GROUP * tq
    return pl.pallas_call(
        kern,
        grid=(batch, n_kv, nq),
        in_specs=[pl.BlockSpec((tq, gw), lambda b, h, i: (b * nq + i, h)),
                  pl.BlockSpec((t_seq, HEAD_DIM), lambda b, h, i: (b, n_q_heads + h)),
                  pl.BlockSpec((HEAD_DIM, t_seq), lambda b, h, i: (h, b))],
        out_specs=pl.BlockSpec((tq, gw), lambda b, h, i: (b * nq + i, h)),
        out_shape=jax.ShapeDtypeStruct((n, n_q_heads * HEAD_DIM), BF16),
        scratch_shapes=[pltpu.VMEM((rows, HEAD_DIM), BF16),
                        pltpu.VMEM((1, rows), F32),
                        pltpu.VMEM((1, rows), F32),
                        pltpu.VMEM((HEAD_DIM, rows), F32)],
        compiler_params=_params(("parallel", "parallel", "arbitrary"),
                                2 * (2 * tq * gw * 2 + 2 * t_seq * HEAD_DIM * 2)
                                + rows * HEAD_DIM * 6 + 16 * rows * 4 + 8 * rows * tk * 4),
        name="global_attention",
    )(qk, qk, v_t)


def _window_kernel(sink_ref, q_ref, kp_ref, kc_ref, kn_ref, vp_ref, vc_ref, vn_ref, o_ref, *,
                   nb, n_heads, slopes):
    blk = pl.program_id(1)
    tq = Q_BLOCK
    r = lax.broadcasted_iota(jnp.int32, (tq, 3 * tq), 0)
    c = lax.broadcasted_iota(jnp.int32, (tq, 3 * tq), 1)
    dist_i = jnp.abs(r - (c - tq))
    valid = (dist_i <= WINDOW) & ((c >= tq) | (blk > 0)) & ((c < 2 * tq) | (blk < nb - 1))
    dist = dist_i.astype(F32)
    scale = 1.0 / math.sqrt(HEAD_DIM)
    for kv in range(n_heads // KV_GROUP):
        ks = slice(kv * HEAD_DIM, (kv + 1) * HEAD_DIM)
        kcat = jnp.concatenate([kp_ref[:, ks], kc_ref[:, ks], kn_ref[:, ks]], axis=0)
        vcat = jnp.concatenate([vp_ref[:, ks], vc_ref[:, ks], vn_ref[:, ks]], axis=0)
        for g in range(KV_GROUP):
            hd = kv * KV_GROUP + g
            qs = slice(hd * HEAD_DIM, (hd + 1) * HEAD_DIM)
            s = lax.dot_general(q_ref[:, qs], kcat, (((1,), (1,)), ((), ())),
                                preferred_element_type=F32)
            s = jnp.where(valid, s * scale - slopes[hd] * dist, MASK_VALUE)
            sink = sink_ref[hd]
            m = jnp.maximum(jnp.max(s, axis=-1, keepdims=True), sink)
            e = jnp.exp(s - m)
            den = jnp.sum(e, axis=-1, keepdims=True) + jnp.exp(sink - m)
            p = (e / den).astype(BF16)
            o_ref[:, qs] = jnp.dot(p, vcat, preferred_element_type=F32).astype(o_ref.dtype)


def _window_attention(rest, sink, batch, t_seq, n_heads, k_col_block, v_col_block):
    n = rest.shape[0]
    nb = t_seq // Q_BLOCK
    qw = n_heads * HEAD_DIM
    kw = (n_heads // KV_GROUP) * HEAD_DIM
    slopes = tuple(float(2.0 ** (-8.0 * (i + 1) / n_heads)) for i in range(n_heads))
    kern = functools.partial(_window_kernel, nb=nb, n_heads=n_heads, slopes=slopes)

    def prev(b, i):
        return b * nb + jnp.maximum(i - 1, 0)

    def cur(b, i):
        return b * nb + i

    def nxt(b, i):
        return b * nb + jnp.minimum(i + 1, nb - 1)

    kv_specs = [pl.BlockSpec((Q_BLOCK, kw), functools.partial(lambda b, i, f, col: (f(b, i), col), f=f, col=col))
                for col in (k_col_block, v_col_block) for f in (prev, cur, nxt)]
    return pl.pallas_call(
        kern,
        grid=(batch, nb),
        in_specs=[pl.BlockSpec(memory_space=pltpu.SMEM),
                  pl.BlockSpec((Q_BLOCK, qw), lambda b, i: (b * nb + i, 0))] + kv_specs,
        out_specs=pl.BlockSpec((Q_BLOCK, qw), lambda b, i: (b * nb + i, 0)),
        out_shape=jax.ShapeDtypeStruct((n, qw), BF16),
        compiler_params=_params(("parallel", "arbitrary"), 32 * 1024 * 1024),
        name="window_attention",
    )(sink, rest, rest, rest, rest, rest, rest, rest)


def _merge_kernel(ya_ref, yb_ref, wa_ref, wb_ref, ga_ref, gb_ref, o_ref):
    a = jnp.dot(ya_ref[...], wa_ref[...], preferred_element_type=F32)
    b = jnp.dot(yb_ref[...], wb_ref[...], preferred_element_type=F32)
    ga = jax.nn.sigmoid(ga_ref[...].astype(F32))
    gb = jax.nn.sigmoid(gb_ref[...].astype(F32))
    o_ref[...] = (ga * a + gb * b).astype(o_ref.dtype)


def _merge(ya, yb, w_a, w_b, rest, gate_col):
    n, ka = ya.shape
    kb = yb.shape[1]
    d = w_a.shape[1]
    tm = _tile(1024, n)
    tn = _tile(512, d, gate_col)
    ga_blk = gate_col // tn
    gb_blk = (gate_col + d) // tn
    return pl.pallas_call(
        _merge_kernel,
        grid=(n // tm, d // tn),
        in_specs=[pl.BlockSpec((tm, ka), lambda i, j: (i, 0)),
                  pl.BlockSpec((tm, kb), lambda i, j: (i, 0)),
                  pl.BlockSpec((ka, tn), lambda i, j: (0, j)),
                  pl.BlockSpec((kb, tn), lambda i, j: (0, j)),
                  pl.BlockSpec((tm, tn), lambda i, j: (i, ga_blk + j)),
                  pl.BlockSpec((tm, tn), lambda i, j: (i, gb_blk + j))],
        out_specs=pl.BlockSpec((tm, tn), lambda i, j: (i, j)),
        out_shape=jax.ShapeDtypeStruct((n, d), BF16),
        compiler_params=_params(("parallel", "arbitrary"),
                                2 * (tm * (ka + kb) * 2 + (ka + kb) * tn * 2 + 3 * tm * tn * 2)
                                + 6 * tm * tn * 4),
        name="branch_merge",
    )(ya, yb, w_a, w_b, rest, rest)


def _resid_kernel(m_ref, w_ref, x_ref, g_ref, o_ref, *, alpha):
    acc = jnp.dot(m_ref[...], w_ref[...], preferred_element_type=F32)
    o_ref[...] = alpha * x_ref[...] + g_ref[0] * acc


def _out_proj(merged, w_o, x2, mod3, t_seq, i_gate, alpha):
    n, d = x2.shape
    k = merged.shape[1]
    tm = _tile(1024, t_seq)
    tn = _tile(512, d)
    per = t_seq // tm
    return pl.pallas_call(
        functools.partial(_resid_kernel, alpha=alpha),
        grid=(n // tm, d // tn),
        in_specs=[pl.BlockSpec((tm, k), lambda i, j: (i, 0)),
                  pl.BlockSpec((k, tn), lambda i, j: (0, j)),
                  pl.BlockSpec((tm, tn), lambda i, j: (i, j)),
                  pl.BlockSpec((1, 1, tn), lambda i, j: ((i // per) * N_MOD + i_gate, 0, j))],
        out_specs=pl.BlockSpec((tm, tn), lambda i, j: (i, j)),
        out_shape=jax.ShapeDtypeStruct((n, d), F32),
        compiler_params=_params(("parallel", "arbitrary"),
                                2 * (tm * k * 2 + k * tn * 2 + 2 * tm * tn * 4) + 2 * tm * tn * 4),
        name="out_proj_residual",
    )(merged, w_o, x2, mod3)


def _ln_pair_kernel(z_ref, g_ref, b_ref, sc_ref, sh_ref, x1_ref, h_ref):
    x1 = _ln(z_ref[...]) * g_ref[...] + b_ref[...]
    x1_ref[...] = x1
    h_ref[...] = (_ln(x1) * (1.0 + sc_ref[0]) + sh_ref[0]).astype(h_ref.dtype)


def _ln_pair(z, ln_g, ln_b, mod3, t_seq, i_scale, i_shift):
    n, d = z.shape
    tm = _tile(256, t_seq)
    per = t_seq // tm
    return pl.pallas_call(
        _ln_pair_kernel,
        grid=(n // tm,),
        in_specs=[pl.BlockSpec((tm, d), lambda i: (i, 0)),
                  pl.BlockSpec((1, d), lambda i: (0, 0)),
                  pl.BlockSpec((1, d), lambda i: (0, 0)),
                  pl.BlockSpec((1, 1, d), lambda i: ((i // per) * N_MOD + i_scale, 0, 0)),
                  pl.BlockSpec((1, 1, d), lambda i: ((i // per) * N_MOD + i_shift, 0, 0))],
        out_specs=[pl.BlockSpec((tm, d), lambda i: (i, 0)),
                   pl.BlockSpec((tm, d), lambda i: (i, 0))],
        out_shape=[jax.ShapeDtypeStruct((n, d), F32), jax.ShapeDtypeStruct((n, d), BF16)],
        compiler_params=_params(("parallel",), 2 * tm * d * 10 + 6 * tm * d * 4),
        name="ln1_ln2mod",
    )(z, ln_g.reshape(1, d), ln_b.reshape(1, d), mod3, mod3)


def _ffn_up_kernel(h_ref, wg_ref, wu_ref, o_ref):
    h = h_ref[...]
    g = jnp.dot(h, wg_ref[...], preferred_element_type=F32)
    u = jnp.dot(h, wu_ref[...], preferred_element_type=F32)
    o_ref[...] = (g * jax.nn.sigmoid(g) * u).astype(o_ref.dtype)


def _ffn_up(h, w_g, w_u):
    n, d = h.shape
    ffp = w_g.shape[1]
    tm = _tile(1024, n)
    tn = _tile(512, ffp)
    return pl.pallas_call(
        _ffn_up_kernel,
        grid=(n // tm, ffp // tn),
        in_specs=[pl.BlockSpec((tm, d), lambda i, j: (i, 0)),
                  pl.BlockSpec((d, tn), lambda i, j: (0, j)),
                  pl.BlockSpec((d, tn), lambda i, j: (0, j))],
        out_specs=pl.BlockSpec((tm, tn), lambda i, j: (i, j)),
        out_shape=jax.ShapeDtypeStruct((n, ffp), BF16),
        compiler_params=_params(("parallel", "arbitrary"),
                                2 * (tm * d * 2 + d * 2 * tn * 2 + tm * tn * 2) + 3 * tm * 2 * tn * 4),
        name="ffn_up",
    )(h, w_g, w_u)


def _ffn_down_kernel(a_ref, w_ref, x_ref, gate_ref, g_ref, b_ref, o_ref, *, alpha, nk, rows):
    k = pl.program_id(1)

    @pl.when(k == 0)
    def _():
        o_ref[...] = jnp.zeros(o_ref.shape, F32)

    o_ref[...] += jnp.dot(a_ref[...], w_ref[...], preferred_element_type=F32)

    @pl.when(k == nk - 1)
    def _():
        def chunk(r, carry):
            sl = pl.ds(pl.multiple_of(r * rows, rows), rows)
            z = alpha * x_ref[sl, :] + gate_ref[0] * o_ref[sl, :]
            o_ref[sl, :] = _ln(z) * g_ref[...] + b_ref[...]
            return carry

        lax.fori_loop(0, o_ref.shape[0] // rows, chunk, 0)


def _ffn_down(a, w_d, x1, ln_g, ln_b, mod3, t_seq, i_gate, alpha):
    n, ffp = a.shape
    d = w_d.shape[1]
    tm = _tile(512, t_seq)
    tk = _tile(1024, ffp)
    nk = ffp // tk
    per = t_seq // tm
    return pl.pallas_call(
        functools.partial(_ffn_down_kernel, alpha=alpha, nk=nk, rows=_tile(LANE, tm)),
        grid=(n // tm, nk),
        in_specs=[pl.BlockSpec((tm, tk), lambda i, k: (i, k)),
                  pl.BlockSpec((tk, d), lambda i, k: (k, 0)),
                  pl.BlockSpec((tm, d), lambda i, k: (i, 0), pipeline_mode=pl.Buffered(1)),
                  pl.BlockSpec((1, 1, d), lambda i, k: ((i // per) * N_MOD + i_gate, 0, 0)),
                  pl.BlockSpec((1, d), lambda i, k: (0, 0)),
                  pl.BlockSpec((1, d), lambda i, k: (0, 0))],
        out_specs=pl.BlockSpec((tm, d), lambda i, k: (i, 0)),
        out_shape=jax.ShapeDtypeStruct((n, d), F32),
        compiler_params=_params(("parallel", "arbitrary"),
                                2 * (tm * tk * 2 + tk * d * 2 + tm * d * 4) + tm * d * 4
                                + 2 * tm * d * 4),
        name="ffn_down_ln",
    )(a, w_d, x1, mod3, ln_g.reshape(1, d), ln_b.reshape(1, d))


def _rope_tables(t_max):
    rows = t_max // GRID_W
    row = jnp.repeat(jnp.arange(rows, dtype=F32), GRID_W)
    col = jnp.tile(jnp.arange(GRID_W, dtype=F32), rows)
    inv = 1.0 / (ROPE_THETA ** (jnp.arange(0, HALF_ROT, 2, dtype=F32) / HALF_ROT))
    ang_r = row[:, None] * inv[None, :]
    ang_c = col[:, None] * inv[None, :]
    cr, sr, cc, sc = jnp.cos(ang_r), jnp.sin(ang_r), jnp.cos(ang_c), jnp.sin(ang_c)
    cos_t = jnp.concatenate([cr, cr, cc, cc], axis=-1)
    sin_t = jnp.concatenate([-sr, sr, -sc, sc], axis=-1)
    return cos_t, sin_t


def _prep_layer(w_in, q_norm, k_norm, w_br_a, w_br_b, w_o, w_gate, w_up, w_down):
    d = w_in.shape[0]
    n_heads = d // HEAD_DIM
    ha = n_heads // 2
    qa_w = ha * HEAD_DIM
    kv_w = (ha // KV_GROUP) * HEAD_DIM
    o_ka = qa_w
    o_va = o_ka + kv_w
    o_qb = o_va + kv_w
    o_kb = o_qb + qa_w
    o_vb = o_kb + kv_w
    o_ga = o_vb + kv_w
    w_qk = w_in[:, :o_va].astype(BF16)
    w_va_t = w_in[:, o_va:o_qb].T.astype(BF16)
    w_rest = w_in[:, o_qb:].astype(BF16)
    gain = jnp.concatenate([jnp.tile(q_norm, ha), jnp.tile(k_norm, ha // KV_GROUP)]).reshape(1, -1)
    scale = jnp.concatenate([jnp.full((qa_w,), LOG2_E / math.sqrt(HEAD_DIM), F32),
                             jnp.ones((kv_w,), F32)]).reshape(1, -1)
    ff = w_gate.shape[1]
    ffp = -(-ff // FF_QUANTUM) * FF_QUANTUM
    pad = ((0, 0), (0, ffp - ff))
    w_g = jnp.pad(w_gate, pad).astype(BF16)
    w_u = jnp.pad(w_up, pad).astype(BF16)
    w_d = jnp.pad(w_down, ((0, ffp - ff), (0, 0))).astype(BF16)
    return dict(w_qk=w_qk, w_va_t=w_va_t, w_rest=w_rest, gain=gain, scale=scale,
                w_a=w_br_a.astype(BF16), w_b=w_br_b.astype(BF16), w_o=w_o.astype(BF16),
                w_g=w_g, w_u=w_u, w_d=w_d, ha=ha, kv_w=kv_w, qa_w=qa_w)


def _encoder_layer(x, mod, p, sink, ln1_g, ln1_b, ln2_g, ln2_b, cos_t, sin_t, alpha):
    b, t, d = x.shape
    x2 = x.reshape(b * t, d)
    mod3 = mod.reshape(b * N_MOD, 1, d)
    ha, kv_w, qa_w = p["ha"], p["kv_w"], p["qa_w"]

    h = _ln_mod(x2, mod3, t, 1, 0)
    qk = _qk_proj(h, p["w_qk"], p["gain"], p["scale"], cos_t, sin_t, t)
    va_t = _proj_t(h, p["w_va_t"])
    rest = _proj(h, p["w_rest"])
    ya = _global_attention(qk, va_t, b, t, ha)
    yb = _window_attention(rest, sink, b, t, ha, k_col_block=qa_w // kv_w,
                           v_col_block=(qa_w + kv_w) // kv_w)
    merged = _merge(ya, yb, p["w_a"], p["w_b"], rest, gate_col=qa_w + 2 * kv_w)
    z = _out_proj(merged, p["w_o"], x2, mod3, t, 2, alpha)
    x1, h2 = _ln_pair(z, ln1_g, ln1_b, mod3, t, 4, 3)
    a = _ffn_up(h2, p["w_g"], p["w_u"])
    y = _ffn_down(a, p["w_d"], x1, ln2_g, ln2_b, mod3, t, 5, alpha)
    return y.reshape(b, t, d)


def kernel(x_prompt, x_sample, c_prompt, c_sample, w_ada, b_ada, w_in, q_norm_a, k_norm_a, sink_b,
           w_br_a, w_br_b, w_o, ln1_g, ln1_b, w_ffn_gate, w_ffn_up, w_ffn_down, ln2_g, ln2_b):
    depth = w_ada.shape[0]
    alpha = float((2.0 * depth) ** 0.25)
    d = x_prompt.shape[-1]
    bp, bs = c_prompt.shape[0], c_sample.shape[0]
    rows = -(-(bp + bs) // 8) * 8
    cos_t, sin_t = _rope_tables(max(x_prompt.shape[1], x_sample.shape[1]))
    y_p, y_s = x_prompt, x_sample
    c_all = jnp.concatenate([c_prompt, c_sample, jnp.zeros((rows - bp - bs, d), F32)], axis=0)
    for l in range(depth):
        mod = _ada(c_all, w_ada[l], b_ada[l]).reshape(rows, N_MOD, d)
        p = _prep_layer(w_in[l], q_norm_a[l], k_norm_a[l], w_br_a[l], w_br_b[l], w_o[l],
                        w_ffn_gate[l], w_ffn_up[l], w_ffn_down[l])
        args = (p, sink_b[l], ln1_g[l], ln1_b[l], ln2_g[l], ln2_b[l], cos_t, sin_t, alpha)
        y_p = _encoder_layer(y_p, mod[:bp], *args)
        y_s = _encoder_layer(y_s, mod[bp:bp + bs], *args)
    return (y_p, y_s)
```

```python
import functools
import math

import jax
import jax.numpy as jnp
import numpy as np
from jax import lax
from jax.experimental import pallas as pl
from jax.experimental.pallas import tpu as pltpu

F32 = jnp.float32
BF16 = jnp.bfloat16

HEAD_DIM = 128
GRID_W = 64
Q_BLOCK = 128
WINDOW = 128
HALF_ROT = HEAD_DIM // 2
ROPE_THETA = 10000.0
N_MOD = 6
LN_EPS = 1e-5
RMS_EPS = 1e-6
KV_GROUP = 4
MASK_VALUE = -1e30
LOG2_E = 1.4426950408889634
_Q4 = HALF_ROT // 2
_ROT_PERM = np.array(list(range(0, _Q4)) + list(range(2 * _Q4, 3 * _Q4))
                     + list(range(_Q4, 2 * _Q4)) + list(range(3 * _Q4, 4 * _Q4)))

LANE = 128
VMEM_CAP_BYTES = 60 * 1024 * 1024
FF_QUANTUM = 1024


def _tile(pref, *dims):
    t = (min(pref, *dims) // LANE) * LANE
    while t >= LANE:
        if all(d % t == 0 for d in dims):
            return t
        t -= LANE
    return min(dims)


def _params(semantics, vmem_bytes):
    limit = int(min(max(vmem_bytes, 16 * 1024 * 1024), VMEM_CAP_BYTES))
    return pltpu.CompilerParams(dimension_semantics=semantics, vmem_limit_bytes=limit)


def _ln(x):
    mu = jnp.mean(x, axis=-1, keepdims=True)
    xc = x - mu
    return xc * lax.rsqrt(jnp.mean(xc * xc, axis=-1, keepdims=True) + LN_EPS)


def _ada_kernel(c_ref, w_ref, b_ref, o_ref):
    c = c_ref[...]
    a = (c * jax.nn.sigmoid(c)).astype(BF16)
    o_ref[...] = jnp.dot(a, w_ref[...].astype(BF16), preferred_element_type=F32) + b_ref[...]


def _ada(c_pad, w_ada, b_ada):
    rows, d = c_pad.shape
    n = w_ada.shape[1]
    tn = _tile(512, n)
    return pl.pallas_call(
        _ada_kernel,
        grid=(n // tn,),
        in_specs=[pl.BlockSpec((rows, d), lambda j: (0, 0)),
                  pl.BlockSpec((d, tn), lambda j: (0, j)),
                  pl.BlockSpec((1, tn), lambda j: (0, j))],
        out_specs=pl.BlockSpec((rows, tn), lambda j: (0, j)),
        out_shape=jax.ShapeDtypeStruct((rows, n), F32),
        compiler_params=_params(("parallel",), 2 * d * tn * 4 + d * tn * 2 + (4 << 20)),
        name="ada_mod",
    )(c_pad, w_ada, b_ada.reshape(1, n))


def _ln_mod_kernel(x_ref, sc_ref, sh_ref, o_ref):
    y = _ln(x_ref[...])
    o_ref[...] = (y * (1.0 + sc_ref[0]) + sh_ref[0]).astype(o_ref.dtype)


def _ln_mod(x2, mod3, t_seq, i_scale, i_shift):
    n, d = x2.shape
    tm = _tile(256, t_seq)
    per = t_seq // tm
    return pl.pallas_call(
        _ln_mod_kernel,
        grid=(n // tm,),
        in_specs=[pl.BlockSpec((tm, d), lambda i: (i, 0)),
                  pl.BlockSpec((1, 1, d), lambda i: ((i // per) * N_MOD + i_scale, 0, 0)),
                  pl.BlockSpec((1, 1, d), lambda i: ((i // per) * N_MOD + i_shift, 0, 0))],
        out_specs=pl.BlockSpec((tm, d), lambda i: (i, 0)),
        out_shape=jax.ShapeDtypeStruct((n, d), BF16),
        compiler_params=_params(("parallel",), 2 * tm * d * 6 + 4 * tm * d * 4),
        name="ln_mod",
    )(x2, mod3, mod3)


def _qk_kernel(h_ref, w_ref, g_ref, s_ref, cos_ref, sin_ref, o_ref, *, chunk):
    h = h_ref[...]
    cos = cos_ref[...]
    sin = sin_ref[...]
    for c in range(o_ref.shape[1] // chunk):
        acc = jnp.dot(h, w_ref[:, c * chunk:(c + 1) * chunk], preferred_element_type=F32)
        for hh in range(chunk // HEAD_DIM):
            sl = slice(c * chunk + hh * HEAD_DIM, c * chunk + (hh + 1) * HEAD_DIM)
            x = acc[:, hh * HEAD_DIM:(hh + 1) * HEAD_DIM]
            y = x * lax.rsqrt(jnp.mean(x * x, axis=-1, keepdims=True) + RMS_EPS) * g_ref[:, sl]
            partner = pltpu.roll(y, HALF_ROT, 1)
            o_ref[:, sl] = ((y * cos + partner * sin) * s_ref[:, sl]).astype(o_ref.dtype)


def _qk_proj(h, w_qk, gain, scale, cos_t, sin_t, t_seq):
    n, d = h.shape
    nw = w_qk.shape[1]
    tm = _tile(512, t_seq)
    per = t_seq // tm
    return pl.pallas_call(
        functools.partial(_qk_kernel, chunk=_tile(256, nw)),
        grid=(n // tm,),
        in_specs=[pl.BlockSpec((tm, d), lambda i: (i, 0)),
                  pl.BlockSpec((d, nw), lambda i: (0, 0), pipeline_mode=pl.Buffered(1)),
                  pl.BlockSpec((1, nw), lambda i: (0, 0)),
                  pl.BlockSpec((1, nw), lambda i: (0, 0)),
                  pl.BlockSpec((tm, HEAD_DIM), lambda i: (i % per, 0)),
                  pl.BlockSpec((tm, HEAD_DIM), lambda i: (i % per, 0))],
        out_specs=pl.BlockSpec((tm, nw), lambda i: (i, 0)),
        out_shape=jax.ShapeDtypeStruct((n, nw), BF16),
        compiler_params=_params(("parallel",),
                                d * nw * 2 + 2 * (tm * d * 2 + tm * nw * 2 + 2 * tm * HEAD_DIM * 4)
                                + 16 * tm * 256 * 4),
        name="qk_proj_rope",
    )(h, w_qk, gain, scale, cos_t, sin_t)


def _mm_kernel(x_ref, w_ref, s_ref, o_ref):
    acc = jnp.dot(x_ref[...], w_ref[...], preferred_element_type=F32)
    o_ref[...] = (acc * s_ref[...]).astype(o_ref.dtype)


def _proj(x, w, col_scale, tn_pref=512):
    n, d = x.shape
    nw = w.shape[1]
    tm = _tile(1024, n)
    tn = _tile(tn_pref, nw)
    return pl.pallas_call(
        _mm_kernel,
        grid=(n // tm, nw // tn),
        in_specs=[pl.BlockSpec((tm, d), lambda i, j: (i, 0)),
                  pl.BlockSpec((d, tn), lambda i, j: (0, j)),
                  pl.BlockSpec((1, tn), lambda i, j: (0, j))],
        out_specs=pl.BlockSpec((tm, tn), lambda i, j: (i, j)),
        out_shape=jax.ShapeDtypeStruct((n, nw), BF16),
        compiler_params=_params(("parallel", "arbitrary"),
                                2 * (tm * d * 2 + d * tn * 2 + tm * tn * 2) + 2 * tm * tn * 4),
        name="proj",
    )(x, w, col_scale)


def _mm_t_kernel(wt_ref, x_ref, o_ref):
    o_ref[...] = lax.dot_general(wt_ref[...], x_ref[...], (((1,), (1,)), ((), ())),
                                 preferred_element_type=F32).astype(o_ref.dtype)


def _proj_t(x, w_t):
    n, d = x.shape
    nw = w_t.shape[0]
    tm = _tile(1024, n)
    return pl.pallas_call(
        _mm_t_kernel,
        grid=(n // tm,),
        in_specs=[pl.BlockSpec((nw, d), lambda i: (0, 0)),
                  pl.BlockSpec((tm, d), lambda i: (i, 0))],
        out_specs=pl.BlockSpec((nw, tm), lambda i: (0, i)),
        out_shape=jax.ShapeDtypeStruct((nw, n), BF16),
        compiler_params=_params(("parallel",),
                                2 * (tm * d * 2 + nw * d * 2 + nw * tm * 2) + 2 * nw * tm * 4),
        name="proj_t",
    )(w_t, x)


def _flash_kernel(q_ref, k_ref, vt_ref, o_ref, qs_ref, sa_ref, sb_ref, m_ref, l_ref, acc_ref, *, tq, tk, nk):
    for g in range(KV_GROUP):
        qs_ref[g * tq:(g + 1) * tq, :] = q_ref[:, g * HEAD_DIM:(g + 1) * HEAD_DIM]
    m_ref[...] = jnp.full(m_ref.shape, -jnp.inf, F32)
    l_ref[...] = jnp.zeros(l_ref.shape, F32)
    acc_ref[...] = jnp.zeros(acc_ref.shape, F32)

    def scores(idx, s_ref):
        start = pl.multiple_of(idx * tk, tk)
        s_ref[...] = lax.dot_general(k_ref[pl.ds(start, tk), :], qs_ref[...], (((1,), (1,)), ((), ())),
                                     preferred_element_type=F32)

    def update(idx, s_ref):
        start = pl.multiple_of(idx * tk, tk)
        s = s_ref[...]
        m_prev = m_ref[...]
        m_new = jnp.maximum(m_prev, jnp.max(s, axis=0, keepdims=True))
        alpha = jnp.exp2(m_prev - m_new)
        p = jnp.exp2(s - m_new)
        l_ref[...] = alpha * l_ref[...] + jnp.sum(p, axis=0, keepdims=True)
        acc_ref[...] = alpha * acc_ref[...] + jnp.dot(vt_ref[:, pl.ds(start, tk)], p.astype(BF16),
                                                      preferred_element_type=F32)
        m_ref[...] = m_new

    scores(0, sa_ref)
    pairs = (nk - 1) // 2

    def body(j, carry):
        scores(2 * j + 1, sb_ref)
        update(2 * j, sa_ref)
        scores(2 * j + 2, sa_ref)
        update(2 * j + 1, sb_ref)
        return carry

    lax.fori_loop(0, pairs, body, 0)
    if nk - 2 * pairs == 2:
        scores(nk - 1, sb_ref)
        update(nk - 2, sa_ref)
        update(nk - 1, sb_ref)
    else:
        update(nk - 1, sa_ref)
    out = (acc_ref[...] / l_ref[...]).T
    for g in range(KV_GROUP):
        o_ref[:, g * HEAD_DIM:(g + 1) * HEAD_DIM] = out[g * tq:(g + 1) * tq, :].astype(o_ref.dtype)


def _global_attention(qk, v_t, batch, t_seq, n_q_heads):
    n = qk.shape[0]
    n_kv = n_q_heads // KV_GROUP
    tq = _tile(256, t_seq)
    tk = _tile(512, t_seq)
    nq = t_seq // tq
    nk = t_seq // tk
    gw = KV_GROUP * HEAD_DIM
    kern = functools.partial(_flash_kernel, tq=tq, tk=tk, nk=nk)
    rows = KV_GROUP * tq
    return pl.pallas_call(
        kern,
        grid=(batch, n_kv, nq),
        in_specs=[pl.BlockSpec((tq, gw), lambda b, h, i: (b * nq + i, h)),
                  pl.BlockSpec((t_seq, HEAD_DIM), lambda b, h, i: (b, n_q_heads + h)),
                  pl.BlockSpec((HEAD_DIM, t_seq), lambda b, h, i: (h, b))],
        out_specs=pl.BlockSpec((tq, gw), lambda b, h, i: (b * nq + i, h)),
        out_shape=jax.ShapeDtypeStruct((n, n_q_heads * HEAD_DIM), BF16),
        scratch_shapes=[pltpu.VMEM((rows, HEAD_DIM), BF16),
                        pltpu.VMEM((tk, rows), F32),
                        pltpu.VMEM((tk, rows), F32),
                        pltpu.VMEM((1, rows), F32),
                        pltpu.VMEM((1, rows), F32),
                        pltpu.VMEM((HEAD_DIM, rows), F32)],
        compiler_params=_params(("parallel", "parallel", "arbitrary"),
                                2 * (2 * tq * gw * 2 + 2 * t_seq * HEAD_DIM * 2)
                                + rows * HEAD_DIM * 6 + 16 * rows * 4 + 8 * rows * tk * 4),
        name="global_attention",
    )(qk, qk, v_t)


def _window_bias(n_heads):
    key = jnp.arange(3 * Q_BLOCK, dtype=jnp.int32)[:, None] - Q_BLOCK
    qpos = jnp.arange(Q_BLOCK, dtype=jnp.int32)[None, :]
    dist = jnp.abs(qpos - key)
    slopes = 2.0 ** (-8.0 * jnp.arange(1, n_heads + 1, dtype=F32) / n_heads)
    bias = -slopes[:, None, None] * dist.astype(F32)[None] * LOG2_E
    n_kv = n_heads // KV_GROUP
    bias = bias.reshape(n_kv, KV_GROUP, 3 * Q_BLOCK, Q_BLOCK).transpose(0, 2, 1, 3)
    bias = bias.reshape(n_kv, 3 * Q_BLOCK, KV_GROUP * Q_BLOCK)
    in_window = (dist <= WINDOW)
    in_window = jnp.tile(in_window, (1, KV_GROUP))[None]
    row = jnp.arange(3 * Q_BLOCK)[None, :, None]
    not_prev = row >= Q_BLOCK
    not_next = row < 2 * Q_BLOCK
    variants = [in_window, in_window & not_prev, in_window & not_next, in_window & not_prev & not_next]
    return jnp.stack([jnp.where(v, bias, MASK_VALUE) for v in variants])


def _window_kernel(q_ref, kp_ref, kc_ref, kn_ref, vp_ref, vc_ref, vn_ref, bias_ref, sink_ref, o_ref, *,
                   n_kv):
    gw = KV_GROUP * HEAD_DIM
    for kv in range(n_kv):
        ks = slice(kv * HEAD_DIM, (kv + 1) * HEAD_DIM)
        kcat = jnp.concatenate([kp_ref[:, ks], kc_ref[:, ks], kn_ref[:, ks]], axis=0)
        vtcat = jnp.concatenate([vp_ref[ks, :], vc_ref[ks, :], vn_ref[ks, :]], axis=1)
        qs = jnp.concatenate([q_ref[:, (kv * KV_GROUP + g) * HEAD_DIM:(kv * KV_GROUP + g + 1) * HEAD_DIM]
                              for g in range(KV_GROUP)], axis=0)
        s = lax.dot_general(kcat, qs, (((1,), (1,)), ((), ())), preferred_element_type=F32)
        s = s + bias_ref[0, kv]
        sink = sink_ref[:, kv * gw:(kv + 1) * gw] * LOG2_E
        m = jnp.maximum(jnp.max(s, axis=0, keepdims=True), sink)
        p = jnp.exp2(s - m)
        den = jnp.sum(p, axis=0, keepdims=True) + jnp.exp2(sink - m)
        pv = jnp.dot(vtcat, p.astype(BF16), preferred_element_type=F32)
        out = (pv / den).T
        for g in range(KV_GROUP):
            hd = kv * KV_GROUP + g
            o_ref[:, hd * HEAD_DIM:(hd + 1) * HEAD_DIM] = out[g * Q_BLOCK:(g + 1) * Q_BLOCK, :].astype(o_ref.dtype)


def _window_attention(rest, v_t, bias, sink_row, batch, t_seq, n_heads, k_col_block, vt_row_block):
    n = rest.shape[0]
    nb = t_seq // Q_BLOCK
    n_kv = n_heads // KV_GROUP
    qw = n_heads * HEAD_DIM
    kw = n_kv * HEAD_DIM

    def prev(b, i):
        return b * nb + jnp.maximum(i - 1, 0)

    def cur(b, i):
        return b * nb + i

    def nxt(b, i):
        return b * nb + jnp.minimum(i + 1, nb - 1)

    def variant(b, i):
        return (i == 0).astype(jnp.int32) + 2 * (i == nb - 1).astype(jnp.int32)

    k_specs = [pl.BlockSpec((Q_BLOCK, kw), functools.partial(lambda b, i, f: (f(b, i), k_col_block), f=f))
               for f in (prev, cur, nxt)]
    v_specs = [pl.BlockSpec((kw, Q_BLOCK), functools.partial(lambda b, i, f: (vt_row_block, f(b, i)), f=f))
               for f in (prev, cur, nxt)]
    return pl.pallas_call(
        functools.partial(_window_kernel, n_kv=n_kv),
        grid=(batch, nb),
        in_specs=[pl.BlockSpec((Q_BLOCK, qw), lambda b, i: (b * nb + i, 0))] + k_specs + v_specs
        + [pl.BlockSpec((1,) + bias.shape[1:], lambda b, i: (variant(b, i), 0, 0, 0)),
           pl.BlockSpec((1, qw), lambda b, i: (0, 0))],
        out_specs=pl.BlockSpec((Q_BLOCK, qw), lambda b, i: (b * nb + i, 0)),
        out_shape=jax.ShapeDtypeStruct((n, qw), BF16),
        compiler_params=_params(("parallel", "arbitrary"),
                                2 * (bias[0].size * 4 + 2 * Q_BLOCK * qw * 2 + 12 * Q_BLOCK * kw)
                                + 16 * 3 * Q_BLOCK * KV_GROUP * Q_BLOCK * 4),
        name="window_attention",
    )(rest, rest, rest, rest, v_t, v_t, v_t, bias, sink_row)


def _merge_kernel(ya_ref, yb_ref, wa_ref, wb_ref, ga_ref, gb_ref, o_ref):
    a = jnp.dot(ya_ref[...], wa_ref[...], preferred_element_type=F32)
    b = jnp.dot(yb_ref[...], wb_ref[...], preferred_element_type=F32)
    ga = jax.nn.sigmoid(ga_ref[...].astype(F32))
    gb = jax.nn.sigmoid(gb_ref[...].astype(F32))
    o_ref[...] = (ga * a + gb * b).astype(o_ref.dtype)


def _merge(ya, yb, w_a, w_b, rest, gate_col):
    n, ka = ya.shape
    kb = yb.shape[1]
    d = w_a.shape[1]
    tm = _tile(1024, n)
    tn = _tile(512, d, gate_col)
    ga_blk = gate_col // tn
    gb_blk = (gate_col + d) // tn
    return pl.pallas_call(
        _merge_kernel,
        grid=(n // tm, d // tn),
        in_specs=[pl.BlockSpec((tm, ka), lambda i, j: (i, 0)),
                  pl.BlockSpec((tm, kb), lambda i, j: (i, 0)),
                  pl.BlockSpec((ka, tn), lambda i, j: (0, j)),
                  pl.BlockSpec((kb, tn), lambda i, j: (0, j)),
                  pl.BlockSpec((tm, tn), lambda i, j: (i, ga_blk + j)),
                  pl.BlockSpec((tm, tn), lambda i, j: (i, gb_blk + j))],
        out_specs=pl.BlockSpec((tm, tn), lambda i, j: (i, j)),
        out_shape=jax.ShapeDtypeStruct((n, d), BF16),
        compiler_params=_params(("parallel", "arbitrary"),
                                2 * (tm * (ka + kb) * 2 + (ka + kb) * tn * 2 + 3 * tm * tn * 2)
                                + 6 * tm * tn * 4),
        name="branch_merge",
    )(ya, yb, w_a, w_b, rest, rest)


def _resid_kernel(m_ref, w_ref, x_ref, g_ref, o_ref, *, alpha):
    acc = jnp.dot(m_ref[...], w_ref[...], preferred_element_type=F32)
    o_ref[...] = alpha * x_ref[...] + g_ref[0] * acc


def _out_proj(merged, w_o, x2, mod3, t_seq, i_gate, alpha):
    n, d = x2.shape
    k = merged.shape[1]
    tm = _tile(1024, t_seq)
    tn = _tile(512, d)
    per = t_seq // tm
    return pl.pallas_call(
        functools.partial(_resid_kernel, alpha=alpha),
        grid=(n // tm, d // tn),
        in_specs=[pl.BlockSpec((tm, k), lambda i, j: (i, 0)),
                  pl.BlockSpec((k, tn), lambda i, j: (0, j)),
                  pl.BlockSpec((tm, tn), lambda i, j: (i, j)),
                  pl.BlockSpec((1, 1, tn), lambda i, j: ((i // per) * N_MOD + i_gate, 0, j))],
        out_specs=pl.BlockSpec((tm, tn), lambda i, j: (i, j)),
        out_shape=jax.ShapeDtypeStruct((n, d), F32),
        compiler_params=_params(("parallel", "arbitrary"),
                                2 * (tm * k * 2 + k * tn * 2 + 2 * tm * tn * 4) + 2 * tm * tn * 4),
        name="out_proj_residual",
    )(merged, w_o, x2, mod3)


def _ln_pair_kernel(z_ref, g_ref, b_ref, sc_ref, sh_ref, x1_ref, h_ref):
    x1 = _ln(z_ref[...]) * g_ref[...] + b_ref[...]
    x1_ref[...] = x1
    h_ref[...] = (_ln(x1) * (1.0 + sc_ref[0]) + sh_ref[0]).astype(h_ref.dtype)


def _ln_pair(z, ln_g, ln_b, mod3, t_seq, i_scale, i_shift):
    n, d = z.shape
    tm = _tile(256, t_seq)
    per = t_seq // tm
    return pl.pallas_call(
        _ln_pair_kernel,
        grid=(n // tm,),
        in_specs=[pl.BlockSpec((tm, d), lambda i: (i, 0)),
                  pl.BlockSpec((1, d), lambda i: (0, 0)),
                  pl.BlockSpec((1, d), lambda i: (0, 0)),
                  pl.BlockSpec((1, 1, d), lambda i: ((i // per) * N_MOD + i_scale, 0, 0)),
                  pl.BlockSpec((1, 1, d), lambda i: ((i // per) * N_MOD + i_shift, 0, 0))],
        out_specs=[pl.BlockSpec((tm, d), lambda i: (i, 0)),
                   pl.BlockSpec((tm, d), lambda i: (i, 0))],
        out_shape=[jax.ShapeDtypeStruct((n, d), F32), jax.ShapeDtypeStruct((n, d), BF16)],
        compiler_params=_params(("parallel",), 2 * tm * d * 10 + 6 * tm * d * 4),
        name="ln1_ln2mod",
    )(z, ln_g.reshape(1, d), ln_b.reshape(1, d), mod3, mod3)


def _ffn_up_kernel(h_ref, wg_ref, wu_ref, o_ref):
    h = h_ref[...]
    g = jnp.dot(h, wg_ref[...], preferred_element_type=F32)
    u = jnp.dot(h, wu_ref[...], preferred_element_type=F32)
    o_ref[...] = (g * jax.nn.sigmoid(g) * u).astype(o_ref.dtype)


def _ffn_up(h, w_g, w_u):
    n, d = h.shape
    ffp = w_g.shape[1]
    tm = _tile(1024, n)
    tn = _tile(512, ffp)
    return pl.pallas_call(
        _ffn_up_kernel,
        grid=(n // tm, ffp // tn),
        in_specs=[pl.BlockSpec((tm, d), lambda i, j: (i, 0)),
                  pl.BlockSpec((d, tn), lambda i, j: (0, j)),
                  pl.BlockSpec((d, tn), lambda i, j: (0, j))],
        out_specs=pl.BlockSpec((tm, tn), lambda i, j: (i, j)),
        out_shape=jax.ShapeDtypeStruct((n, ffp), BF16),
        compiler_params=_params(("parallel", "arbitrary"),
                                2 * (tm * d * 2 + d * 2 * tn * 2 + tm * tn * 2) + 3 * tm * 2 * tn * 4),
        name="ffn_up",
    )(h, w_g, w_u)


def _ffn_down_kernel(a_ref, w_ref, x_ref, gate_ref, g_ref, b_ref, o_ref, *, alpha, nk, rows):
    k = pl.program_id(1)

    @pl.when(k == 0)
    def _():
        o_ref[...] = jnp.zeros(o_ref.shape, F32)

    o_ref[...] += jnp.dot(a_ref[...], w_ref[...], preferred_element_type=F32)

    @pl.when(k == nk - 1)
    def _():
        def chunk(r, carry):
            sl = pl.ds(pl.multiple_of(r * rows, rows), rows)
            z = alpha * x_ref[sl, :] + gate_ref[0] * o_ref[sl, :]
            o_ref[sl, :] = _ln(z) * g_ref[...] + b_ref[...]
            return carry

        lax.fori_loop(0, o_ref.shape[0] // rows, chunk, 0)


def _ffn_down(a, w_d, x1, ln_g, ln_b, mod3, t_seq, i_gate, alpha):
    n, ffp = a.shape
    d = w_d.shape[1]
    tm = _tile(512, t_seq)
    tk = _tile(1024, ffp)
    nk = ffp // tk
    per = t_seq // tm
    return pl.pallas_call(
        functools.partial(_ffn_down_kernel, alpha=alpha, nk=nk, rows=_tile(LANE, tm)),
        grid=(n // tm, nk),
        in_specs=[pl.BlockSpec((tm, tk), lambda i, k: (i, k)),
                  pl.BlockSpec((tk, d), lambda i, k: (k, 0)),
                  pl.BlockSpec((tm, d), lambda i, k: (i, 0), pipeline_mode=pl.Buffered(1)),
                  pl.BlockSpec((1, 1, d), lambda i, k: ((i // per) * N_MOD + i_gate, 0, 0)),
                  pl.BlockSpec((1, d), lambda i, k: (0, 0)),
                  pl.BlockSpec((1, d), lambda i, k: (0, 0))],
        out_specs=pl.BlockSpec((tm, d), lambda i, k: (i, 0)),
        out_shape=jax.ShapeDtypeStruct((n, d), F32),
        compiler_params=_params(("parallel", "arbitrary"),
                                2 * (tm * tk * 2 + tk * d * 2 + tm * d * 4) + tm * d * 4
                                + 2 * tm * d * 4),
        name="ffn_down_ln",
    )(a, w_d, x1, mod3, ln_g.reshape(1, d), ln_b.reshape(1, d))


def _rope_tables(t_max):
    rows = t_max // GRID_W
    row = jnp.repeat(jnp.arange(rows, dtype=F32), GRID_W)
    col = jnp.tile(jnp.arange(GRID_W, dtype=F32), rows)
    inv = 1.0 / (ROPE_THETA ** (jnp.arange(0, HALF_ROT, 2, dtype=F32) / HALF_ROT))
    ang_r = row[:, None] * inv[None, :]
    ang_c = col[:, None] * inv[None, :]
    cr, sr, cc, sc = jnp.cos(ang_r), jnp.sin(ang_r), jnp.cos(ang_c), jnp.sin(ang_c)
    cos_t = jnp.concatenate([cr, cc, cr, cc], axis=-1)
    sin_t = jnp.concatenate([-sr, -sc, sr, sc], axis=-1)
    return cos_t, sin_t


def _prep_layer(w_in, q_norm, k_norm, w_br_a, w_br_b, w_o, w_gate, w_up, w_down):
    d = w_in.shape[0]
    n_heads = d // HEAD_DIM
    ha = n_heads // 2
    qa_w = ha * HEAD_DIM
    kv_w = (ha // KV_GROUP) * HEAD_DIM
    o_ka = qa_w
    o_va = o_ka + kv_w
    o_qb = o_va + kv_w
    o_kb = o_qb + qa_w
    o_vb = o_kb + kv_w
    o_ga = o_vb + kv_w
    w_qk = w_in[:, :o_va].reshape(d, -1, HEAD_DIM)[:, :, _ROT_PERM].reshape(d, o_va).astype(BF16)
    w_v_t = jnp.concatenate([w_in[:, o_va:o_qb], w_in[:, o_vb:o_ga]], axis=1).T.astype(BF16)
    w_rest = jnp.concatenate([w_in[:, o_qb:o_vb], w_in[:, o_ga:]], axis=1).astype(BF16)
    gain = jnp.concatenate([jnp.tile(q_norm[_ROT_PERM], ha),
                            jnp.tile(k_norm[_ROT_PERM], ha // KV_GROUP)]).reshape(1, -1)
    q_scale = jnp.full((qa_w,), LOG2_E / math.sqrt(HEAD_DIM), F32)
    scale = jnp.concatenate([q_scale, jnp.ones((kv_w,), F32)]).reshape(1, -1)
    rest_scale = jnp.concatenate([q_scale, jnp.ones((w_rest.shape[1] - qa_w,), F32)]).reshape(1, -1)
    ff = w_gate.shape[1]
    ffp = -(-ff // FF_QUANTUM) * FF_QUANTUM
    pad = ((0, 0), (0, ffp - ff))
    w_g = jnp.pad(w_gate, pad).astype(BF16)
    w_u = jnp.pad(w_up, pad).astype(BF16)
    w_d = jnp.pad(w_down, ((0, ffp - ff), (0, 0))).astype(BF16)
    return dict(w_qk=w_qk, w_v_t=w_v_t, w_rest=w_rest, gain=gain, scale=scale, rest_scale=rest_scale,
                w_a=w_br_a.astype(BF16), w_b=w_br_b.astype(BF16), w_o=w_o.astype(BF16),
                w_g=w_g, w_u=w_u, w_d=w_d, ha=ha, kv_w=kv_w, qa_w=qa_w)


def _encoder_layer(x, mod, p, sink_row, win_bias, ln1_g, ln1_b, ln2_g, ln2_b, cos_t, sin_t, alpha):
    b, t, d = x.shape
    x2 = x.reshape(b * t, d)
    mod3 = mod.reshape(b * N_MOD, 1, d)
    ha, kv_w, qa_w = p["ha"], p["kv_w"], p["qa_w"]

    h = _ln_mod(x2, mod3, t, 1, 0)
    qk = _qk_proj(h, p["w_qk"], p["gain"], p["scale"], cos_t, sin_t, t)
    v_t = _proj_t(h, p["w_v_t"])
    rest = _proj(h, p["w_rest"], p["rest_scale"])
    ya = _global_attention(qk, v_t, b, t, ha)
    yb = _window_attention(rest, v_t, win_bias, sink_row, b, t, ha, k_col_block=qa_w // kv_w,
                           vt_row_block=1)
    merged = _merge(ya, yb, p["w_a"], p["w_b"], rest, gate_col=qa_w + kv_w)
    z = _out_proj(merged, p["w_o"], x2, mod3, t, 2, alpha)
    x1, h2 = _ln_pair(z, ln1_g, ln1_b, mod3, t, 4, 3)
    a = _ffn_up(h2, p["w_g"], p["w_u"])
    y = _ffn_down(a, p["w_d"], x1, ln2_g, ln2_b, mod3, t, 5, alpha)
    return y.reshape(b, t, d)


def kernel(x_prompt, x_sample, c_prompt, c_sample, w_ada, b_ada, w_in, q_norm_a, k_norm_a, sink_b,
           w_br_a, w_br_b, w_o, ln1_g, ln1_b, w_ffn_gate, w_ffn_up, w_ffn_down, ln2_g, ln2_b):
    depth = w_ada.shape[0]
    alpha = float((2.0 * depth) ** 0.25)
    d = x_prompt.shape[-1]
    bp, bs = c_prompt.shape[0], c_sample.shape[0]
    rows = -(-(bp + bs) // 8) * 8
    cos_t, sin_t = _rope_tables(max(x_prompt.shape[1], x_sample.shape[1]))
    y_p, y_s = x_prompt, x_sample
    c_all = jnp.concatenate([c_prompt, c_sample, jnp.zeros((rows - bp - bs, d), F32)], axis=0)
    win_bias = _window_bias(sink_b.shape[1])
    for l in range(depth):
        mod = _ada(c_all, w_ada[l], b_ada[l]).reshape(rows, N_MOD, d)
        p = _prep_layer(w_in[l], q_norm_a[l], k_norm_a[l], w_br_a[l], w_br_b[l], w_o[l],
                        w_ffn_gate[l], w_ffn_up[l], w_ffn_down[l])
        sink_row = jnp.repeat(sink_b[l], HEAD_DIM).reshape(1, -1)
        args = (p, sink_row, win_bias, ln1_g[l], ln1_b[l], ln2_g[l], ln2_b[l], cos_t, sin_t, alpha)
        y_p = _encoder_layer(y_p, mod[:bp], *args)
        y_s = _encoder_layer(y_s, mod[bp:bp + bs], *args)
    return (y_p, y_s)
```

```python
import functools
import math

import jax
import jax.numpy as jnp
import numpy as np
from jax import lax
from jax.experimental import pallas as pl
from jax.experimental.pallas import tpu as pltpu

F32 = jnp.float32
BF16 = jnp.bfloat16

HEAD_DIM = 128
GRID_W = 64
Q_BLOCK = 128
WINDOW = 128
HALF_ROT = HEAD_DIM // 2
ROPE_THETA = 10000.0
N_MOD = 6
LN_EPS = 1e-5
RMS_EPS = 1e-6
KV_GROUP = 4
MASK_VALUE = -1e30
LOG2_E = 1.4426950408889634
_Q4 = HALF_ROT // 2
_ROT_PERM = np.array(list(range(0, _Q4)) + list(range(2 * _Q4, 3 * _Q4))
                     + list(range(_Q4, 2 * _Q4)) + list(range(3 * _Q4, 4 * _Q4)))

ONES_ROWS = 16
V_ROWS = HEAD_DIM + ONES_ROWS
LANE = 128
VMEM_CAP_BYTES = 60 * 1024 * 1024
CAST_BLOCK_BYTES = 4 * 1024 * 1024


def _tile(pref, *dims):
    t = (min(pref, *dims) // LANE) * LANE
    while t >= LANE:
        if all(d % t == 0 for d in dims):
            return t
        t -= LANE
    return min(dims)


def _params(semantics, vmem_bytes):
    limit = int(min(max(vmem_bytes, 16 * 1024 * 1024), VMEM_CAP_BYTES))
    return pltpu.CompilerParams(dimension_semantics=semantics, vmem_limit_bytes=limit)


def _ln(x):
    mu = jnp.mean(x, axis=-1, keepdims=True)
    xc = x - mu
    return xc * lax.rsqrt(jnp.mean(xc * xc, axis=-1, keepdims=True) + LN_EPS)


def _ada_kernel(c_ref, w_ref, b_ref, o_ref):
    c = c_ref[...]
    a = (c * jax.nn.sigmoid(c)).astype(BF16)
    o_ref[...] = jnp.dot(a, w_ref[...].astype(BF16), preferred_element_type=F32) + b_ref[...]


def _ada(c_pad, w_ada, b_ada):
    rows, d = c_pad.shape
    n = w_ada.shape[1]
    tn = _tile(512, n)
    return pl.pallas_call(
        _ada_kernel,
        grid=(n // tn,),
        in_specs=[pl.BlockSpec((rows, d), lambda j: (0, 0)),
                  pl.BlockSpec((d, tn), lambda j: (0, j)),
                  pl.BlockSpec((1, tn), lambda j: (0, j))],
        out_specs=pl.BlockSpec((rows, tn), lambda j: (0, j)),
        out_shape=jax.ShapeDtypeStruct((rows, n), F32),
        compiler_params=_params(("parallel",), 2 * d * tn * 4 + d * tn * 2 + (4 << 20)),
        name="ada_mod",
    )(c_pad, w_ada, b_ada.reshape(1, n))


def _ln_mod_kernel(x_ref, sc_ref, sh_ref, o_ref):
    y = _ln(x_ref[...])
    o_ref[...] = (y * (1.0 + sc_ref[0]) + sh_ref[0]).astype(o_ref.dtype)


def _ln_mod(x2, mod3, t_seq, i_scale, i_shift):
    n, d = x2.shape
    tm = _tile(256, t_seq)
    per = t_seq // tm
    return pl.pallas_call(
        _ln_mod_kernel,
        grid=(n // tm,),
        in_specs=[pl.BlockSpec((tm, d), lambda i: (i, 0)),
                  pl.BlockSpec((1, 1, d), lambda i: ((i // per) * N_MOD + i_scale, 0, 0)),
                  pl.BlockSpec((1, 1, d), lambda i: ((i // per) * N_MOD + i_shift, 0, 0))],
        out_specs=pl.BlockSpec((tm, d), lambda i: (i, 0)),
        out_shape=jax.ShapeDtypeStruct((n, d), BF16),
        compiler_params=_params(("parallel",), 2 * tm * d * 6 + 4 * tm * d * 4),
        name="ln_mod",
    )(x2, mod3, mod3)


def _qk_kernel(h_ref, w_ref, g_ref, s_ref, cos_ref, sin_ref, o_ref, *, chunk):
    h = h_ref[...]
    cos = cos_ref[...]
    sin = sin_ref[...]
    for c in range(o_ref.shape[1] // chunk):
        acc = jnp.dot(h, w_ref[:, c * chunk:(c + 1) * chunk], preferred_element_type=F32)
        for hh in range(chunk // HEAD_DIM):
            sl = slice(c * chunk + hh * HEAD_DIM, c * chunk + (hh + 1) * HEAD_DIM)
            x = acc[:, hh * HEAD_DIM:(hh + 1) * HEAD_DIM]
            y = x * lax.rsqrt(jnp.mean(x * x, axis=-1, keepdims=True) + RMS_EPS) * g_ref[:, sl]
            partner = pltpu.roll(y, HALF_ROT, 1)
            o_ref[:, sl] = ((y * cos + partner * sin) * s_ref[:, sl]).astype(o_ref.dtype)


def _qk_proj(h, w_qk, gain, scale, cos_t, sin_t, t_seq):
    n, d = h.shape
    nw = w_qk.shape[1]
    tm = _tile(512, t_seq)
    per = t_seq // tm
    return pl.pallas_call(
        functools.partial(_qk_kernel, chunk=_tile(256, nw)),
        grid=(n // tm,),
        in_specs=[pl.BlockSpec((tm, d), lambda i: (i, 0)),
                  pl.BlockSpec((d, nw), lambda i: (0, 0), pipeline_mode=pl.Buffered(1)),
                  pl.BlockSpec((1, nw), lambda i: (0, 0)),
                  pl.BlockSpec((1, nw), lambda i: (0, 0)),
                  pl.BlockSpec((tm, HEAD_DIM), lambda i: (i % per, 0)),
                  pl.BlockSpec((tm, HEAD_DIM), lambda i: (i % per, 0))],
        out_specs=pl.BlockSpec((tm, nw), lambda i: (i, 0)),
        out_shape=jax.ShapeDtypeStruct((n, nw), BF16),
        compiler_params=_params(("parallel",),
                                d * nw * 2 + 2 * (tm * d * 2 + tm * nw * 2 + 2 * tm * HEAD_DIM * 4)
                                + 16 * tm * 256 * 4),
        name="qk_proj_rope",
    )(h, w_qk, gain, scale, cos_t, sin_t)


def _cast_kernel(w_ref, o_ref):
    o_ref[...] = w_ref[...].astype(o_ref.dtype)


def _cast_cols(w, col0=0, ncols=None):
    rows, width = w.shape
    ncols = width if ncols is None else ncols
    if col0 == 0 and ncols == width:
        tc = width
    else:
        tc = _tile(2048, ncols, *((col0,) if col0 else ()))
    tr = max(8, min(rows, (CAST_BLOCK_BYTES // (4 * tc)) // 8 * 8))
    while rows % tr:
        tr -= 8
    c0 = col0 // tc
    return pl.pallas_call(
        _cast_kernel,
        grid=(rows // tr, ncols // tc),
        in_specs=[pl.BlockSpec((tr, tc), lambda i, j: (i, c0 + j))],
        out_specs=pl.BlockSpec((tr, tc), lambda i, j: (i, j)),
        out_shape=jax.ShapeDtypeStruct((rows, ncols), BF16),
        compiler_params=_params(("parallel", "parallel"), 2 * tr * tc * 6 + tr * tc * 4),
        name="cast_bf16",
    )(w)


def _mm_scaled_kernel(x_ref, w_ref, s_ref, o_ref):
    acc = jnp.dot(x_ref[...], w_ref[...], preferred_element_type=F32)
    o_ref[...] = (acc * s_ref[...]).astype(o_ref.dtype)


def _mm_kernel(x_ref, w_ref, o_ref):
    o_ref[...] = jnp.dot(x_ref[...], w_ref[...], preferred_element_type=F32).astype(o_ref.dtype)


def _proj(x, w, col_scale, tn_pref=512):
    n, d = x.shape
    nw = w.shape[1]
    tm = _tile(1024, n)
    tn = _tile(tn_pref, nw)
    scaled = col_scale is not None
    in_specs = [pl.BlockSpec((tm, d), lambda i, j: (i, 0)),
                pl.BlockSpec((d, tn), lambda i, j: (0, j))]
    if scaled:
        in_specs.append(pl.BlockSpec((1, tn), lambda i, j: (0, j)))
    return pl.pallas_call(
        _mm_scaled_kernel if scaled else _mm_kernel,
        grid=(n // tm, nw // tn),
        in_specs=in_specs,
        out_specs=pl.BlockSpec((tm, tn), lambda i, j: (i, j)),
        out_shape=jax.ShapeDtypeStruct((n, nw), BF16),
        compiler_params=_params(("parallel", "arbitrary"),
                                2 * (tm * d * 2 + d * tn * 2 + tm * tn * 2) + 2 * tm * tn * 4),
        name="proj",
    )(*((x, w, col_scale) if scaled else (x, w)))


def _mm_t_kernel(wt_ref, x_ref, oa_ref, ob_ref):
    res = lax.dot_general(wt_ref[...], x_ref[...], (((1,), (1,)), ((), ())), preferred_element_type=F32)
    nb = ob_ref.shape[0]
    tm = x_ref.shape[0]
    for j in range(oa_ref.shape[0] // V_ROWS):
        oa_ref[j * V_ROWS:j * V_ROWS + HEAD_DIM, :] = res[j * HEAD_DIM:(j + 1) * HEAD_DIM, :].astype(oa_ref.dtype)
        oa_ref[j * V_ROWS + HEAD_DIM:(j + 1) * V_ROWS, :] = jnp.ones((ONES_ROWS, tm), oa_ref.dtype)
    ob_ref[...] = res[res.shape[0] - nb:, :].astype(ob_ref.dtype)


def _proj_t(x, w_t, n_a_heads):
    n, d = x.shape
    nw = w_t.shape[0]
    nb = nw - n_a_heads * HEAD_DIM
    tm = _tile(1024, n)
    return pl.pallas_call(
        _mm_t_kernel,
        grid=(n // tm,),
        in_specs=[pl.BlockSpec((nw, d), lambda i: (0, 0)),
                  pl.BlockSpec((tm, d), lambda i: (i, 0))],
        out_specs=[pl.BlockSpec((n_a_heads * V_ROWS, tm), lambda i: (0, i)),
                   pl.BlockSpec((nb, tm), lambda i: (0, i))],
        out_shape=[jax.ShapeDtypeStruct((n_a_heads * V_ROWS, n), BF16),
                   jax.ShapeDtypeStruct((nb, n), BF16)],
        compiler_params=_params(("parallel",),
                                2 * (tm * d * 2 + nw * d * 2 + 2 * nw * tm * 2) + 2 * nw * tm * 4),
        name="proj_t",
    )(w_t, x)


def _flash_kernel(q_ref, k_ref, vt_ref, o_ref, qs_ref, sa_ref, sb_ref, m_ref, acc_ref, *, tq, tk, nk):
    for g in range(KV_GROUP):
        qs_ref[g * tq:(g + 1) * tq, :] = q_ref[:, g * HEAD_DIM:(g + 1) * HEAD_DIM]
    m_ref[...] = jnp.full(m_ref.shape, -jnp.inf, F32)
    acc_ref[...] = jnp.zeros(acc_ref.shape, F32)

    def scores(idx, s_ref):
        start = pl.multiple_of(idx * tk, tk)
        s_ref[...] = lax.dot_general(k_ref[pl.ds(start, tk), :], qs_ref[...], (((1,), (1,)), ((), ())),
                                     preferred_element_type=F32)

    def update(idx, s_ref):
        start = pl.multiple_of(idx * tk, tk)
        s = s_ref[...]
        m_prev = m_ref[...]
        m_new = jnp.maximum(m_prev, jnp.max(s, axis=0, keepdims=True))
        alpha = jnp.exp2(m_prev - m_new)
        p = jnp.exp2(s - m_new).astype(BF16)
        acc_ref[...] = alpha * acc_ref[...] + jnp.dot(vt_ref[:, pl.ds(start, tk)], p,
                                                      preferred_element_type=F32)
        m_ref[...] = m_new

    scores(0, sa_ref)
    pairs = (nk - 1) // 2

    def body(j, carry):
        scores(2 * j + 1, sb_ref)
        update(2 * j, sa_ref)
        scores(2 * j + 2, sa_ref)
        update(2 * j + 1, sb_ref)
        return carry

    lax.fori_loop(0, pairs, body, 0)
    if nk - 2 * pairs == 2:
        scores(nk - 1, sb_ref)
        update(nk - 2, sa_ref)
        update(nk - 1, sb_ref)
    else:
        update(nk - 1, sa_ref)
    out = (acc_ref[:HEAD_DIM, :] / acc_ref[HEAD_DIM:HEAD_DIM + 1, :]).T
    for g in range(KV_GROUP):
        o_ref[:, g * HEAD_DIM:(g + 1) * HEAD_DIM] = out[g * tq:(g + 1) * tq, :].astype(o_ref.dtype)


def _global_attention(qk, v_t, batch, t_seq, n_q_heads):
    n = qk.shape[0]
    n_kv = n_q_heads // KV_GROUP
    tq = _tile(256, t_seq)
    tk = _tile(512, t_seq)
    nq = t_seq // tq
    nk = t_seq // tk
    gw = KV_GROUP * HEAD_DIM
    kern = functools.partial(_flash_kernel, tq=tq, tk=tk, nk=nk)
    rows = KV_GROUP * tq
    return pl.pallas_call(
        kern,
        grid=(batch, n_kv, nq),
        in_specs=[pl.BlockSpec((tq, gw), lambda b, h, i: (b * nq + i, h)),
                  pl.BlockSpec((t_seq, HEAD_DIM), lambda b, h, i: (b, n_q_heads + h)),
                  pl.BlockSpec((V_ROWS, t_seq), lambda b, h, i: (h, b))],
        out_specs=pl.BlockSpec((tq, gw), lambda b, h, i: (b * nq + i, h)),
        out_shape=jax.ShapeDtypeStruct((n, n_q_heads * HEAD_DIM), BF16),
        scratch_shapes=[pltpu.VMEM((rows, HEAD_DIM), BF16),
                        pltpu.VMEM((tk, rows), F32),
                        pltpu.VMEM((tk, rows), F32),
                        pltpu.VMEM((1, rows), F32),
                        pltpu.VMEM((V_ROWS, rows), F32)],
        compiler_params=_params(("parallel", "parallel", "arbitrary"),
                                2 * (2 * tq * gw * 2 + 2 * t_seq * HEAD_DIM * 2)
                                + rows * HEAD_DIM * 6 + 16 * rows * 4 + 8 * rows * tk * 4),
        name="global_attention",
    )(qk, qk, v_t)


def _window_bias(n_heads):
    key = jnp.arange(3 * Q_BLOCK, dtype=jnp.int32)[:, None] - Q_BLOCK
    qpos = jnp.arange(Q_BLOCK, dtype=jnp.int32)[None, :]
    dist = jnp.abs(qpos - key)
    slopes = 2.0 ** (-8.0 * jnp.arange(1, n_heads + 1, dtype=F32) / n_heads)
    bias = -slopes[:, None, None] * dist.astype(F32)[None] * LOG2_E
    n_kv = n_heads // KV_GROUP
    bias = bias.reshape(n_kv, KV_GROUP, 3 * Q_BLOCK, Q_BLOCK).transpose(0, 2, 1, 3)
    bias = bias.reshape(n_kv, 3 * Q_BLOCK, KV_GROUP * Q_BLOCK)
    in_window = (dist <= WINDOW)
    in_window = jnp.tile(in_window, (1, KV_GROUP))[None]
    row = jnp.arange(3 * Q_BLOCK)[None, :, None]
    not_prev = row >= Q_BLOCK
    not_next = row < 2 * Q_BLOCK
    variants = [in_window, in_window & not_prev, in_window & not_next, in_window & not_prev & not_next]
    return jnp.stack([jnp.where(v, bias, MASK_VALUE) for v in variants])


def _window_kernel(q_ref, kp_ref, kc_ref, kn_ref, vp_ref, vc_ref, vn_ref, bias_ref, sink_ref, o_ref, *,
                   n_kv):
    gw = KV_GROUP * HEAD_DIM
    for kv in range(n_kv):
        ks = slice(kv * HEAD_DIM, (kv + 1) * HEAD_DIM)
        kcat = jnp.concatenate([kp_ref[:, ks], kc_ref[:, ks], kn_ref[:, ks]], axis=0)
        vtcat = jnp.concatenate([vp_ref[ks, :], vc_ref[ks, :], vn_ref[ks, :]], axis=1)
        qs = jnp.concatenate([q_ref[:, (kv * KV_GROUP + g) * HEAD_DIM:(kv * KV_GROUP + g + 1) * HEAD_DIM]
                              for g in range(KV_GROUP)], axis=0)
        s = lax.dot_general(kcat, qs, (((1,), (1,)), ((), ())), preferred_element_type=F32)
        s = s + bias_ref[0, kv]
        sink = sink_ref[:, kv * gw:(kv + 1) * gw] * LOG2_E
        m = jnp.maximum(jnp.max(s, axis=0, keepdims=True), sink)
        p = jnp.exp2(s - m)
        den = jnp.sum(p, axis=0, keepdims=True) + jnp.exp2(sink - m)
        pv = jnp.dot(vtcat, p.astype(BF16), preferred_element_type=F32)
        out = (pv / den).T
        for g in range(KV_GROUP):
            hd = kv * KV_GROUP + g
            o_ref[:, hd * HEAD_DIM:(hd + 1) * HEAD_DIM] = out[g * Q_BLOCK:(g + 1) * Q_BLOCK, :].astype(o_ref.dtype)


def _window_attention(rest, v_t, bias, sink_row, batch, t_seq, n_heads, k_col_block, vt_row_block):
    n = rest.shape[0]
    nb = t_seq // Q_BLOCK
    n_kv = n_heads // KV_GROUP
    qw = n_heads * HEAD_DIM
    kw = n_kv * HEAD_DIM

    def prev(b, i):
        return b * nb + jnp.maximum(i - 1, 0)

    def cur(b, i):
        return b * nb + i

    def nxt(b, i):
        return b * nb + jnp.minimum(i + 1, nb - 1)

    def variant(b, i):
        return (i == 0).astype(jnp.int32) + 2 * (i == nb - 1).astype(jnp.int32)

    k_specs = [pl.BlockSpec((Q_BLOCK, kw), functools.partial(lambda b, i, f: (f(b, i), k_col_block), f=f))
               for f in (prev, cur, nxt)]
    v_specs = [pl.BlockSpec((kw, Q_BLOCK), functools.partial(lambda b, i, f: (vt_row_block, f(b, i)), f=f))
               for f in (prev, cur, nxt)]
    return pl.pallas_call(
        functools.partial(_window_kernel, n_kv=n_kv),
        grid=(batch, nb),
        in_specs=[pl.BlockSpec((Q_BLOCK, qw), lambda b, i: (b * nb + i, 0))] + k_specs + v_specs
        + [pl.BlockSpec((1,) + bias.shape[1:], lambda b, i: (variant(b, i), 0, 0, 0)),
           pl.BlockSpec((1, qw), lambda b, i: (0, 0))],
        out_specs=pl.BlockSpec((Q_BLOCK, qw), lambda b, i: (b * nb + i, 0)),
        out_shape=jax.ShapeDtypeStruct((n, qw), BF16),
        compiler_params=_params(("parallel", "arbitrary"),
                                2 * (bias[0].size * 4 + 2 * Q_BLOCK * qw * 2 + 12 * Q_BLOCK * kw)
                                + 16 * 3 * Q_BLOCK * KV_GROUP * Q_BLOCK * 4),
        name="window_attention",
    )(rest, rest, rest, rest, v_t, v_t, v_t, bias, sink_row)


def _merge_kernel(ya_ref, yb_ref, wa_ref, wb_ref, ga_ref, gb_ref, o_ref):
    a = jnp.dot(ya_ref[...], wa_ref[...], preferred_element_type=F32)
    b = jnp.dot(yb_ref[...], wb_ref[...], preferred_element_type=F32)
    ga = jax.nn.sigmoid(ga_ref[...].astype(F32))
    gb = jax.nn.sigmoid(gb_ref[...].astype(F32))
    o_ref[...] = (ga * a + gb * b).astype(o_ref.dtype)


def _merge(ya, yb, w_a, w_b, gates):
    n, ka = ya.shape
    kb = yb.shape[1]
    d = w_a.shape[1]
    tm = _tile(1024, n)
    tn = _tile(512, d)
    ga_blk = 0
    gb_blk = d // tn
    return pl.pallas_call(
        _merge_kernel,
        grid=(n // tm, d // tn),
        in_specs=[pl.BlockSpec((tm, ka), lambda i, j: (i, 0)),
                  pl.BlockSpec((tm, kb), lambda i, j: (i, 0)),
                  pl.BlockSpec((ka, tn), lambda i, j: (0, j)),
                  pl.BlockSpec((kb, tn), lambda i, j: (0, j)),
                  pl.BlockSpec((tm, tn), lambda i, j: (i, ga_blk + j)),
                  pl.BlockSpec((tm, tn), lambda i, j: (i, gb_blk + j))],
        out_specs=pl.BlockSpec((tm, tn), lambda i, j: (i, j)),
        out_shape=jax.ShapeDtypeStruct((n, d), BF16),
        compiler_params=_params(("parallel", "arbitrary"),
                                2 * (tm * (ka + kb) * 2 + (ka + kb) * tn * 2 + 3 * tm * tn * 2)
                                + 6 * tm * tn * 4),
        name="branch_merge",
    )(ya, yb, w_a, w_b, gates, gates)


def _resid_kernel(m_ref, w_ref, x_ref, g_ref, o_ref, *, alpha):
    acc = jnp.dot(m_ref[...], w_ref[...], preferred_element_type=F32)
    o_ref[...] = alpha * x_ref[...] + g_ref[0] * acc


def _out_proj(merged, w_o, x2, mod3, t_seq, i_gate, alpha):
    n, d = x2.shape
    k = merged.shape[1]
    tm = _tile(1024, t_seq)
    tn = _tile(512, d)
    per = t_seq // tm
    return pl.pallas_call(
        functools.partial(_resid_kernel, alpha=alpha),
        grid=(n // tm, d // tn),
        in_specs=[pl.BlockSpec((tm, k), lambda i, j: (i, 0)),
                  pl.BlockSpec((k, tn), lambda i, j: (0, j)),
                  pl.BlockSpec((tm, tn), lambda i, j: (i, j)),
                  pl.BlockSpec((1, 1, tn), lambda i, j: ((i // per) * N_MOD + i_gate, 0, j))],
        out_specs=pl.BlockSpec((tm, tn), lambda i, j: (i, j)),
        out_shape=jax.ShapeDtypeStruct((n, d), F32),
        compiler_params=_params(("parallel", "arbitrary"),
                                2 * (tm * k * 2 + k * tn * 2 + 2 * tm * tn * 4) + 2 * tm * tn * 4),
        name="out_proj_residual",
    )(merged, w_o, x2, mod3)


def _ln_pair_kernel(z_ref, g_ref, b_ref, sc_ref, sh_ref, x1_ref, h_ref):
    x1 = _ln(z_ref[...]) * g_ref[...] + b_ref[...]
    x1_ref[...] = x1
    h_ref[...] = (_ln(x1) * (1.0 + sc_ref[0]) + sh_ref[0]).astype(h_ref.dtype)


def _ln_pair(z, ln_g, ln_b, mod3, t_seq, i_scale, i_shift):
    n, d = z.shape
    tm = _tile(256, t_seq)
    per = t_seq // tm
    return pl.pallas_call(
        _ln_pair_kernel,
        grid=(n // tm,),
        in_specs=[pl.BlockSpec((tm, d), lambda i: (i, 0)),
                  pl.BlockSpec((1, d), lambda i: (0, 0)),
                  pl.BlockSpec((1, d), lambda i: (0, 0)),
                  pl.BlockSpec((1, 1, d), lambda i: ((i // per) * N_MOD + i_scale, 0, 0)),
                  pl.BlockSpec((1, 1, d), lambda i: ((i // per) * N_MOD + i_shift, 0, 0))],
        out_specs=[pl.BlockSpec((tm, d), lambda i: (i, 0)),
                   pl.BlockSpec((tm, d), lambda i: (i, 0))],
        out_shape=[jax.ShapeDtypeStruct((n, d), F32), jax.ShapeDtypeStruct((n, d), BF16)],
        compiler_params=_params(("parallel",), 2 * tm * d * 10 + 6 * tm * d * 4),
        name="ln1_ln2mod",
    )(z, ln_g.reshape(1, d), ln_b.reshape(1, d), mod3, mod3)


def _ffn_up_kernel(h_ref, wg_ref, wu_ref, o_ref):
    h = h_ref[...]
    g = jnp.dot(h, wg_ref[...], preferred_element_type=F32)
    u = jnp.dot(h, wu_ref[...], preferred_element_type=F32)
    o_ref[...] = (g * jax.nn.sigmoid(g) * u).astype(o_ref.dtype)


def _ffn_up(h, w_g, w_u):
    n, d = h.shape
    ff = w_g.shape[1]
    tm = _tile(1024, n)
    tn = min(512, ff)
    return pl.pallas_call(
        _ffn_up_kernel,
        grid=(n // tm, pl.cdiv(ff, tn)),
        in_specs=[pl.BlockSpec((tm, d), lambda i, j: (i, 0)),
                  pl.BlockSpec((d, tn), lambda i, j: (0, j)),
                  pl.BlockSpec((d, tn), lambda i, j: (0, j))],
        out_specs=pl.BlockSpec((tm, tn), lambda i, j: (i, j)),
        out_shape=jax.ShapeDtypeStruct((n, ff), BF16),
        compiler_params=_params(("parallel", "arbitrary"),
                                2 * (tm * d * 2 + d * 2 * tn * 2 + tm * tn * 2) + 3 * tm * 2 * tn * 4),
        name="ffn_up",
    )(h, w_g, w_u)


def _ffn_down_kernel(a_ref, w_ref, x_ref, gate_ref, g_ref, b_ref, o_ref, *, alpha, nk, last, rows):
    k = pl.program_id(1)
    tk = a_ref.shape[1]

    @pl.when(k == 0)
    def _():
        o_ref[...] = jnp.zeros(o_ref.shape, F32)

    if last == tk:
        o_ref[...] += jnp.dot(a_ref[...], w_ref[...], preferred_element_type=F32)
    else:
        @pl.when(k < nk - 1)
        def _():
            o_ref[...] += jnp.dot(a_ref[...], w_ref[...], preferred_element_type=F32)

        @pl.when(k == nk - 1)
        def _():
            o_ref[...] += jnp.dot(a_ref[:, :last], w_ref[:last, :], preferred_element_type=F32)

    @pl.when(k == nk - 1)
    def _():
        def chunk(r, carry):
            sl = pl.ds(pl.multiple_of(r * rows, rows), rows)
            z = alpha * x_ref[sl, :] + gate_ref[0] * o_ref[sl, :]
            o_ref[sl, :] = _ln(z) * g_ref[...] + b_ref[...]
            return carry

        lax.fori_loop(0, o_ref.shape[0] // rows, chunk, 0)


def _ffn_down(a, w_d, x1, ln_g, ln_b, mod3, t_seq, i_gate, alpha):
    n, ff = a.shape
    d = w_d.shape[1]
    tm = _tile(512, t_seq)
    tk = min(1024, ff)
    nk = pl.cdiv(ff, tk)
    last = ff - (nk - 1) * tk
    assert last % LANE == 0, "the partial contraction block must stay lane-aligned"
    per = t_seq // tm
    return pl.pallas_call(
        functools.partial(_ffn_down_kernel, alpha=alpha, nk=nk, last=last, rows=_tile(LANE, tm)),
        grid=(n // tm, nk),
        in_specs=[pl.BlockSpec((tm, tk), lambda i, k: (i, k)),
                  pl.BlockSpec((tk, d), lambda i, k: (k, 0)),
                  pl.BlockSpec((tm, d), lambda i, k: (i, 0), pipeline_mode=pl.Buffered(1)),
                  pl.BlockSpec((1, 1, d), lambda i, k: ((i // per) * N_MOD + i_gate, 0, 0)),
                  pl.BlockSpec((1, d), lambda i, k: (0, 0)),
                  pl.BlockSpec((1, d), lambda i, k: (0, 0))],
        out_specs=pl.BlockSpec((tm, d), lambda i, k: (i, 0)),
        out_shape=jax.ShapeDtypeStruct((n, d), F32),
        compiler_params=_params(("parallel", "arbitrary"),
                                2 * (tm * tk * 2 + tk * d * 2 + tm * d * 4) + tm * d * 4
                                + 2 * tm * d * 4),
        name="ffn_down_ln",
    )(a, w_d, x1, mod3, ln_g.reshape(1, d), ln_b.reshape(1, d))


def _rope_tables(t_max):
    rows = t_max // GRID_W
    row = jnp.repeat(jnp.arange(rows, dtype=F32), GRID_W)
    col = jnp.tile(jnp.arange(GRID_W, dtype=F32), rows)
    inv = 1.0 / (ROPE_THETA ** (jnp.arange(0, HALF_ROT, 2, dtype=F32) / HALF_ROT))
    ang_r = row[:, None] * inv[None, :]
    ang_c = col[:, None] * inv[None, :]
    cr, sr, cc, sc = jnp.cos(ang_r), jnp.sin(ang_r), jnp.cos(ang_c), jnp.sin(ang_c)
    cos_t = jnp.concatenate([cr, cc, cr, cc], axis=-1)
    sin_t = jnp.concatenate([-sr, -sc, sr, sc], axis=-1)
    return cos_t, sin_t


def _prep_layer(w_in, q_norm, k_norm, w_br_a, w_br_b, w_o, w_gate, w_up, w_down):
    d = w_in.shape[0]
    n_heads = d // HEAD_DIM
    ha = n_heads // 2
    qa_w = ha * HEAD_DIM
    kv_w = (ha // KV_GROUP) * HEAD_DIM
    o_ka = qa_w
    o_va = o_ka + kv_w
    o_qb = o_va + kv_w
    o_kb = o_qb + qa_w
    o_vb = o_kb + kv_w
    o_ga = o_vb + kv_w
    w_qk = w_in[:, :o_va].reshape(d, -1, HEAD_DIM)[:, :, _ROT_PERM].reshape(d, o_va).astype(BF16)
    w_v_t = jnp.concatenate([w_in[:, o_va:o_qb], w_in[:, o_vb:o_ga]], axis=1).T.astype(BF16)
    w_qkb = _cast_cols(w_in, o_qb, o_vb - o_qb)
    w_gates = _cast_cols(w_in, o_ga, w_in.shape[1] - o_ga)
    gain = jnp.concatenate([jnp.tile(q_norm[_ROT_PERM], ha),
                            jnp.tile(k_norm[_ROT_PERM], ha // KV_GROUP)]).reshape(1, -1)
    q_scale = jnp.full((qa_w,), LOG2_E / math.sqrt(HEAD_DIM), F32)
    scale = jnp.concatenate([q_scale, jnp.ones((kv_w,), F32)]).reshape(1, -1)
    return dict(w_qk=w_qk, w_v_t=w_v_t, w_qkb=w_qkb, w_gates=w_gates, gain=gain, scale=scale,
                w_a=_cast_cols(w_br_a), w_b=_cast_cols(w_br_b), w_o=_cast_cols(w_o),
                w_g=_cast_cols(w_gate), w_u=_cast_cols(w_up), w_d=_cast_cols(w_down),
                ha=ha, kv_w=kv_w, qa_w=qa_w)


def _encoder_layer(x, mod, p, sink_row, win_bias, ln1_g, ln1_b, ln2_g, ln2_b, cos_t, sin_t, alpha):
    b, t, d = x.shape
    x2 = x.reshape(b * t, d)
    mod3 = mod.reshape(b * N_MOD, 1, d)
    ha, kv_w, qa_w = p["ha"], p["kv_w"], p["qa_w"]

    h = _ln_mod(x2, mod3, t, 1, 0)
    qk = _qk_proj(h, p["w_qk"], p["gain"], p["scale"], cos_t, sin_t, t)
    va_t, vb_t = _proj_t(h, p["w_v_t"], kv_w // HEAD_DIM)
    qkb = _proj(h, p["w_qkb"], p["scale"])
    gates = _proj(h, p["w_gates"], None)
    ya = _global_attention(qk, va_t, b, t, ha)
    yb = _window_attention(qkb, vb_t, win_bias, sink_row, b, t, ha, k_col_block=qa_w // kv_w,
                           vt_row_block=0)
    merged = _merge(ya, yb, p["w_a"], p["w_b"], gates)
    z = _out_proj(merged, p["w_o"], x2, mod3, t, 2, alpha)
    x1, h2 = _ln_pair(z, ln1_g, ln1_b, mod3, t, 4, 3)
    a = _ffn_up(h2, p["w_g"], p["w_u"])
    y = _ffn_down(a, p["w_d"], x1, ln2_g, ln2_b, mod3, t, 5, alpha)
    return y.reshape(b, t, d)


def kernel(x_prompt, x_sample, c_prompt, c_sample, w_ada, b_ada, w_in, q_norm_a, k_norm_a, sink_b,
           w_br_a, w_br_b, w_o, ln1_g, ln1_b, w_ffn_gate, w_ffn_up, w_ffn_down, ln2_g, ln2_b):
    depth = w_ada.shape[0]
    alpha = float((2.0 * depth) ** 0.25)
    d = x_prompt.shape[-1]
    bp, bs = c_prompt.shape[0], c_sample.shape[0]
    rows = -(-(bp + bs) // 8) * 8
    cos_t, sin_t = _rope_tables(max(x_prompt.shape[1], x_sample.shape[1]))
    y_p, y_s = x_prompt, x_sample
    c_all = jnp.concatenate([c_prompt, c_sample, jnp.zeros((rows - bp - bs, d), F32)], axis=0)
    win_bias = _window_bias(sink_b.shape[1])
    for l in range(depth):
        mod = _ada(c_all, w_ada[l], b_ada[l]).reshape(rows, N_MOD, d)
        p = _prep_layer(w_in[l], q_norm_a[l], k_norm_a[l], w_br_a[l], w_br_b[l], w_o[l],
                        w_ffn_gate[l], w_ffn_up[l], w_ffn_down[l])
        sink_row = jnp.repeat(sink_b[l], HEAD_DIM).reshape(1, -1)
        args = (p, sink_row, win_bias, ln1_g[l], ln1_b[l], ln2_g[l], ln2_b[l], cos_t, sin_t, alpha)
        y_p = _encoder_layer(y_p, mod[:bp], *args)
        y_s = _encoder_layer(y_s, mod[bp:bp + bs], *args)
    return (y_p, y_s)
```

```python
import functools
import math

import jax
import jax.numpy as jnp
from jax import lax
from jax.experimental import pallas as pl
from jax.experimental.pallas import tpu as pltpu

F32 = jnp.float32
BF16 = jnp.bfloat16

HEAD_DIM = 128
GRID_W = 64
Q_BLOCK = 128
WINDOW = 128
HALF_ROT = HEAD_DIM // 2
ROPE_THETA = 10000.0
N_MOD = 6
LN_EPS = 1e-5
RMS_EPS = 1e-6
KV_GROUP = 4
MASK_VALUE = -1e30
LOG2_E = 1.4426950408889634

ONES_ROWS = 16
V_ROWS = HEAD_DIM + ONES_ROWS
LANE = 128
VMEM_CAP_BYTES = 60 * 1024 * 1024
CAST_BLOCK_BYTES = 4 * 1024 * 1024


def _tile(pref, *dims):
    t = (min(pref, *dims) // LANE) * LANE
    while t >= LANE:
        if all(d % t == 0 for d in dims):
            return t
        t -= LANE
    return min(dims)


def _params(semantics, vmem_bytes):
    limit = int(min(max(vmem_bytes, 16 * 1024 * 1024), VMEM_CAP_BYTES))
    return pltpu.CompilerParams(dimension_semantics=semantics, vmem_limit_bytes=limit)


def _ln(x):
    mu = jnp.mean(x, axis=-1, keepdims=True)
    xc = x - mu
    return xc * lax.rsqrt(jnp.mean(xc * xc, axis=-1, keepdims=True) + LN_EPS)


def _ada_kernel(c_ref, w_ref, b_ref, o_ref):
    c = c_ref[...]
    a = (c * jax.nn.sigmoid(c)).astype(BF16)
    o_ref[...] = jnp.dot(a, w_ref[...].astype(BF16), preferred_element_type=F32) + b_ref[...]


def _ada(c_pad, w_ada, b_ada):
    rows, d = c_pad.shape
    n = w_ada.shape[1]
    tn = _tile(512, n)
    return pl.pallas_call(
        _ada_kernel,
        grid=(n // tn,),
        in_specs=[pl.BlockSpec((rows, d), lambda j: (0, 0)),
                  pl.BlockSpec((d, tn), lambda j: (0, j)),
                  pl.BlockSpec((1, tn), lambda j: (0, j))],
        out_specs=pl.BlockSpec((rows, tn), lambda j: (0, j)),
        out_shape=jax.ShapeDtypeStruct((rows, n), F32),
        compiler_params=_params(("parallel",), 2 * d * tn * 4 + d * tn * 2 + (4 << 20)),
        name="ada_mod",
    )(c_pad, w_ada, b_ada.reshape(1, n))


def _ln_mod_kernel(x_ref, sc_ref, sh_ref, o_ref):
    y = _ln(x_ref[...])
    o_ref[...] = (y * (1.0 + sc_ref[0]) + sh_ref[0]).astype(o_ref.dtype)


def _ln_mod(x2, mod3, t_seq, i_scale, i_shift):
    n, d = x2.shape
    tm = _tile(256, t_seq)
    per = t_seq // tm
    return pl.pallas_call(
        _ln_mod_kernel,
        grid=(n // tm,),
        in_specs=[pl.BlockSpec((tm, d), lambda i: (i, 0)),
                  pl.BlockSpec((1, 1, d), lambda i: ((i // per) * N_MOD + i_scale, 0, 0)),
                  pl.BlockSpec((1, 1, d), lambda i: ((i // per) * N_MOD + i_shift, 0, 0))],
        out_specs=pl.BlockSpec((tm, d), lambda i: (i, 0)),
        out_shape=jax.ShapeDtypeStruct((n, d), BF16),
        compiler_params=_params(("parallel",), 2 * tm * d * 6 + 4 * tm * d * 4),
        name="ln_mod",
    )(x2, mod3, mod3)


def _qk_kernel(h_ref, w_ref, g_ref, s_ref, cos_ref, sin_ref, o_ref, *, chunk):
    h = h_ref[...]
    cos = cos_ref[...]
    sin = sin_ref[...]
    for c in range(o_ref.shape[1] // chunk):
        acc = jnp.dot(h, w_ref[:, c * chunk:(c + 1) * chunk], preferred_element_type=F32)
        for hh in range(chunk // HEAD_DIM):
            sl = slice(c * chunk + hh * HEAD_DIM, c * chunk + (hh + 1) * HEAD_DIM)
            x = acc[:, hh * HEAD_DIM:(hh + 1) * HEAD_DIM]
            y = x * lax.rsqrt(jnp.mean(x * x, axis=-1, keepdims=True) + RMS_EPS) * g_ref[:, sl]
            partner = pltpu.roll(y, HALF_ROT, 1)
            o_ref[:, sl] = ((y * cos + partner * sin) * s_ref[:, sl]).astype(o_ref.dtype)


def _qk_proj(h, w_qk, gain, scale, cos_t, sin_t, t_seq):
    n, d = h.shape
    nw = w_qk.shape[1]
    tm = _tile(512, t_seq)
    per = t_seq // tm
    return pl.pallas_call(
        functools.partial(_qk_kernel, chunk=_tile(256, nw)),
        grid=(n // tm,),
        in_specs=[pl.BlockSpec((tm, d), lambda i: (i, 0)),
                  pl.BlockSpec((d, nw), lambda i: (0, 0), pipeline_mode=pl.Buffered(1)),
                  pl.BlockSpec((1, nw), lambda i: (0, 0)),
                  pl.BlockSpec((1, nw), lambda i: (0, 0)),
                  pl.BlockSpec((tm, HEAD_DIM), lambda i: (i % per, 0)),
                  pl.BlockSpec((tm, HEAD_DIM), lambda i: (i % per, 0))],
        out_specs=pl.BlockSpec((tm, nw), lambda i: (i, 0)),
        out_shape=jax.ShapeDtypeStruct((n, nw), BF16),
        compiler_params=_params(("parallel",),
                                d * nw * 2 + 2 * (tm * d * 2 + tm * nw * 2 + 2 * tm * HEAD_DIM * 4)
                                + 16 * tm * 256 * 4),
        name="qk_proj_rope",
    )(h, w_qk, gain, scale, cos_t, sin_t)


def _cast_kernel(w_ref, o_ref):
    o_ref[...] = w_ref[...].astype(o_ref.dtype)


def _cast_cols(w, col0=0, ncols=None):
    rows, width = w.shape
    ncols = width if ncols is None else ncols
    if col0 == 0 and ncols == width:
        tc = width
    else:
        tc = _tile(2048, ncols, *((col0,) if col0 else ()))
    tr = max(8, min(rows, (CAST_BLOCK_BYTES // (4 * tc)) // 8 * 8))
    while rows % tr:
        tr -= 8
    c0 = col0 // tc
    return pl.pallas_call(
        _cast_kernel,
        grid=(rows // tr, ncols // tc),
        in_specs=[pl.BlockSpec((tr, tc), lambda i, j: (i, c0 + j))],
        out_specs=pl.BlockSpec((tr, tc), lambda i, j: (i, j)),
        out_shape=jax.ShapeDtypeStruct((rows, ncols), BF16),
        compiler_params=_params(("parallel", "parallel"), 2 * tr * tc * 6 + tr * tc * 4),
        name="cast_bf16",
    )(w)


def _mm_scaled_kernel(x_ref, w_ref, s_ref, o_ref):
    acc = jnp.dot(x_ref[...], w_ref[...], preferred_element_type=F32)
    o_ref[...] = (acc * s_ref[...]).astype(o_ref.dtype)


def _mm_kernel(x_ref, w_ref, o_ref):
    o_ref[...] = jnp.dot(x_ref[...], w_ref[...], preferred_element_type=F32).astype(o_ref.dtype)


def _proj(x, w, col_scale, tn_pref=512):
    n, d = x.shape
    nw = w.shape[1]
    tm = _tile(1024, n)
    tn = _tile(tn_pref, nw)
    scaled = col_scale is not None
    in_specs = [pl.BlockSpec((tm, d), lambda i, j: (i, 0)),
                pl.BlockSpec((d, tn), lambda i, j: (0, j))]
    if scaled:
        in_specs.append(pl.BlockSpec((1, tn), lambda i, j: (0, j)))
    return pl.pallas_call(
        _mm_scaled_kernel if scaled else _mm_kernel,
        grid=(n // tm, nw // tn),
        in_specs=in_specs,
        out_specs=pl.BlockSpec((tm, tn), lambda i, j: (i, j)),
        out_shape=jax.ShapeDtypeStruct((n, nw), BF16),
        compiler_params=_params(("parallel", "arbitrary"),
                                2 * (tm * d * 2 + d * tn * 2 + tm * tn * 2) + 2 * tm * tn * 4),
        name="proj",
    )(*((x, w, col_scale) if scaled else (x, w)))


def _mm_t_kernel(wt_ref, x_ref, oa_ref, ob_ref):
    res = lax.dot_general(wt_ref[...], x_ref[...], (((1,), (1,)), ((), ())), preferred_element_type=F32)
    nb = ob_ref.shape[0]
    tm = x_ref.shape[0]
    for j in range(oa_ref.shape[0] // V_ROWS):
        oa_ref[j * V_ROWS:j * V_ROWS + HEAD_DIM, :] = res[j * HEAD_DIM:(j + 1) * HEAD_DIM, :].astype(oa_ref.dtype)
        oa_ref[j * V_ROWS + HEAD_DIM:(j + 1) * V_ROWS, :] = jnp.ones((ONES_ROWS, tm), oa_ref.dtype)
    ob_ref[...] = res[res.shape[0] - nb:, :].astype(ob_ref.dtype)


def _proj_t(x, w_t, n_a_heads):
    n, d = x.shape
    nw = w_t.shape[0]
    nb = nw - n_a_heads * HEAD_DIM
    tm = _tile(1024, n)
    return pl.pallas_call(
        _mm_t_kernel,
        grid=(n // tm,),
        in_specs=[pl.BlockSpec((nw, d), lambda i: (0, 0)),
                  pl.BlockSpec((tm, d), lambda i: (i, 0))],
        out_specs=[pl.BlockSpec((n_a_heads * V_ROWS, tm), lambda i: (0, i)),
                   pl.BlockSpec((nb, tm), lambda i: (0, i))],
        out_shape=[jax.ShapeDtypeStruct((n_a_heads * V_ROWS, n), BF16),
                   jax.ShapeDtypeStruct((nb, n), BF16)],
        compiler_params=_params(("parallel",),
                                2 * (tm * d * 2 + nw * d * 2 + 2 * nw * tm * 2) + 2 * nw * tm * 4),
        name="proj_t",
    )(w_t, x)


def _flash_kernel(q_ref, k_ref, vt_ref, o_ref, qs_ref, sa_ref, sb_ref, m_ref, acc_ref, *, tq, tk, nk):
    for g in range(KV_GROUP):
        qs_ref[g * tq:(g + 1) * tq, :] = q_ref[:, g * HEAD_DIM:(g + 1) * HEAD_DIM]
    m_ref[...] = jnp.full(m_ref.shape, -jnp.inf, F32)
    acc_ref[...] = jnp.zeros(acc_ref.shape, F32)

    def scores(idx, s_ref):
        start = pl.multiple_of(idx * tk, tk)
        s_ref[...] = lax.dot_general(k_ref[pl.ds(start, tk), :], qs_ref[...], (((1,), (1,)), ((), ())),
                                     preferred_element_type=F32)

    def update(idx, s_ref):
        start = pl.multiple_of(idx * tk, tk)
        s = s_ref[...]
        m_prev = m_ref[...]
        m_new = jnp.maximum(m_prev, jnp.max(s, axis=0, keepdims=True))
        alpha = jnp.exp2(m_prev - m_new)
        p = jnp.exp2(s - m_new).astype(BF16)
        acc_ref[...] = alpha * acc_ref[...] + jnp.dot(vt_ref[:, pl.ds(start, tk)], p,
                                                      preferred_element_type=F32)
        m_ref[...] = m_new

    scores(0, sa_ref)
    pairs = (nk - 1) // 2

    def body(j, carry):
        scores(2 * j + 1, sb_ref)
        update(2 * j, sa_ref)
        scores(2 * j + 2, sa_ref)
        update(2 * j + 1, sb_ref)
        return carry

    lax.fori_loop(0, pairs, body, 0)
    if nk - 2 * pairs == 2:
        scores(nk - 1, sb_ref)
        update(nk - 2, sa_ref)
        update(nk - 1, sb_ref)
    else:
        update(nk - 1, sa_ref)
    out = (acc_ref[:HEAD_DIM, :] / acc_ref[HEAD_DIM:HEAD_DIM + 1, :]).T
    for g in range(KV_GROUP):
        o_ref[:, g * HEAD_DIM:(g + 1) * HEAD_DIM] = out[g * tq:(g + 1) * tq, :].astype(o_ref.dtype)


def _global_attention(qk, v_t, batch, t_seq, n_q_heads):
    n = qk.shape[0]
    n_kv = n_q_heads // KV_GROUP
    tq = _tile(256, t_seq)
    tk = _tile(512, t_seq)
    nq = t_seq // tq
    nk = t_seq // tk
    gw = KV_GROUP * HEAD_DIM
    kern = functools.partial(_flash_kernel, tq=tq, tk=tk, nk=nk)
    rows = KV_GROUP * tq
    return pl.pallas_call(
        kern,
        grid=(batch, n_kv, nq),
        in_specs=[pl.BlockSpec((tq, gw), lambda b, h, i: (b * nq + i, h)),
                  pl.BlockSpec((t_seq, HEAD_DIM), lambda b, h, i: (b, n_q_heads + h)),
                  pl.BlockSpec((V_ROWS, t_seq), lambda b, h, i: (h, b))],
        out_specs=pl.BlockSpec((tq, gw), lambda b, h, i: (b * nq + i, h)),
        out_shape=jax.ShapeDtypeStruct((n, n_q_heads * HEAD_DIM), BF16),
        scratch_shapes=[pltpu.VMEM((rows, HEAD_DIM), BF16),
                        pltpu.VMEM((tk, rows), F32),
                        pltpu.VMEM((tk, rows), F32),
                        pltpu.VMEM((1, rows), F32),
                        pltpu.VMEM((V_ROWS, rows), F32)],
        compiler_params=_params(("parallel", "parallel", "arbitrary"),
                                2 * (2 * tq * gw * 2 + 2 * t_seq * HEAD_DIM * 2)
                                + rows * HEAD_DIM * 6 + 16 * rows * 4 + 8 * rows * tk * 4),
        name="global_attention",
    )(qk, qk, v_t)


def _window_bias(n_heads):
    key = jnp.arange(3 * Q_BLOCK, dtype=jnp.int32)[:, None] - Q_BLOCK
    qpos = jnp.arange(Q_BLOCK, dtype=jnp.int32)[None, :]
    dist = jnp.abs(qpos - key)
    slopes = 2.0 ** (-8.0 * jnp.arange(1, n_heads + 1, dtype=F32) / n_heads)
    bias = -slopes[:, None, None] * dist.astype(F32)[None] * LOG2_E
    n_kv = n_heads // KV_GROUP
    bias = bias.reshape(n_kv, KV_GROUP, 3 * Q_BLOCK, Q_BLOCK).transpose(0, 2, 1, 3)
    bias = bias.reshape(n_kv, 3 * Q_BLOCK, KV_GROUP * Q_BLOCK)
    in_window = (dist <= WINDOW)
    in_window = jnp.tile(in_window, (1, KV_GROUP))[None]
    row = jnp.arange(3 * Q_BLOCK)[None, :, None]
    not_prev = row >= Q_BLOCK
    not_next = row < 2 * Q_BLOCK
    variants = [in_window, in_window & not_prev, in_window & not_next, in_window & not_prev & not_next]
    return jnp.stack([jnp.where(v, bias, MASK_VALUE) for v in variants])


def _window_kernel(q_ref, kp_ref, kc_ref, kn_ref, vp_ref, vc_ref, vn_ref, bias_ref, sink_ref, o_ref, *,
                   n_kv):
    gw = KV_GROUP * HEAD_DIM
    for kv in range(n_kv):
        ks = slice(kv * HEAD_DIM, (kv + 1) * HEAD_DIM)
        kcat = jnp.concatenate([kp_ref[:, ks], kc_ref[:, ks], kn_ref[:, ks]], axis=0)
        vtcat = jnp.concatenate([vp_ref[ks, :], vc_ref[ks, :], vn_ref[ks, :]], axis=1)
        qs = jnp.concatenate([q_ref[:, (kv * KV_GROUP + g) * HEAD_DIM:(kv * KV_GROUP + g + 1) * HEAD_DIM]
                              for g in range(KV_GROUP)], axis=0)
        s = lax.dot_general(kcat, qs, (((1,), (1,)), ((), ())), preferred_element_type=F32)
        s = s + bias_ref[0, kv]
        sink = sink_ref[:, kv * gw:(kv + 1) * gw] * LOG2_E
        m = jnp.maximum(jnp.max(s, axis=0, keepdims=True), sink)
        p = jnp.exp2(s - m)
        den = jnp.sum(p, axis=0, keepdims=True) + jnp.exp2(sink - m)
        pv = jnp.dot(vtcat, p.astype(BF16), preferred_element_type=F32)
        out = (pv / den).T
        for g in range(KV_GROUP):
            hd = kv * KV_GROUP + g
            o_ref[:, hd * HEAD_DIM:(hd + 1) * HEAD_DIM] = out[g * Q_BLOCK:(g + 1) * Q_BLOCK, :].astype(o_ref.dtype)


def _window_attention(rest, v_t, bias, sink_row, batch, t_seq, n_heads, k_col_block, vt_row_block):
    n = rest.shape[0]
    nb = t_seq // Q_BLOCK
    n_kv = n_heads // KV_GROUP
    qw = n_heads * HEAD_DIM
    kw = n_kv * HEAD_DIM

    def prev(b, i):
        return b * nb + jnp.maximum(i - 1, 0)

    def cur(b, i):
        return b * nb + i

    def nxt(b, i):
        return b * nb + jnp.minimum(i + 1, nb - 1)

    def variant(b, i):
        return (i == 0).astype(jnp.int32) + 2 * (i == nb - 1).astype(jnp.int32)

    k_specs = [pl.BlockSpec((Q_BLOCK, kw), functools.partial(lambda b, i, f: (f(b, i), k_col_block), f=f))
               for f in (prev, cur, nxt)]
    v_specs = [pl.BlockSpec((kw, Q_BLOCK), functools.partial(lambda b, i, f: (vt_row_block, f(b, i)), f=f))
               for f in (prev, cur, nxt)]
    return pl.pallas_call(
        functools.partial(_window_kernel, n_kv=n_kv),
        grid=(batch, nb),
        in_specs=[pl.BlockSpec((Q_BLOCK, qw), lambda b, i: (b * nb + i, 0))] + k_specs + v_specs
        + [pl.BlockSpec((1,) + bias.shape[1:], lambda b, i: (variant(b, i), 0, 0, 0)),
           pl.BlockSpec((1, qw), lambda b, i: (0, 0))],
        out_specs=pl.BlockSpec((Q_BLOCK, qw), lambda b, i: (b * nb + i, 0)),
        out_shape=jax.ShapeDtypeStruct((n, qw), BF16),
        compiler_params=_params(("parallel", "arbitrary"),
                                2 * (bias[0].size * 4 + 2 * Q_BLOCK * qw * 2 + 12 * Q_BLOCK * kw)
                                + 16 * 3 * Q_BLOCK * KV_GROUP * Q_BLOCK * 4),
        name="window_attention",
    )(rest, rest, rest, rest, v_t, v_t, v_t, bias, sink_row)


def _merge_kernel(ya_ref, yb_ref, wa_ref, wb_ref, ga_ref, gb_ref, o_ref):
    a = jnp.dot(ya_ref[...], wa_ref[...], preferred_element_type=F32)
    b = jnp.dot(yb_ref[...], wb_ref[...], preferred_element_type=F32)
    ga = jax.nn.sigmoid(ga_ref[...].astype(F32))
    gb = jax.nn.sigmoid(gb_ref[...].astype(F32))
    o_ref[...] = (ga * a + gb * b).astype(o_ref.dtype)


def _merge(ya, yb, w_a, w_b, gates):
    n, ka = ya.shape
    kb = yb.shape[1]
    d = w_a.shape[1]
    tm = _tile(1024, n)
    tn = _tile(512, d)
    ga_blk = 0
    gb_blk = d // tn
    return pl.pallas_call(
        _merge_kernel,
        grid=(n // tm, d // tn),
        in_specs=[pl.BlockSpec((tm, ka), lambda i, j: (i, 0)),
                  pl.BlockSpec((tm, kb), lambda i, j: (i, 0)),
                  pl.BlockSpec((ka, tn), lambda i, j: (0, j)),
                  pl.BlockSpec((kb, tn), lambda i, j: (0, j)),
                  pl.BlockSpec((tm, tn), lambda i, j: (i, ga_blk + j)),
                  pl.BlockSpec((tm, tn), lambda i, j: (i, gb_blk + j))],
        out_specs=pl.BlockSpec((tm, tn), lambda i, j: (i, j)),
        out_shape=jax.ShapeDtypeStruct((n, d), BF16),
        compiler_params=_params(("parallel", "arbitrary"),
                                2 * (tm * (ka + kb) * 2 + (ka + kb) * tn * 2 + 3 * tm * tn * 2)
                                + 6 * tm * tn * 4),
        name="branch_merge",
    )(ya, yb, w_a, w_b, gates, gates)


def _resid_kernel(m_ref, w_ref, x_ref, g_ref, o_ref, *, alpha):
    acc = jnp.dot(m_ref[...], w_ref[...], preferred_element_type=F32)
    o_ref[...] = alpha * x_ref[...] + g_ref[0] * acc


def _out_proj(merged, w_o, x2, mod3, t_seq, i_gate, alpha):
    n, d = x2.shape
    k = merged.shape[1]
    tm = _tile(1024, t_seq)
    tn = _tile(512, d)
    per = t_seq // tm
    return pl.pallas_call(
        functools.partial(_resid_kernel, alpha=alpha),
        grid=(n // tm, d // tn),
        in_specs=[pl.BlockSpec((tm, k), lambda i, j: (i, 0)),
                  pl.BlockSpec((k, tn), lambda i, j: (0, j)),
                  pl.BlockSpec((tm, tn), lambda i, j: (i, j)),
                  pl.BlockSpec((1, 1, tn), lambda i, j: ((i // per) * N_MOD + i_gate, 0, j))],
        out_specs=pl.BlockSpec((tm, tn), lambda i, j: (i, j)),
        out_shape=jax.ShapeDtypeStruct((n, d), F32),
        compiler_params=_params(("parallel", "arbitrary"),
                                2 * (tm * k * 2 + k * tn * 2 + 2 * tm * tn * 4) + 2 * tm * tn * 4),
        name="out_proj_residual",
    )(merged, w_o, x2, mod3)


def _ln_pair_kernel(z_ref, g_ref, b_ref, sc_ref, sh_ref, x1_ref, h_ref):
    x1 = _ln(z_ref[...]) * g_ref[...] + b_ref[...]
    x1_ref[...] = x1
    h_ref[...] = (_ln(x1) * (1.0 + sc_ref[0]) + sh_ref[0]).astype(h_ref.dtype)


def _ln_pair(z, ln_g, ln_b, mod3, t_seq, i_scale, i_shift):
    n, d = z.shape
    tm = _tile(256, t_seq)
    per = t_seq // tm
    return pl.pallas_call(
        _ln_pair_kernel,
        grid=(n // tm,),
        in_specs=[pl.BlockSpec((tm, d), lambda i: (i, 0)),
                  pl.BlockSpec((1, d), lambda i: (0, 0)),
                  pl.BlockSpec((1, d), lambda i: (0, 0)),
                  pl.BlockSpec((1, 1, d), lambda i: ((i // per) * N_MOD + i_scale, 0, 0)),
                  pl.BlockSpec((1, 1, d), lambda i: ((i // per) * N_MOD + i_shift, 0, 0))],
        out_specs=[pl.BlockSpec((tm, d), lambda i: (i, 0)),
                   pl.BlockSpec((tm, d), lambda i: (i, 0))],
        out_shape=[jax.ShapeDtypeStruct((n, d), F32), jax.ShapeDtypeStruct((n, d), BF16)],
        compiler_params=_params(("parallel",), 2 * tm * d * 10 + 6 * tm * d * 4),
        name="ln1_ln2mod",
    )(z, ln_g.reshape(1, d), ln_b.reshape(1, d), mod3, mod3)


def _ffn_up_kernel(h_ref, wg_ref, wu_ref, o_ref):
    h = h_ref[...]
    g = jnp.dot(h, wg_ref[...], preferred_element_type=F32)
    u = jnp.dot(h, wu_ref[...], preferred_element_type=F32)
    o_ref[...] = (g * jax.nn.sigmoid(g) * u).astype(o_ref.dtype)


def _ffn_up(h, w_g, w_u):
    n, d = h.shape
    ff = w_g.shape[1]
    tm = _tile(1024, n)
    tn = min(512, ff)
    return pl.pallas_call(
        _ffn_up_kernel,
        grid=(n // tm, pl.cdiv(ff, tn)),
        in_specs=[pl.BlockSpec((tm, d), lambda i, j: (i, 0)),
                  pl.BlockSpec((d, tn), lambda i, j: (0, j)),
                  pl.BlockSpec((d, tn), lambda i, j: (0, j))],
        out_specs=pl.BlockSpec((tm, tn), lambda i, j: (i, j)),
        out_shape=jax.ShapeDtypeStruct((n, ff), BF16),
        compiler_params=_params(("parallel", "arbitrary"),
                                2 * (tm * d * 2 + d * 2 * tn * 2 + tm * tn * 2) + 3 * tm * 2 * tn * 4),
        name="ffn_up",
    )(h, w_g, w_u)


def _ffn_down_kernel(a_ref, w_ref, x_ref, gate_ref, o_ref, *, alpha, nk, last):
    k = pl.program_id(2)
    tk = a_ref.shape[1]

    @pl.when(k == 0)
    def _():
        o_ref[...] = jnp.zeros(o_ref.shape, F32)

    if last == tk:
        o_ref[...] += jnp.dot(a_ref[...], w_ref[...], preferred_element_type=F32)
    else:
        @pl.when(k < nk - 1)
        def _():
            o_ref[...] += jnp.dot(a_ref[...], w_ref[...], preferred_element_type=F32)

        @pl.when(k == nk - 1)
        def _():
            o_ref[...] += jnp.dot(a_ref[:, :last], w_ref[:last, :], preferred_element_type=F32)

    @pl.when(k == nk - 1)
    def _():
        o_ref[...] = alpha * x_ref[...] + gate_ref[0] * o_ref[...]


def _ffn_down(a, w_d, x1, mod3, t_seq, i_gate, alpha):
    n, ff = a.shape
    d = w_d.shape[1]
    tm = _tile(1024, t_seq)
    tn = _tile(2048, d)
    tk = min(1024, ff)
    nk = pl.cdiv(ff, tk)
    last = ff - (nk - 1) * tk
    assert last % LANE == 0, "the partial contraction block must stay lane-aligned"
    per = t_seq // tm
    return pl.pallas_call(
        functools.partial(_ffn_down_kernel, alpha=alpha, nk=nk, last=last),
        grid=(n // tm, d // tn, nk),
        in_specs=[pl.BlockSpec((tm, tk), lambda i, j, k: (i, k)),
                  pl.BlockSpec((tk, tn), lambda i, j, k: (k, j)),
                  pl.BlockSpec((tm, tn), lambda i, j, k: (i, j), pipeline_mode=pl.Buffered(1)),
                  pl.BlockSpec((1, 1, tn), lambda i, j, k: ((i // per) * N_MOD + i_gate, 0, j))],
        out_specs=pl.BlockSpec((tm, tn), lambda i, j, k: (i, j)),
        out_shape=jax.ShapeDtypeStruct((n, d), F32),
        compiler_params=_params(("parallel", "parallel", "arbitrary"),
                                2 * (tm * tk * 2 + tk * tn * 2 + tm * tn * 4) + tm * tn * 4
                                + 2 * tm * tn * 4),
        name="ffn_down",
    )(a, w_d, x1, mod3)


def _ln_affine_kernel(z_ref, g_ref, b_ref, o_ref):
    o_ref[...] = _ln(z_ref[...]) * g_ref[...] + b_ref[...]


def _ln_affine(z, ln_g, ln_b):
    n, d = z.shape
    tm = _tile(256, n)
    return pl.pallas_call(
        _ln_affine_kernel,
        grid=(n // tm,),
        in_specs=[pl.BlockSpec((tm, d), lambda i: (i, 0)),
                  pl.BlockSpec((1, d), lambda i: (0, 0)),
                  pl.BlockSpec((1, d), lambda i: (0, 0))],
        out_specs=pl.BlockSpec((tm, d), lambda i: (i, 0)),
        out_shape=jax.ShapeDtypeStruct((n, d), F32),
        compiler_params=_params(("parallel",), 2 * tm * d * 8 + 4 * tm * d * 4),
        name="ln_out",
    )(z, ln_g.reshape(1, d), ln_b.reshape(1, d))


def _rope_tables(t_max):
    rows = t_max // GRID_W
    row = jnp.repeat(jnp.arange(rows, dtype=F32), GRID_W)
    col = jnp.tile(jnp.arange(GRID_W, dtype=F32), rows)
    inv = 1.0 / (ROPE_THETA ** (jnp.arange(0, HALF_ROT, 2, dtype=F32) / HALF_ROT))
    ang_r = row[:, None] * inv[None, :]
    ang_c = col[:, None] * inv[None, :]
    cr, sr, cc, sc = jnp.cos(ang_r), jnp.sin(ang_r), jnp.cos(ang_c), jnp.sin(ang_c)
    cos_t = jnp.concatenate([cr, cc, cr, cc], axis=-1)
    sin_t = jnp.concatenate([-sr, -sc, sr, sc], axis=-1)
    return cos_t, sin_t


def _rot_layout(w):
    lead = w.shape[:-1]
    w = w.reshape(lead + (-1, 2, 2, HALF_ROT // 2))
    return jnp.swapaxes(w, -3, -2).reshape(lead + (-1,))


def _prep_layer(w_in, q_norm, k_norm, w_br_a, w_br_b, w_o, w_gate, w_up, w_down):
    d = w_in.shape[0]
    n_heads = d // HEAD_DIM
    ha = n_heads // 2
    qa_w = ha * HEAD_DIM
    kv_w = (ha // KV_GROUP) * HEAD_DIM
    o_ka = qa_w
    o_va = o_ka + kv_w
    o_qb = o_va + kv_w
    o_kb = o_qb + qa_w
    o_vb = o_kb + kv_w
    o_ga = o_vb + kv_w
    w_qk = _rot_layout(w_in[:, :o_va]).astype(BF16)
    w_v_t = jnp.concatenate([w_in[:, o_va:o_qb], w_in[:, o_vb:o_ga]], axis=1).T.astype(BF16)
    w_qkb = _cast_cols(w_in, o_qb, o_vb - o_qb)
    w_gates = _cast_cols(w_in, o_ga, w_in.shape[1] - o_ga)
    gain = jnp.concatenate([jnp.tile(_rot_layout(q_norm), ha),
                            jnp.tile(_rot_layout(k_norm), ha // KV_GROUP)]).reshape(1, -1)
    q_scale = jnp.full((qa_w,), LOG2_E / math.sqrt(HEAD_DIM), F32)
    scale = jnp.concatenate([q_scale, jnp.ones((kv_w,), F32)]).reshape(1, -1)
    return dict(w_qk=w_qk, w_v_t=w_v_t, w_qkb=w_qkb, w_gates=w_gates, gain=gain, scale=scale,
                w_a=_cast_cols(w_br_a), w_b=_cast_cols(w_br_b), w_o=_cast_cols(w_o),
                w_g=_cast_cols(w_gate), w_u=_cast_cols(w_up), w_d=_cast_cols(w_down),
                ha=ha, kv_w=kv_w, qa_w=qa_w)


def _encoder_layer(x, mod, p, sink_row, win_bias, ln1_g, ln1_b, ln2_g, ln2_b, cos_t, sin_t, alpha):
    b, t, d = x.shape
    x2 = x.reshape(b * t, d)
    mod3 = mod.reshape(b * N_MOD, 1, d)
    ha, kv_w, qa_w = p["ha"], p["kv_w"], p["qa_w"]

    h = _ln_mod(x2, mod3, t, 1, 0)
    qk = _qk_proj(h, p["w_qk"], p["gain"], p["scale"], cos_t, sin_t, t)
    va_t, vb_t = _proj_t(h, p["w_v_t"], kv_w // HEAD_DIM)
    qkb = _proj(h, p["w_qkb"], p["scale"])
    gates = _proj(h, p["w_gates"], None)
    ya = _global_attention(qk, va_t, b, t, ha)
    yb = _window_attention(qkb, vb_t, win_bias, sink_row, b, t, ha, k_col_block=qa_w // kv_w,
                           vt_row_block=0)
    merged = _merge(ya, yb, p["w_a"], p["w_b"], gates)
    z = _out_proj(merged, p["w_o"], x2, mod3, t, 2, alpha)
    x1, h2 = _ln_pair(z, ln1_g, ln1_b, mod3, t, 4, 3)
    a = _ffn_up(h2, p["w_g"], p["w_u"])
    z2 = _ffn_down(a, p["w_d"], x1, mod3, t, 5, alpha)
    return _ln_affine(z2, ln2_g, ln2_b).reshape(b, t, d)


def kernel(x_prompt, x_sample, c_prompt, c_sample, w_ada, b_ada, w_in, q_norm_a, k_norm_a, sink_b,
           w_br_a, w_br_b, w_o, ln1_g, ln1_b, w_ffn_gate, w_ffn_up, w_ffn_down, ln2_g, ln2_b):
    depth = w_ada.shape[0]
    alpha = float((2.0 * depth) ** 0.25)
    d = x_prompt.shape[-1]
    bp, bs = c_prompt.shape[0], c_sample.shape[0]
    rows = -(-(bp + bs) // 8) * 8
    cos_t, sin_t = _rope_tables(max(x_prompt.shape[1], x_sample.shape[1]))
    y_p, y_s = x_prompt, x_sample
    c_all = jnp.concatenate([c_prompt, c_sample, jnp.zeros((rows - bp - bs, d), F32)], axis=0)
    win_bias = _window_bias(sink_b.shape[1])
    for l in range(depth):
        mod = _ada(c_all, w_ada[l], b_ada[l]).reshape(rows, N_MOD, d)
        p = _prep_layer(w_in[l], q_norm_a[l], k_norm_a[l], w_br_a[l], w_br_b[l], w_o[l],
                        w_ffn_gate[l], w_ffn_up[l], w_ffn_down[l])
        sink_row = jnp.repeat(sink_b[l], HEAD_DIM).reshape(1, -1)
        args = (p, sink_row, win_bias, ln1_g[l], ln1_b[l], ln2_g[l], ln2_b[l], cos_t, sin_t, alpha)
        y_p = _encoder_layer(y_p, mod[:bp], *args)
        y_s = _encoder_layer(y_s, mod[bp:bp + bs], *args)
    return (y_p, y_s)
```

```python
import functools
import math

import jax
import jax.numpy as jnp
from jax import lax
from jax.experimental import pallas as pl
from jax.experimental.pallas import tpu as pltpu

F32 = jnp.float32
BF16 = jnp.bfloat16

HEAD_DIM = 128
GRID_W = 64
Q_BLOCK = 128
WINDOW = 128
HALF_ROT = HEAD_DIM // 2
ROPE_THETA = 10000.0
N_MOD = 6
LN_EPS = 1e-5
RMS_EPS = 1e-6
KV_GROUP = 4
MASK_VALUE = -1e30
LOG2_E = 1.4426950408889634

ONES_ROWS = 16
V_ROWS = HEAD_DIM + ONES_ROWS
LANE = 128
VMEM_CAP_BYTES = 60 * 1024 * 1024
CAST_BLOCK_BYTES = 4 * 1024 * 1024


def _tile(pref, *dims):
    t = (min(pref, *dims) // LANE) * LANE
    while t >= LANE:
        if all(d % t == 0 for d in dims):
            return t
        t -= LANE
    return min(dims)


def _params(semantics, vmem_bytes):
    limit = int(min(max(vmem_bytes, 16 * 1024 * 1024), VMEM_CAP_BYTES))
    return pltpu.CompilerParams(dimension_semantics=semantics, vmem_limit_bytes=limit)


def _ln(x):
    mu = jnp.mean(x, axis=-1, keepdims=True)
    xc = x - mu
    return xc * lax.rsqrt(jnp.mean(xc * xc, axis=-1, keepdims=True) + LN_EPS)


def _ada_kernel(c_ref, w_ref, b_ref, o_ref):
    c = c_ref[...]
    a = (c * jax.nn.sigmoid(c)).astype(BF16)
    o_ref[...] = jnp.dot(a, w_ref[...].astype(BF16), preferred_element_type=F32) + b_ref[...]


def _ada(c_pad, w_ada, b_ada):
    rows, d = c_pad.shape
    n = w_ada.shape[1]
    tn = _tile(512, n)
    return pl.pallas_call(
        _ada_kernel,
        grid=(n // tn,),
        in_specs=[pl.BlockSpec((rows, d), lambda j: (0, 0)),
                  pl.BlockSpec((d, tn), lambda j: (0, j)),
                  pl.BlockSpec((1, tn), lambda j: (0, j))],
        out_specs=pl.BlockSpec((rows, tn), lambda j: (0, j)),
        out_shape=jax.ShapeDtypeStruct((rows, n), F32),
        compiler_params=_params(("parallel",), 2 * d * tn * 4 + d * tn * 2 + (4 << 20)),
        name="ada_mod",
    )(c_pad, w_ada, b_ada.reshape(1, n))


def _ln_mod_kernel(x_ref, sc_ref, sh_ref, o_ref):
    y = _ln(x_ref[...])
    o_ref[...] = (y * (1.0 + sc_ref[0]) + sh_ref[0]).astype(o_ref.dtype)


def _ln_mod(x2, mod3, t_seq, i_scale, i_shift):
    n, d = x2.shape
    tm = _tile(256, t_seq)
    per = t_seq // tm
    return pl.pallas_call(
        _ln_mod_kernel,
        grid=(n // tm,),
        in_specs=[pl.BlockSpec((tm, d), lambda i: (i, 0)),
                  pl.BlockSpec((1, 1, d), lambda i: ((i // per) * N_MOD + i_scale, 0, 0)),
                  pl.BlockSpec((1, 1, d), lambda i: ((i // per) * N_MOD + i_shift, 0, 0))],
        out_specs=pl.BlockSpec((tm, d), lambda i: (i, 0)),
        out_shape=jax.ShapeDtypeStruct((n, d), BF16),
        compiler_params=_params(("parallel",), 2 * tm * d * 6 + 4 * tm * d * 4),
        name="ln_mod",
    )(x2, mod3, mod3)


def _qk_kernel(h_ref, w_ref, g_ref, s_ref, cos_ref, sin_ref, o_ref, *, chunk):
    h = h_ref[...]
    cos = cos_ref[...]
    sin = sin_ref[...]
    for c in range(o_ref.shape[1] // chunk):
        acc = jnp.dot(h, w_ref[:, c * chunk:(c + 1) * chunk], preferred_element_type=F32)
        for hh in range(chunk // HEAD_DIM):
            sl = slice(c * chunk + hh * HEAD_DIM, c * chunk + (hh + 1) * HEAD_DIM)
            x = acc[:, hh * HEAD_DIM:(hh + 1) * HEAD_DIM]
            y = x * lax.rsqrt(jnp.mean(x * x, axis=-1, keepdims=True) + RMS_EPS) * g_ref[:, sl]
            partner = pltpu.roll(y, HALF_ROT, 1)
            o_ref[:, sl] = ((y * cos + partner * sin) * s_ref[:, sl]).astype(o_ref.dtype)


def _qk_proj(h, w_qk, gain, scale, cos_t, sin_t, t_seq):
    n, d = h.shape
    nw = w_qk.shape[1]
    tm = _tile(512, t_seq)
    per = t_seq // tm
    return pl.pallas_call(
        functools.partial(_qk_kernel, chunk=_tile(256, nw)),
        grid=(n // tm,),
        in_specs=[pl.BlockSpec((tm, d), lambda i: (i, 0)),
                  pl.BlockSpec((d, nw), lambda i: (0, 0), pipeline_mode=pl.Buffered(1)),
                  pl.BlockSpec((1, nw), lambda i: (0, 0)),
                  pl.BlockSpec((1, nw), lambda i: (0, 0)),
                  pl.BlockSpec((tm, HEAD_DIM), lambda i: (i % per, 0)),
                  pl.BlockSpec((tm, HEAD_DIM), lambda i: (i % per, 0))],
        out_specs=pl.BlockSpec((tm, nw), lambda i: (i, 0)),
        out_shape=jax.ShapeDtypeStruct((n, nw), BF16),
        compiler_params=_params(("parallel",),
                                d * nw * 2 + 2 * (tm * d * 2 + tm * nw * 2 + 2 * tm * HEAD_DIM * 4)
                                + 16 * tm * 256 * 4),
        name="qk_proj_rope",
    )(h, w_qk, gain, scale, cos_t, sin_t)


def _cast_kernel(w_ref, o_ref):
    o_ref[...] = w_ref[...].astype(o_ref.dtype)


def _cast_cols(w, col0=0, ncols=None):
    rows, width = w.shape
    ncols = width if ncols is None else ncols
    if col0 == 0 and ncols == width:
        tc = width
    else:
        tc = _tile(2048, ncols, *((col0,) if col0 else ()))
    tr = max(8, min(rows, (CAST_BLOCK_BYTES // (4 * tc)) // 8 * 8))
    while rows % tr:
        tr -= 8
    c0 = col0 // tc
    return pl.pallas_call(
        _cast_kernel,
        grid=(rows // tr, ncols // tc),
        in_specs=[pl.BlockSpec((tr, tc), lambda i, j: (i, c0 + j))],
        out_specs=pl.BlockSpec((tr, tc), lambda i, j: (i, j)),
        out_shape=jax.ShapeDtypeStruct((rows, ncols), BF16),
        compiler_params=_params(("parallel", "parallel"), 2 * tr * tc * 6 + tr * tc * 4),
        name="cast_bf16",
    )(w)


def _mm_scaled_kernel(x_ref, w_ref, s_ref, o_ref):
    acc = jnp.dot(x_ref[...], w_ref[...], preferred_element_type=F32)
    o_ref[...] = (acc * s_ref[...]).astype(o_ref.dtype)


def _mm_kernel(x_ref, w_ref, o_ref):
    o_ref[...] = jnp.dot(x_ref[...], w_ref[...], preferred_element_type=F32).astype(o_ref.dtype)


def _proj(x, w, col_scale, tn_pref=512):
    n, d = x.shape
    nw = w.shape[1]
    tm = _tile(1024, n)
    tn = _tile(tn_pref, nw)
    scaled = col_scale is not None
    in_specs = [pl.BlockSpec((tm, d), lambda i, j: (i, 0)),
                pl.BlockSpec((d, tn), lambda i, j: (0, j))]
    if scaled:
        in_specs.append(pl.BlockSpec((1, tn), lambda i, j: (0, j)))
    return pl.pallas_call(
        _mm_scaled_kernel if scaled else _mm_kernel,
        grid=(n // tm, nw // tn),
        in_specs=in_specs,
        out_specs=pl.BlockSpec((tm, tn), lambda i, j: (i, j)),
        out_shape=jax.ShapeDtypeStruct((n, nw), BF16),
        compiler_params=_params(("parallel", "arbitrary"),
                                2 * (tm * d * 2 + d * tn * 2 + tm * tn * 2) + 2 * tm * tn * 4),
        name="proj",
    )(*((x, w, col_scale) if scaled else (x, w)))


def _mm_t_kernel(wt_ref, x_ref, oa_ref, ob_ref):
    res = lax.dot_general(wt_ref[...], x_ref[...], (((1,), (1,)), ((), ())), preferred_element_type=F32)
    nb = ob_ref.shape[0]
    tm = x_ref.shape[0]
    for j in range(oa_ref.shape[0] // V_ROWS):
        oa_ref[j * V_ROWS:j * V_ROWS + HEAD_DIM, :] = res[j * HEAD_DIM:(j + 1) * HEAD_DIM, :].astype(oa_ref.dtype)
        oa_ref[j * V_ROWS + HEAD_DIM:(j + 1) * V_ROWS, :] = jnp.ones((ONES_ROWS, tm), oa_ref.dtype)
    ob_ref[...] = res[res.shape[0] - nb:, :].astype(ob_ref.dtype)


def _proj_t(x, w_t, n_a_heads):
    n, d = x.shape
    nw = w_t.shape[0]
    nb = nw - n_a_heads * HEAD_DIM
    tm = _tile(1024, n)
    return pl.pallas_call(
        _mm_t_kernel,
        grid=(n // tm,),
        in_specs=[pl.BlockSpec((nw, d), lambda i: (0, 0)),
                  pl.BlockSpec((tm, d), lambda i: (i, 0))],
        out_specs=[pl.BlockSpec((n_a_heads * V_ROWS, tm), lambda i: (0, i)),
                   pl.BlockSpec((nb, tm), lambda i: (0, i))],
        out_shape=[jax.ShapeDtypeStruct((n_a_heads * V_ROWS, n), BF16),
                   jax.ShapeDtypeStruct((nb, n), BF16)],
        compiler_params=_params(("parallel",),
                                2 * (tm * d * 2 + nw * d * 2 + 2 * nw * tm * 2) + 2 * nw * tm * 4),
        name="proj_t",
    )(w_t, x)


def _flash_kernel(q_ref, k_ref, vt_ref, o_ref, qs_ref, sa_ref, sb_ref, m_ref, acc_ref, *, tq, tk, nk):
    for g in range(KV_GROUP):
        qs_ref[g * tq:(g + 1) * tq, :] = q_ref[:, g * HEAD_DIM:(g + 1) * HEAD_DIM]
    m_ref[...] = jnp.full(m_ref.shape, -jnp.inf, F32)
    acc_ref[...] = jnp.zeros(acc_ref.shape, F32)

    def scores(idx, s_ref):
        start = pl.multiple_of(idx * tk, tk)
        s_ref[...] = lax.dot_general(k_ref[pl.ds(start, tk), :], qs_ref[...], (((1,), (1,)), ((), ())),
                                     preferred_element_type=F32)

    def update(idx, s_ref):
        start = pl.multiple_of(idx * tk, tk)
        s = s_ref[...]
        m_prev = m_ref[...]
        m_new = jnp.maximum(m_prev, jnp.max(s, axis=0, keepdims=True))
        alpha = jnp.exp2(m_prev - m_new)
        p = jnp.exp2(s - m_new).astype(BF16)
        acc_ref[...] = alpha * acc_ref[...] + jnp.dot(vt_ref[:, pl.ds(start, tk)], p,
                                                      preferred_element_type=F32)
        m_ref[...] = m_new

    scores(0, sa_ref)
    pairs = (nk - 1) // 2

    def body(j, carry):
        scores(2 * j + 1, sb_ref)
        update(2 * j, sa_ref)
        scores(2 * j + 2, sa_ref)
        update(2 * j + 1, sb_ref)
        return carry

    lax.fori_loop(0, pairs, body, 0)
    if nk - 2 * pairs == 2:
        scores(nk - 1, sb_ref)
        update(nk - 2, sa_ref)
        update(nk - 1, sb_ref)
    else:
        update(nk - 1, sa_ref)
    out = (acc_ref[:HEAD_DIM, :] / acc_ref[HEAD_DIM:HEAD_DIM + 1, :]).T
    for g in range(KV_GROUP):
        o_ref[:, g * HEAD_DIM:(g + 1) * HEAD_DIM] = out[g * tq:(g + 1) * tq, :].astype(o_ref.dtype)


def _global_attention(qk, v_t, batch, t_seq, n_q_heads):
    n = qk.shape[0]
    n_kv = n_q_heads // KV_GROUP
    tq = _tile(256, t_seq)
    tk = _tile(512, t_seq)
    nq = t_seq // tq
    nk = t_seq // tk
    gw = KV_GROUP * HEAD_DIM
    kern = functools.partial(_flash_kernel, tq=tq, tk=tk, nk=nk)
    rows = KV_GROUP * tq
    return pl.pallas_call(
        kern,
        grid=(batch, n_kv, nq),
        in_specs=[pl.BlockSpec((tq, gw), lambda b, h, i: (b * nq + i, h)),
                  pl.BlockSpec((t_seq, HEAD_DIM), lambda b, h, i: (b, n_q_heads + h)),
                  pl.BlockSpec((V_ROWS, t_seq), lambda b, h, i: (h, b))],
        out_specs=pl.BlockSpec((tq, gw), lambda b, h, i: (b * nq + i, h)),
        out_shape=jax.ShapeDtypeStruct((n, n_q_heads * HEAD_DIM), BF16),
        scratch_shapes=[pltpu.VMEM((rows, HEAD_DIM), BF16),
                        pltpu.VMEM((tk, rows), F32),
                        pltpu.VMEM((tk, rows), F32),
                        pltpu.VMEM((1, rows), F32),
                        pltpu.VMEM((V_ROWS, rows), F32)],
        compiler_params=_params(("parallel", "parallel", "arbitrary"),
                                2 * (2 * tq * gw * 2 + 2 * t_seq * HEAD_DIM * 2)
                                + rows * HEAD_DIM * 6 + 16 * rows * 4 + 8 * rows * tk * 4),
        name="global_attention",
    )(qk, qk, v_t)


def _window_bias(n_heads):
    key = jnp.arange(3 * Q_BLOCK, dtype=jnp.int32)[:, None] - Q_BLOCK
    qpos = jnp.arange(Q_BLOCK, dtype=jnp.int32)[None, :]
    dist = jnp.abs(qpos - key)
    slopes = 2.0 ** (-8.0 * jnp.arange(1, n_heads + 1, dtype=F32) / n_heads)
    bias = -slopes[:, None, None] * dist.astype(F32)[None] * LOG2_E
    n_kv = n_heads // KV_GROUP
    bias = bias.reshape(n_kv, KV_GROUP, 3 * Q_BLOCK, Q_BLOCK).transpose(0, 2, 1, 3)
    bias = bias.reshape(n_kv, 3 * Q_BLOCK, KV_GROUP * Q_BLOCK)
    in_window = (dist <= WINDOW)
    in_window = jnp.tile(in_window, (1, KV_GROUP))[None]
    row = jnp.arange(3 * Q_BLOCK)[None, :, None]
    not_prev = row >= Q_BLOCK
    not_next = row < 2 * Q_BLOCK
    variants = [in_window, in_window & not_prev, in_window & not_next, in_window & not_prev & not_next]
    return jnp.stack([jnp.where(v, bias, MASK_VALUE) for v in variants])


def _window_kernel(q_ref, kp_ref, kc_ref, kn_ref, vp_ref, vc_ref, vn_ref, bias_ref, sink_ref, o_ref, *,
                   n_kv):
    gw = KV_GROUP * HEAD_DIM
    for kv in range(n_kv):
        ks = slice(kv * HEAD_DIM, (kv + 1) * HEAD_DIM)
        kcat = jnp.concatenate([kp_ref[:, ks], kc_ref[:, ks], kn_ref[:, ks]], axis=0)
        vtcat = jnp.concatenate([vp_ref[ks, :], vc_ref[ks, :], vn_ref[ks, :]], axis=1)
        qs = jnp.concatenate([q_ref[:, (kv * KV_GROUP + g) * HEAD_DIM:(kv * KV_GROUP + g + 1) * HEAD_DIM]
                              for g in range(KV_GROUP)], axis=0)
        s = lax.dot_general(kcat, qs, (((1,), (1,)), ((), ())), preferred_element_type=F32)
        s = s + bias_ref[0, kv]
        sink = sink_ref[:, kv * gw:(kv + 1) * gw] * LOG2_E
        m = jnp.maximum(jnp.max(s, axis=0, keepdims=True), sink)
        p = jnp.exp2(s - m)
        den = jnp.sum(p, axis=0, keepdims=True) + jnp.exp2(sink - m)
        pv = jnp.dot(vtcat, p.astype(BF16), preferred_element_type=F32)
        out = (pv / den).T
        for g in range(KV_GROUP):
            hd = kv * KV_GROUP + g
            o_ref[:, hd * HEAD_DIM:(hd + 1) * HEAD_DIM] = out[g * Q_BLOCK:(g + 1) * Q_BLOCK, :].astype(o_ref.dtype)


def _window_attention(rest, v_t, bias, sink_row, batch, t_seq, n_heads, k_col_block, vt_row_block):
    n = rest.shape[0]
    nb = t_seq // Q_BLOCK
    n_kv = n_heads // KV_GROUP
    qw = n_heads * HEAD_DIM
    kw = n_kv * HEAD_DIM

    def prev(b, i):
        return b * nb + jnp.maximum(i - 1, 0)

    def cur(b, i):
        return b * nb + i

    def nxt(b, i):
        return b * nb + jnp.minimum(i + 1, nb - 1)

    def variant(b, i):
        return (i == 0).astype(jnp.int32) + 2 * (i == nb - 1).astype(jnp.int32)

    k_specs = [pl.BlockSpec((Q_BLOCK, kw), functools.partial(lambda b, i, f: (f(b, i), k_col_block), f=f))
               for f in (prev, cur, nxt)]
    v_specs = [pl.BlockSpec((kw, Q_BLOCK), functools.partial(lambda b, i, f: (vt_row_block, f(b, i)), f=f))
               for f in (prev, cur, nxt)]
    return pl.pallas_call(
        functools.partial(_window_kernel, n_kv=n_kv),
        grid=(batch, nb),
        in_specs=[pl.BlockSpec((Q_BLOCK, qw), lambda b, i: (b * nb + i, 0))] + k_specs + v_specs
        + [pl.BlockSpec((1,) + bias.shape[1:], lambda b, i: (variant(b, i), 0, 0, 0)),
           pl.BlockSpec((1, qw), lambda b, i: (0, 0))],
        out_specs=pl.BlockSpec((Q_BLOCK, qw), lambda b, i: (b * nb + i, 0)),
        out_shape=jax.ShapeDtypeStruct((n, qw), BF16),
        compiler_params=_params(("parallel", "arbitrary"),
                                2 * (bias[0].size * 4 + 2 * Q_BLOCK * qw * 2 + 12 * Q_BLOCK * kw)
                                + 16 * 3 * Q_BLOCK * KV_GROUP * Q_BLOCK * 4),
        name="window_attention",
    )(rest, rest, rest, rest, v_t, v_t, v_t, bias, sink_row)


def _merge_kernel(ya_ref, yb_ref, wa_ref, wb_ref, ga_ref, gb_ref, o_ref):
    a = jnp.dot(ya_ref[...], wa_ref[...], preferred_element_type=F32)
    b = jnp.dot(yb_ref[...], wb_ref[...], preferred_element_type=F32)
    ga = jax.nn.sigmoid(ga_ref[...].astype(F32))
    gb = jax.nn.sigmoid(gb_ref[...].astype(F32))
    o_ref[...] = (ga * a + gb * b).astype(o_ref.dtype)


def _merge(ya, yb, w_a, w_b, gates):
    n, ka = ya.shape
    kb = yb.shape[1]
    d = w_a.shape[1]
    tm = _tile(1024, n)
    tn = _tile(512, d)
    ga_blk = 0
    gb_blk = d // tn
    return pl.pallas_call(
        _merge_kernel,
        grid=(n // tm, d // tn),
        in_specs=[pl.BlockSpec((tm, ka), lambda i, j: (i, 0)),
                  pl.BlockSpec((tm, kb), lambda i, j: (i, 0)),
                  pl.BlockSpec((ka, tn), lambda i, j: (0, j)),
                  pl.BlockSpec((kb, tn), lambda i, j: (0, j)),
                  pl.BlockSpec((tm, tn), lambda i, j: (i, ga_blk + j)),
                  pl.BlockSpec((tm, tn), lambda i, j: (i, gb_blk + j))],
        out_specs=pl.BlockSpec((tm, tn), lambda i, j: (i, j)),
        out_shape=jax.ShapeDtypeStruct((n, d), BF16),
        compiler_params=_params(("parallel", "arbitrary"),
                                2 * (tm * (ka + kb) * 2 + (ka + kb) * tn * 2 + 3 * tm * tn * 2)
                                + 6 * tm * tn * 4),
        name="branch_merge",
    )(ya, yb, w_a, w_b, gates, gates)


def _resid_kernel(m_ref, w_ref, x_ref, g_ref, o_ref, *, alpha):
    acc = jnp.dot(m_ref[...], w_ref[...], preferred_element_type=F32)
    o_ref[...] = alpha * x_ref[...] + g_ref[0] * acc


def _out_proj(merged, w_o, x2, mod3, t_seq, i_gate, alpha):
    n, d = x2.shape
    k = merged.shape[1]
    tm = _tile(1024, t_seq)
    tn = _tile(512, d)
    per = t_seq // tm
    return pl.pallas_call(
        functools.partial(_resid_kernel, alpha=alpha),
        grid=(n // tm, d // tn),
        in_specs=[pl.BlockSpec((tm, k), lambda i, j: (i, 0)),
                  pl.BlockSpec((k, tn), lambda i, j: (0, j)),
                  pl.BlockSpec((tm, tn), lambda i, j: (i, j)),
                  pl.BlockSpec((1, 1, tn), lambda i, j: ((i // per) * N_MOD + i_gate, 0, j))],
        out_specs=pl.BlockSpec((tm, tn), lambda i, j: (i, j)),
        out_shape=jax.ShapeDtypeStruct((n, d), F32),
        compiler_params=_params(("parallel", "arbitrary"),
                                2 * (tm * k * 2 + k * tn * 2 + 2 * tm * tn * 4) + 2 * tm * tn * 4),
        name="out_proj_residual",
    )(merged, w_o, x2, mod3)


def _ln_pair_kernel(z_ref, g_ref, b_ref, sc_ref, sh_ref, x1_ref, h_ref):
    x1 = _ln(z_ref[...]) * g_ref[...] + b_ref[...]
    x1_ref[...] = x1
    h_ref[...] = (_ln(x1) * (1.0 + sc_ref[0]) + sh_ref[0]).astype(h_ref.dtype)


def _ln_pair(z, ln_g, ln_b, mod3, t_seq, i_scale, i_shift):
    n, d = z.shape
    tm = _tile(256, t_seq)
    per = t_seq // tm
    return pl.pallas_call(
        _ln_pair_kernel,
        grid=(n // tm,),
        in_specs=[pl.BlockSpec((tm, d), lambda i: (i, 0)),
                  pl.BlockSpec((1, d), lambda i: (0, 0)),
                  pl.BlockSpec((1, d), lambda i: (0, 0)),
                  pl.BlockSpec((1, 1, d), lambda i: ((i // per) * N_MOD + i_scale, 0, 0)),
                  pl.BlockSpec((1, 1, d), lambda i: ((i // per) * N_MOD + i_shift, 0, 0))],
        out_specs=[pl.BlockSpec((tm, d), lambda i: (i, 0)),
                   pl.BlockSpec((tm, d), lambda i: (i, 0))],
        out_shape=[jax.ShapeDtypeStruct((n, d), F32), jax.ShapeDtypeStruct((n, d), BF16)],
        compiler_params=_params(("parallel",), 2 * tm * d * 10 + 6 * tm * d * 4),
        name="ln1_ln2mod",
    )(z, ln_g.reshape(1, d), ln_b.reshape(1, d), mod3, mod3)


def _ffn_up_kernel(h_ref, wg_ref, wu_ref, o_ref):
    h = h_ref[...]
    g = jnp.dot(h, wg_ref[...], preferred_element_type=F32)
    u = jnp.dot(h, wu_ref[...], preferred_element_type=F32)
    o_ref[...] = (g * jax.nn.sigmoid(g) * u).astype(o_ref.dtype)


def _ffn_up(h, w_g, w_u):
    n, d = h.shape
    ff = w_g.shape[1]
    tm = _tile(1024, n)
    tn = min(512, ff)
    return pl.pallas_call(
        _ffn_up_kernel,
        grid=(n // tm, pl.cdiv(ff, tn)),
        in_specs=[pl.BlockSpec((tm, d), lambda i, j: (i, 0)),
                  pl.BlockSpec((d, tn), lambda i, j: (0, j)),
                  pl.BlockSpec((d, tn), lambda i, j: (0, j))],
        out_specs=pl.BlockSpec((tm, tn), lambda i, j: (i, j)),
        out_shape=jax.ShapeDtypeStruct((n, ff), BF16),
        compiler_params=_params(("parallel", "arbitrary"),
                                2 * (tm * d * 2 + d * 2 * tn * 2 + tm * tn * 2) + 3 * tm * 2 * tn * 4),
        name="ffn_up",
    )(h, w_g, w_u)


def _ffn_down_kernel(a_ref, w_ref, x_ref, gate_ref, g_ref, b_ref, o_ref, *, alpha, nk, last, rows):
    k = pl.program_id(1)
    tk = a_ref.shape[1]

    @pl.when(k == 0)
    def _():
        o_ref[...] = jnp.zeros(o_ref.shape, F32)

    if last == tk:
        o_ref[...] += jnp.dot(a_ref[...], w_ref[...], preferred_element_type=F32)
    else:
        @pl.when(k < nk - 1)
        def _():
            o_ref[...] += jnp.dot(a_ref[...], w_ref[...], preferred_element_type=F32)

        @pl.when(k == nk - 1)
        def _():
            o_ref[...] += jnp.dot(a_ref[:, :last], w_ref[:last, :], preferred_element_type=F32)

    @pl.when(k == nk - 1)
    def _():
        def chunk(r, carry):
            sl = pl.ds(pl.multiple_of(r * rows, rows), rows)
            z = alpha * x_ref[sl, :] + gate_ref[0] * o_ref[sl, :]
            o_ref[sl, :] = _ln(z) * g_ref[...] + b_ref[...]
            return carry

        lax.fori_loop(0, o_ref.shape[0] // rows, chunk, 0)


def _ffn_down(a, w_d, x1, ln_g, ln_b, mod3, t_seq, i_gate, alpha):
    n, ff = a.shape
    d = w_d.shape[1]
    tm = _tile(512, t_seq)
    tk = min(1024, ff)
    nk = pl.cdiv(ff, tk)
    last = ff - (nk - 1) * tk
    assert last % LANE == 0, "the partial contraction block must stay lane-aligned"
    per = t_seq // tm
    return pl.pallas_call(
        functools.partial(_ffn_down_kernel, alpha=alpha, nk=nk, last=last, rows=_tile(LANE, tm)),
        grid=(n // tm, nk),
        in_specs=[pl.BlockSpec((tm, tk), lambda i, k: (i, k)),
                  pl.BlockSpec((tk, d), lambda i, k: (k, 0)),
                  pl.BlockSpec((tm, d), lambda i, k: (i, 0)),
                  pl.BlockSpec((1, 1, d), lambda i, k: ((i // per) * N_MOD + i_gate, 0, 0)),
                  pl.BlockSpec((1, d), lambda i, k: (0, 0)),
                  pl.BlockSpec((1, d), lambda i, k: (0, 0))],
        out_specs=pl.BlockSpec((tm, d), lambda i, k: (i, 0)),
        out_shape=jax.ShapeDtypeStruct((n, d), F32),
        compiler_params=_params(("parallel", "arbitrary"),
                                2 * (tm * tk * 2 + tk * d * 2 + 2 * tm * d * 4) + 2 * tm * d * 4),
        name="ffn_down_ln",
    )(a, w_d, x1, mod3, ln_g.reshape(1, d), ln_b.reshape(1, d))


def _rope_tables(t_max):
    rows = t_max // GRID_W
    row = jnp.repeat(jnp.arange(rows, dtype=F32), GRID_W)
    col = jnp.tile(jnp.arange(GRID_W, dtype=F32), rows)
    inv = 1.0 / (ROPE_THETA ** (jnp.arange(0, HALF_ROT, 2, dtype=F32) / HALF_ROT))
    ang_r = row[:, None] * inv[None, :]
    ang_c = col[:, None] * inv[None, :]
    cr, sr, cc, sc = jnp.cos(ang_r), jnp.sin(ang_r), jnp.cos(ang_c), jnp.sin(ang_c)
    cos_t = jnp.concatenate([cr, cc, cr, cc], axis=-1)
    sin_t = jnp.concatenate([-sr, -sc, sr, sc], axis=-1)
    return cos_t, sin_t


def _rot_layout(w):
    lead = w.shape[:-1]
    w = w.reshape(lead + (-1, 2, 2, HALF_ROT // 2))
    return jnp.swapaxes(w, -3, -2).reshape(lead + (-1,))


def _prep_layer(w_in, q_norm, k_norm, w_br_a, w_br_b, w_o, w_gate, w_up, w_down):
    d = w_in.shape[0]
    n_heads = d // HEAD_DIM
    ha = n_heads // 2
    qa_w = ha * HEAD_DIM
    kv_w = (ha // KV_GROUP) * HEAD_DIM
    o_ka = qa_w
    o_va = o_ka + kv_w
    o_qb = o_va + kv_w
    o_kb = o_qb + qa_w
    o_vb = o_kb + kv_w
    o_ga = o_vb + kv_w
    w_qk = _rot_layout(_cast_cols(w_in, 0, o_va))
    w_v_t = jnp.concatenate([_cast_cols(w_in, o_va, kv_w), _cast_cols(w_in, o_vb, kv_w)], axis=1).T
    w_qkb = _cast_cols(w_in, o_qb, o_vb - o_qb)
    w_gates = _cast_cols(w_in, o_ga, w_in.shape[1] - o_ga)
    gain = jnp.concatenate([jnp.tile(_rot_layout(q_norm), ha),
                            jnp.tile(_rot_layout(k_norm), ha // KV_GROUP)]).reshape(1, -1)
    q_scale = jnp.full((qa_w,), LOG2_E / math.sqrt(HEAD_DIM), F32)
    scale = jnp.concatenate([q_scale, jnp.ones((kv_w,), F32)]).reshape(1, -1)
    return dict(w_qk=w_qk, w_v_t=w_v_t, w_qkb=w_qkb, w_gates=w_gates, gain=gain, scale=scale,
                w_a=_cast_cols(w_br_a), w_b=_cast_cols(w_br_b), w_o=_cast_cols(w_o),
                w_g=_cast_cols(w_gate), w_u=_cast_cols(w_up), w_d=_cast_cols(w_down),
                ha=ha, kv_w=kv_w, qa_w=qa_w)


def _encoder_layer(x, mod, p, sink_row, win_bias, ln1_g, ln1_b, ln2_g, ln2_b, cos_t, sin_t, alpha):
    b, t, d = x.shape
    x2 = x.reshape(b * t, d)
    mod3 = mod.reshape(b * N_MOD, 1, d)
    ha, kv_w, qa_w = p["ha"], p["kv_w"], p["qa_w"]

    h = _ln_mod(x2, mod3, t, 1, 0)
    qk = _qk_proj(h, p["w_qk"], p["gain"], p["scale"], cos_t, sin_t, t)
    va_t, vb_t = _proj_t(h, p["w_v_t"], kv_w // HEAD_DIM)
    qkb = _proj(h, p["w_qkb"], p["scale"])
    gates = _proj(h, p["w_gates"], None)
    ya = _global_attention(qk, va_t, b, t, ha)
    yb = _window_attention(qkb, vb_t, win_bias, sink_row, b, t, ha, k_col_block=qa_w // kv_w,
                           vt_row_block=0)
    merged = _merge(ya, yb, p["w_a"], p["w_b"], gates)
    z = _out_proj(merged, p["w_o"], x2, mod3, t, 2, alpha)
    x1, h2 = _ln_pair(z, ln1_g, ln1_b, mod3, t, 4, 3)
    a = _ffn_up(h2, p["w_g"], p["w_u"])
    y = _ffn_down(a, p["w_d"], x1, ln2_g, ln2_b, mod3, t, 5, alpha)
    return y.reshape(b, t, d)


def kernel(x_prompt, x_sample, c_prompt, c_sample, w_ada, b_ada, w_in, q_norm_a, k_norm_a, sink_b,
           w_br_a, w_br_b, w_o, ln1_g, ln1_b, w_ffn_gate, w_ffn_up, w_ffn_down, ln2_g, ln2_b):
    depth = w_ada.shape[0]
    alpha = float((2.0 * depth) ** 0.25)
    d = x_prompt.shape[-1]
    bp, bs = c_prompt.shape[0], c_sample.shape[0]
    rows = -(-(bp + bs) // 8) * 8
    cos_t, sin_t = _rope_tables(max(x_prompt.shape[1], x_sample.shape[1]))
    y_p, y_s = x_prompt, x_sample
    c_all = jnp.concatenate([c_prompt, c_sample, jnp.zeros((rows - bp - bs, d), F32)], axis=0)
    win_bias = _window_bias(sink_b.shape[1])
    for l in range(depth):
        mod = _ada(c_all, w_ada[l], b_ada[l]).reshape(rows, N_MOD, d)
        p = _prep_layer(w_in[l], q_norm_a[l], k_norm_a[l], w_br_a[l], w_br_b[l], w_o[l],
                        w_ffn_gate[l], w_ffn_up[l], w_ffn_down[l])
        sink_row = jnp.repeat(sink_b[l], HEAD_DIM).reshape(1, -1)
        args = (p, sink_row, win_bias, ln1_g[l], ln1_b[l], ln2_g[l], ln2_b[l], cos_t, sin_t, alpha)
        y_p = _encoder_layer(y_p, mod[:bp], *args)
        y_s = _encoder_layer(y_s, mod[bp:bp + bs], *args)
    return (y_p, y_s)
```

```python
import functools
import math

import jax
import jax.numpy as jnp
from jax import lax
from jax.experimental import pallas as pl
from jax.experimental.pallas import tpu as pltpu

F32 = jnp.float32
BF16 = jnp.bfloat16

HEAD_DIM = 128
GRID_W = 64
Q_BLOCK = 128
WINDOW = 128
HALF_ROT = HEAD_DIM // 2
ROPE_THETA = 10000.0
N_MOD = 6
LN_EPS = 1e-5
RMS_EPS = 1e-6
KV_GROUP = 4
MASK_VALUE = -1e30
LOG2_E = 1.4426950408889634

ONES_ROWS = 16
V_ROWS = HEAD_DIM + ONES_ROWS
LANE = 128
VMEM_CAP_BYTES = 60 * 1024 * 1024
CAST_BLOCK_BYTES = 4 * 1024 * 1024


def _tile(pref, *dims):
    t = (min(pref, *dims) // LANE) * LANE
    while t >= LANE:
        if all(d % t == 0 for d in dims):
            return t
        t -= LANE
    return min(dims)


def _params(semantics, vmem_bytes):
    limit = int(min(max(vmem_bytes, 16 * 1024 * 1024), VMEM_CAP_BYTES))
    return pltpu.CompilerParams(dimension_semantics=semantics, vmem_limit_bytes=limit)


def _ln(x):
    mu = jnp.mean(x, axis=-1, keepdims=True)
    xc = x - mu
    return xc * lax.rsqrt(jnp.mean(xc * xc, axis=-1, keepdims=True) + LN_EPS)


def _ada_kernel(c_ref, w_ref, b_ref, o_ref):
    c = c_ref[...]
    a = (c * jax.nn.sigmoid(c)).astype(BF16)
    o_ref[...] = jnp.dot(a, w_ref[...].astype(BF16), preferred_element_type=F32) + b_ref[...]


def _ada(c_pad, w_ada, b_ada):
    rows, d = c_pad.shape
    n = w_ada.shape[1]
    tn = _tile(512, n)
    return pl.pallas_call(
        _ada_kernel,
        grid=(n // tn,),
        in_specs=[pl.BlockSpec((rows, d), lambda j: (0, 0)),
                  pl.BlockSpec((d, tn), lambda j: (0, j)),
                  pl.BlockSpec((1, tn), lambda j: (0, j))],
        out_specs=pl.BlockSpec((rows, tn), lambda j: (0, j)),
        out_shape=jax.ShapeDtypeStruct((rows, n), F32),
        compiler_params=_params(("parallel",), 2 * d * tn * 4 + d * tn * 2 + (4 << 20)),
        name="ada_mod",
    )(c_pad, w_ada, b_ada.reshape(1, n))


def _ln_mod_kernel(x_ref, sc_ref, sh_ref, o_ref):
    y = _ln(x_ref[...])
    o_ref[...] = (y * (1.0 + sc_ref[0]) + sh_ref[0]).astype(o_ref.dtype)


def _ln_mod(x2, mod3, t_seq, i_scale, i_shift):
    n, d = x2.shape
    tm = _tile(256, t_seq)
    per = t_seq // tm
    return pl.pallas_call(
        _ln_mod_kernel,
        grid=(n // tm,),
        in_specs=[pl.BlockSpec((tm, d), lambda i: (i, 0)),
                  pl.BlockSpec((1, 1, d), lambda i: ((i // per) * N_MOD + i_scale, 0, 0)),
                  pl.BlockSpec((1, 1, d), lambda i: ((i // per) * N_MOD + i_shift, 0, 0))],
        out_specs=pl.BlockSpec((tm, d), lambda i: (i, 0)),
        out_shape=jax.ShapeDtypeStruct((n, d), BF16),
        compiler_params=_params(("parallel",), 2 * tm * d * 6 + 4 * tm * d * 4),
        name="ln_mod",
    )(x2, mod3, mod3)


def _qk_kernel(h_ref, w_ref, g_ref, s_ref, cos_ref, sin_ref, o_ref, *, chunk):
    h = h_ref[...]
    cos = cos_ref[...]
    sin = sin_ref[...]
    for c in range(o_ref.shape[1] // chunk):
        acc = jnp.dot(h, w_ref[:, c * chunk:(c + 1) * chunk], preferred_element_type=F32)
        for hh in range(chunk // HEAD_DIM):
            sl = slice(c * chunk + hh * HEAD_DIM, c * chunk + (hh + 1) * HEAD_DIM)
            x = acc[:, hh * HEAD_DIM:(hh + 1) * HEAD_DIM]
            y = x * lax.rsqrt(jnp.mean(x * x, axis=-1, keepdims=True) + RMS_EPS) * g_ref[:, sl]
            partner = pltpu.roll(y, HALF_ROT, 1)
            o_ref[:, sl] = ((y * cos + partner * sin) * s_ref[:, sl]).astype(o_ref.dtype)


def _qk_proj(h, w_qk, gain, scale, cos_t, sin_t, t_seq):
    n, d = h.shape
    nw = w_qk.shape[1]
    tm = _tile(512, t_seq)
    per = t_seq // tm
    return pl.pallas_call(
        functools.partial(_qk_kernel, chunk=_tile(256, nw)),
        grid=(n // tm,),
        in_specs=[pl.BlockSpec((tm, d), lambda i: (i, 0)),
                  pl.BlockSpec((d, nw), lambda i: (0, 0), pipeline_mode=pl.Buffered(1)),
                  pl.BlockSpec((1, nw), lambda i: (0, 0)),
                  pl.BlockSpec((1, nw), lambda i: (0, 0)),
                  pl.BlockSpec((tm, HEAD_DIM), lambda i: (i % per, 0)),
                  pl.BlockSpec((tm, HEAD_DIM), lambda i: (i % per, 0))],
        out_specs=pl.BlockSpec((tm, nw), lambda i: (i, 0)),
        out_shape=jax.ShapeDtypeStruct((n, nw), BF16),
        compiler_params=_params(("parallel",),
                                d * nw * 2 + 2 * (tm * d * 2 + tm * nw * 2 + 2 * tm * HEAD_DIM * 4)
                                + 16 * tm * 256 * 4),
        name="qk_proj_rope",
    )(h, w_qk, gain, scale, cos_t, sin_t)


def _cast_kernel(w_ref, o_ref):
    o_ref[...] = w_ref[...].astype(o_ref.dtype)


def _cast_cols(w, col0=0, ncols=None):
    rows, width = w.shape
    ncols = width if ncols is None else ncols
    if col0 == 0 and ncols == width:
        tc = width
    else:
        tc = _tile(2048, ncols, *((col0,) if col0 else ()))
    tr = max(8, min(rows, (CAST_BLOCK_BYTES // (4 * tc)) // 8 * 8))
    while rows % tr:
        tr -= 8
    c0 = col0 // tc
    return pl.pallas_call(
        _cast_kernel,
        grid=(rows // tr, ncols // tc),
        in_specs=[pl.BlockSpec((tr, tc), lambda i, j: (i, c0 + j))],
        out_specs=pl.BlockSpec((tr, tc), lambda i, j: (i, j)),
        out_shape=jax.ShapeDtypeStruct((rows, ncols), BF16),
        compiler_params=_params(("parallel", "parallel"), 2 * tr * tc * 6 + tr * tc * 4),
        name="cast_bf16",
    )(w)


def _mm_scaled_kernel(x_ref, w_ref, s_ref, o_ref):
    acc = jnp.dot(x_ref[...], w_ref[...], preferred_element_type=F32)
    o_ref[...] = (acc * s_ref[...]).astype(o_ref.dtype)


def _mm_kernel(x_ref, w_ref, o_ref):
    o_ref[...] = jnp.dot(x_ref[...], w_ref[...], preferred_element_type=F32).astype(o_ref.dtype)


def _proj(x, w, col_scale, tn_pref=512):
    n, d = x.shape
    nw = w.shape[1]
    tm = _tile(1024, n)
    tn = _tile(tn_pref, nw)
    scaled = col_scale is not None
    in_specs = [pl.BlockSpec((tm, d), lambda i, j: (i, 0)),
                pl.BlockSpec((d, tn), lambda i, j: (0, j))]
    if scaled:
        in_specs.append(pl.BlockSpec((1, tn), lambda i, j: (0, j)))
    return pl.pallas_call(
        _mm_scaled_kernel if scaled else _mm_kernel,
        grid=(n // tm, nw // tn),
        in_specs=in_specs,
        out_specs=pl.BlockSpec((tm, tn), lambda i, j: (i, j)),
        out_shape=jax.ShapeDtypeStruct((n, nw), BF16),
        compiler_params=_params(("parallel", "arbitrary"),
                                2 * (tm * d * 2 + d * tn * 2 + tm * tn * 2) + 2 * tm * tn * 4),
        name="proj",
    )(*((x, w, col_scale) if scaled else (x, w)))


def _mm_t_kernel(wt_ref, x_ref, oa_ref, ob_ref):
    res = lax.dot_general(wt_ref[...], x_ref[...], (((1,), (1,)), ((), ())), preferred_element_type=F32)
    nb = ob_ref.shape[0]
    tm = x_ref.shape[0]
    for j in range(oa_ref.shape[0] // V_ROWS):
        oa_ref[j * V_ROWS:j * V_ROWS + HEAD_DIM, :] = res[j * HEAD_DIM:(j + 1) * HEAD_DIM, :].astype(oa_ref.dtype)
        oa_ref[j * V_ROWS + HEAD_DIM:(j + 1) * V_ROWS, :] = jnp.ones((ONES_ROWS, tm), oa_ref.dtype)
    ob_ref[...] = res[res.shape[0] - nb:, :].astype(ob_ref.dtype)


def _proj_t(x, w_t, n_a_heads):
    n, d = x.shape
    nw = w_t.shape[0]
    nb = nw - n_a_heads * HEAD_DIM
    tm = _tile(1024, n)
    return pl.pallas_call(
        _mm_t_kernel,
        grid=(n // tm,),
        in_specs=[pl.BlockSpec((nw, d), lambda i: (0, 0)),
                  pl.BlockSpec((tm, d), lambda i: (i, 0))],
        out_specs=[pl.BlockSpec((n_a_heads * V_ROWS, tm), lambda i: (0, i)),
                   pl.BlockSpec((nb, tm), lambda i: (0, i))],
        out_shape=[jax.ShapeDtypeStruct((n_a_heads * V_ROWS, n), BF16),
                   jax.ShapeDtypeStruct((nb, n), BF16)],
        compiler_params=_params(("parallel",),
                                2 * (tm * d * 2 + nw * d * 2 + 2 * nw * tm * 2) + 2 * nw * tm * 4),
        name="proj_t",
    )(w_t, x)


def _flash_kernel(q_ref, k_ref, vt_ref, *refs, tq, tk, nk, n_side):
    side_in = refs[:n_side]
    o_ref = refs[n_side]
    side_out = refs[n_side + 1:2 * n_side + 1]
    qs_ref, sa_ref, sb_ref, m_ref, acc_ref = refs[2 * n_side + 1:]
    for w_ref, wo_ref in zip(side_in, side_out):
        wo_ref[...] = w_ref[...].astype(wo_ref.dtype)
    for g in range(KV_GROUP):
        qs_ref[g * tq:(g + 1) * tq, :] = q_ref[:, g * HEAD_DIM:(g + 1) * HEAD_DIM]
    m_ref[...] = jnp.full(m_ref.shape, -jnp.inf, F32)
    acc_ref[...] = jnp.zeros(acc_ref.shape, F32)

    def scores(idx, s_ref):
        start = pl.multiple_of(idx * tk, tk)
        s_ref[...] = lax.dot_general(k_ref[pl.ds(start, tk), :], qs_ref[...], (((1,), (1,)), ((), ())),
                                     preferred_element_type=F32)

    def update(idx, s_ref):
        start = pl.multiple_of(idx * tk, tk)
        s = s_ref[...]
        m_prev = m_ref[...]
        m_new = jnp.maximum(m_prev, jnp.max(s, axis=0, keepdims=True))
        alpha = jnp.exp2(m_prev - m_new)
        p = jnp.exp2(s - m_new).astype(BF16)
        acc_ref[...] = alpha * acc_ref[...] + jnp.dot(vt_ref[:, pl.ds(start, tk)], p,
                                                      preferred_element_type=F32)
        m_ref[...] = m_new

    scores(0, sa_ref)
    pairs = (nk - 1) // 2

    def body(j, carry):
        scores(2 * j + 1, sb_ref)
        update(2 * j, sa_ref)
        scores(2 * j + 2, sa_ref)
        update(2 * j + 1, sb_ref)
        return carry

    lax.fori_loop(0, pairs, body, 0)
    if nk - 2 * pairs == 2:
        scores(nk - 1, sb_ref)
        update(nk - 2, sa_ref)
        update(nk - 1, sb_ref)
    else:
        update(nk - 1, sa_ref)
    out = (acc_ref[:HEAD_DIM, :] / acc_ref[HEAD_DIM:HEAD_DIM + 1, :]).T
    for g in range(KV_GROUP):
        o_ref[:, g * HEAD_DIM:(g + 1) * HEAD_DIM] = out[g * tq:(g + 1) * tq, :].astype(o_ref.dtype)


def _side_rows(rows, steps):
    need = -(-rows // steps)
    for br in range(-(-need // ONES_ROWS) * ONES_ROWS, rows, ONES_ROWS):
        if rows % br == 0:
            return br
    return rows


def _global_attention(qk, v_t, batch, t_seq, n_q_heads, side_casts=()):
    n = qk.shape[0]
    n_kv = n_q_heads // KV_GROUP
    tq = _tile(256, t_seq)
    tk = _tile(512, t_seq)
    nq = t_seq // tq
    nk = t_seq // tk
    gw = KV_GROUP * HEAD_DIM
    kern = functools.partial(_flash_kernel, tq=tq, tk=tk, nk=nk, n_side=len(side_casts))
    rows = KV_GROUP * tq
    steps = batch * n_kv * nq
    side_specs, side_shapes, side_bytes = [], [], 0
    for w in side_casts:
        br = _side_rows(w.shape[0], steps)
        last = pl.cdiv(w.shape[0], br) - 1
        index = functools.partial(lambda b, h, i, last: (jnp.minimum((b * n_kv + h) * nq + i, last), 0), last=last)
        side_specs.append(pl.BlockSpec((br, w.shape[1]), index))
        side_shapes.append(jax.ShapeDtypeStruct(w.shape, BF16))
        side_bytes += 2 * br * w.shape[1] * 6
    outs = pl.pallas_call(
        kern,
        grid=(batch, n_kv, nq),
        in_specs=[pl.BlockSpec((tq, gw), lambda b, h, i: (b * nq + i, h)),
                  pl.BlockSpec((t_seq, HEAD_DIM), lambda b, h, i: (b, n_q_heads + h)),
                  pl.BlockSpec((V_ROWS, t_seq), lambda b, h, i: (h, b))] + side_specs,
        out_specs=[pl.BlockSpec((tq, gw), lambda b, h, i: (b * nq + i, h))] + side_specs,
        out_shape=[jax.ShapeDtypeStruct((n, n_q_heads * HEAD_DIM), BF16)] + side_shapes,
        scratch_shapes=[pltpu.VMEM((rows, HEAD_DIM), BF16),
                        pltpu.VMEM((tk, rows), F32),
                        pltpu.VMEM((tk, rows), F32),
                        pltpu.VMEM((1, rows), F32),
                        pltpu.VMEM((V_ROWS, rows), F32)],
        compiler_params=_params(("arbitrary", "arbitrary", "arbitrary"),
                                2 * (2 * tq * gw * 2 + 2 * t_seq * HEAD_DIM * 2) + side_bytes
                                + rows * HEAD_DIM * 6 + 16 * rows * 4 + 8 * rows * tk * 4),
        name="global_attention",
    )(qk, qk, v_t, *side_casts)
    return outs[0], tuple(outs[1:])


def _window_bias(n_heads):
    key = jnp.arange(3 * Q_BLOCK, dtype=jnp.int32)[:, None] - Q_BLOCK
    qpos = jnp.arange(Q_BLOCK, dtype=jnp.int32)[None, :]
    dist = jnp.abs(qpos - key)
    slopes = 2.0 ** (-8.0 * jnp.arange(1, n_heads + 1, dtype=F32) / n_heads)
    bias = -slopes[:, None, None] * dist.astype(F32)[None] * LOG2_E
    n_kv = n_heads // KV_GROUP
    bias = bias.reshape(n_kv, KV_GROUP, 3 * Q_BLOCK, Q_BLOCK).transpose(0, 2, 1, 3)
    bias = bias.reshape(n_kv, 3 * Q_BLOCK, KV_GROUP * Q_BLOCK)
    in_window = (dist <= WINDOW)
    in_window = jnp.tile(in_window, (1, KV_GROUP))[None]
    row = jnp.arange(3 * Q_BLOCK)[None, :, None]
    not_prev = row >= Q_BLOCK
    not_next = row < 2 * Q_BLOCK
    variants = [in_window, in_window & not_prev, in_window & not_next, in_window & not_prev & not_next]
    return jnp.stack([jnp.where(v, bias, MASK_VALUE) for v in variants])


def _window_kernel(q_ref, kp_ref, kc_ref, kn_ref, vp_ref, vc_ref, vn_ref, bias_ref, sink_ref, o_ref, *,
                   n_kv):
    gw = KV_GROUP * HEAD_DIM
    for kv in range(n_kv):
        ks = slice(kv * HEAD_DIM, (kv + 1) * HEAD_DIM)
        kcat = jnp.concatenate([kp_ref[:, ks], kc_ref[:, ks], kn_ref[:, ks]], axis=0)
        vtcat = jnp.concatenate([vp_ref[ks, :], vc_ref[ks, :], vn_ref[ks, :]], axis=1)
        qs = jnp.concatenate([q_ref[:, (kv * KV_GROUP + g) * HEAD_DIM:(kv * KV_GROUP + g + 1) * HEAD_DIM]
                              for g in range(KV_GROUP)], axis=0)
        s = lax.dot_general(kcat, qs, (((1,), (1,)), ((), ())), preferred_element_type=F32)
        s = s + bias_ref[0, kv]
        sink = sink_ref[:, kv * gw:(kv + 1) * gw] * LOG2_E
        m = jnp.maximum(jnp.max(s, axis=0, keepdims=True), sink)
        p = jnp.exp2(s - m)
        den = jnp.sum(p, axis=0, keepdims=True) + jnp.exp2(sink - m)
        pv = jnp.dot(vtcat, p.astype(BF16), preferred_element_type=F32)
        out = (pv / den).T
        for g in range(KV_GROUP):
            hd = kv * KV_GROUP + g
            o_ref[:, hd * HEAD_DIM:(hd + 1) * HEAD_DIM] = out[g * Q_BLOCK:(g + 1) * Q_BLOCK, :].astype(o_ref.dtype)


def _window_attention(rest, v_t, bias, sink_row, batch, t_seq, n_heads, k_col_block, vt_row_block):
    n = rest.shape[0]
    nb = t_seq // Q_BLOCK
    n_kv = n_heads // KV_GROUP
    qw = n_heads * HEAD_DIM
    kw = n_kv * HEAD_DIM

    def prev(b, i):
        return b * nb + jnp.maximum(i - 1, 0)

    def cur(b, i):
        return b * nb + i

    def nxt(b, i):
        return b * nb + jnp.minimum(i + 1, nb - 1)

    def variant(b, i):
        return (i == 0).astype(jnp.int32) + 2 * (i == nb - 1).astype(jnp.int32)

    k_specs = [pl.BlockSpec((Q_BLOCK, kw), functools.partial(lambda b, i, f: (f(b, i), k_col_block), f=f))
               for f in (prev, cur, nxt)]
    v_specs = [pl.BlockSpec((kw, Q_BLOCK), functools.partial(lambda b, i, f: (vt_row_block, f(b, i)), f=f))
               for f in (prev, cur, nxt)]
    return pl.pallas_call(
        functools.partial(_window_kernel, n_kv=n_kv),
        grid=(batch, nb),
        in_specs=[pl.BlockSpec((Q_BLOCK, qw), lambda b, i: (b * nb + i, 0))] + k_specs + v_specs
        + [pl.BlockSpec((1,) + bias.shape[1:], lambda b, i: (variant(b, i), 0, 0, 0)),
           pl.BlockSpec((1, qw), lambda b, i: (0, 0))],
        out_specs=pl.BlockSpec((Q_BLOCK, qw), lambda b, i: (b * nb + i, 0)),
        out_shape=jax.ShapeDtypeStruct((n, qw), BF16),
        compiler_params=_params(("parallel", "arbitrary"),
                                2 * (bias[0].size * 4 + 2 * Q_BLOCK * qw * 2 + 12 * Q_BLOCK * kw)
                                + 16 * 3 * Q_BLOCK * KV_GROUP * Q_BLOCK * 4),
        name="window_attention",
    )(rest, rest, rest, rest, v_t, v_t, v_t, bias, sink_row)


def _merge_kernel(ya_ref, yb_ref, wa_ref, wb_ref, ga_ref, gb_ref, o_ref):
    a = jnp.dot(ya_ref[...], wa_ref[...], preferred_element_type=F32)
    b = jnp.dot(yb_ref[...], wb_ref[...], preferred_element_type=F32)
    ga = jax.nn.sigmoid(ga_ref[...].astype(F32))
    gb = jax.nn.sigmoid(gb_ref[...].astype(F32))
    o_ref[...] = (ga * a + gb * b).astype(o_ref.dtype)


def _merge(ya, yb, w_a, w_b, gates):
    n, ka = ya.shape
    kb = yb.shape[1]
    d = w_a.shape[1]
    tm = _tile(1024, n)
    tn = _tile(512, d)
    ga_blk = 0
    gb_blk = d // tn
    return pl.pallas_call(
        _merge_kernel,
        grid=(n // tm, d // tn),
        in_specs=[pl.BlockSpec((tm, ka), lambda i, j: (i, 0)),
                  pl.BlockSpec((tm, kb), lambda i, j: (i, 0)),
                  pl.BlockSpec((ka, tn), lambda i, j: (0, j)),
                  pl.BlockSpec((kb, tn), lambda i, j: (0, j)),
                  pl.BlockSpec((tm, tn), lambda i, j: (i, ga_blk + j)),
                  pl.BlockSpec((tm, tn), lambda i, j: (i, gb_blk + j))],
        out_specs=pl.BlockSpec((tm, tn), lambda i, j: (i, j)),
        out_shape=jax.ShapeDtypeStruct((n, d), BF16),
        compiler_params=_params(("parallel", "arbitrary"),
                                2 * (tm * (ka + kb) * 2 + (ka + kb) * tn * 2 + 3 * tm * tn * 2)
                                + 6 * tm * tn * 4),
        name="branch_merge",
    )(ya, yb, w_a, w_b, gates, gates)


def _resid_kernel(m_ref, w_ref, x_ref, g_ref, o_ref, *, alpha):
    acc = jnp.dot(m_ref[...], w_ref[...], preferred_element_type=F32)
    o_ref[...] = alpha * x_ref[...] + g_ref[0] * acc


def _out_proj(merged, w_o, x2, mod3, t_seq, i_gate, alpha):
    n, d = x2.shape
    k = merged.shape[1]
    tm = _tile(1024, t_seq)
    tn = _tile(512, d)
    per = t_seq // tm
    return pl.pallas_call(
        functools.partial(_resid_kernel, alpha=alpha),
        grid=(n // tm, d // tn),
        in_specs=[pl.BlockSpec((tm, k), lambda i, j: (i, 0)),
                  pl.BlockSpec((k, tn), lambda i, j: (0, j)),
                  pl.BlockSpec((tm, tn), lambda i, j: (i, j)),
                  pl.BlockSpec((1, 1, tn), lambda i, j: ((i // per) * N_MOD + i_gate, 0, j))],
        out_specs=pl.BlockSpec((tm, tn), lambda i, j: (i, j)),
        out_shape=jax.ShapeDtypeStruct((n, d), F32),
        compiler_params=_params(("parallel", "arbitrary"),
                                2 * (tm * k * 2 + k * tn * 2 + 2 * tm * tn * 4) + 2 * tm * tn * 4),
        name="out_proj_residual",
    )(merged, w_o, x2, mod3)


def _ln_pair_kernel(z_ref, g_ref, b_ref, sc_ref, sh_ref, x1_ref, h_ref):
    x1 = _ln(z_ref[...]) * g_ref[...] + b_ref[...]
    x1_ref[...] = x1
    h_ref[...] = (_ln(x1) * (1.0 + sc_ref[0]) + sh_ref[0]).astype(h_ref.dtype)


def _ln_pair(z, ln_g, ln_b, mod3, t_seq, i_scale, i_shift):
    n, d = z.shape
    tm = _tile(256, t_seq)
    per = t_seq // tm
    return pl.pallas_call(
        _ln_pair_kernel,
        grid=(n // tm,),
        in_specs=[pl.BlockSpec((tm, d), lambda i: (i, 0)),
                  pl.BlockSpec((1, d), lambda i: (0, 0)),
                  pl.BlockSpec((1, d), lambda i: (0, 0)),
                  pl.BlockSpec((1, 1, d), lambda i: ((i // per) * N_MOD + i_scale, 0, 0)),
                  pl.BlockSpec((1, 1, d), lambda i: ((i // per) * N_MOD + i_shift, 0, 0))],
        out_specs=[pl.BlockSpec((tm, d), lambda i: (i, 0)),
                   pl.BlockSpec((tm, d), lambda i: (i, 0))],
        out_shape=[jax.ShapeDtypeStruct((n, d), F32), jax.ShapeDtypeStruct((n, d), BF16)],
        compiler_params=_params(("parallel",), 2 * tm * d * 10 + 6 * tm * d * 4),
        name="ln1_ln2mod",
    )(z, ln_g.reshape(1, d), ln_b.reshape(1, d), mod3, mod3)


def _ffn_up_kernel(h_ref, wg_ref, wu_ref, o_ref):
    h = h_ref[...]
    g = jnp.dot(h, wg_ref[...], preferred_element_type=F32)
    u = jnp.dot(h, wu_ref[...], preferred_element_type=F32)
    o_ref[...] = (g * jax.nn.sigmoid(g) * u).astype(o_ref.dtype)


def _ffn_up(h, w_g, w_u):
    n, d = h.shape
    ff = w_g.shape[1]
    tm = _tile(1024, n)
    tn = min(512, ff)
    return pl.pallas_call(
        _ffn_up_kernel,
        grid=(n // tm, pl.cdiv(ff, tn)),
        in_specs=[pl.BlockSpec((tm, d), lambda i, j: (i, 0)),
                  pl.BlockSpec((d, tn), lambda i, j: (0, j)),
                  pl.BlockSpec((d, tn), lambda i, j: (0, j))],
        out_specs=pl.BlockSpec((tm, tn), lambda i, j: (i, j)),
        out_shape=jax.ShapeDtypeStruct((n, ff), BF16),
        compiler_params=_params(("parallel", "arbitrary"),
                                2 * (tm * d * 2 + d * 2 * tn * 2 + tm * tn * 2) + 3 * tm * 2 * tn * 4),
        name="ffn_up",
    )(h, w_g, w_u)


def _ffn_down_kernel(a_ref, w_ref, x_ref, gate_ref, g_ref, b_ref, o_ref, *, alpha, nk, last, rows):
    k = pl.program_id(1)
    tk = a_ref.shape[1]

    @pl.when(k == 0)
    def _():
        o_ref[...] = jnp.zeros(o_ref.shape, F32)

    if last == tk:
        o_ref[...] += jnp.dot(a_ref[...], w_ref[...], preferred_element_type=F32)
    else:
        @pl.when(k < nk - 1)
        def _():
            o_ref[...] += jnp.dot(a_ref[...], w_ref[...], preferred_element_type=F32)

        @pl.when(k == nk - 1)
        def _():
            o_ref[...] += jnp.dot(a_ref[:, :last], w_ref[:last, :], preferred_element_type=F32)

    @pl.when(k == nk - 1)
    def _():
        def chunk(r, carry):
            sl = pl.ds(pl.multiple_of(r * rows, rows), rows)
            z = alpha * x_ref[sl, :] + gate_ref[0] * o_ref[sl, :]
            o_ref[sl, :] = _ln(z) * g_ref[...] + b_ref[...]
            return carry

        lax.fori_loop(0, o_ref.shape[0] // rows, chunk, 0)


def _ffn_down(a, w_d, x1, ln_g, ln_b, mod3, t_seq, i_gate, alpha):
    n, ff = a.shape
    d = w_d.shape[1]
    tm = _tile(512, t_seq)
    tk = min(1024, ff)
    nk = pl.cdiv(ff, tk)
    last = ff - (nk - 1) * tk
    assert last % LANE == 0, "the partial contraction block must stay lane-aligned"
    per = t_seq // tm
    return pl.pallas_call(
        functools.partial(_ffn_down_kernel, alpha=alpha, nk=nk, last=last, rows=_tile(LANE, tm)),
        grid=(n // tm, nk),
        in_specs=[pl.BlockSpec((tm, tk), lambda i, k: (i, k)),
                  pl.BlockSpec((tk, d), lambda i, k: (k, 0)),
                  pl.BlockSpec((tm, d), lambda i, k: (i, 0)),
                  pl.BlockSpec((1, 1, d), lambda i, k: ((i // per) * N_MOD + i_gate, 0, 0)),
                  pl.BlockSpec((1, d), lambda i, k: (0, 0)),
                  pl.BlockSpec((1, d), lambda i, k: (0, 0))],
        out_specs=pl.BlockSpec((tm, d), lambda i, k: (i, 0)),
        out_shape=jax.ShapeDtypeStruct((n, d), F32),
        compiler_params=_params(("parallel", "arbitrary"),
                                2 * (tm * tk * 2 + tk * d * 2 + 2 * tm * d * 4) + 2 * tm * d * 4),
        name="ffn_down_ln",
    )(a, w_d, x1, mod3, ln_g.reshape(1, d), ln_b.reshape(1, d))


def _rope_tables(t_max):
    rows = t_max // GRID_W
    row = jnp.repeat(jnp.arange(rows, dtype=F32), GRID_W)
    col = jnp.tile(jnp.arange(GRID_W, dtype=F32), rows)
    inv = 1.0 / (ROPE_THETA ** (jnp.arange(0, HALF_ROT, 2, dtype=F32) / HALF_ROT))
    ang_r = row[:, None] * inv[None, :]
    ang_c = col[:, None] * inv[None, :]
    cr, sr, cc, sc = jnp.cos(ang_r), jnp.sin(ang_r), jnp.cos(ang_c), jnp.sin(ang_c)
    cos_t = jnp.concatenate([cr, cc, cr, cc], axis=-1)
    sin_t = jnp.concatenate([-sr, -sc, sr, sc], axis=-1)
    return cos_t, sin_t


def _rot_layout(w):
    lead = w.shape[:-1]
    w = w.reshape(lead + (-1, 2, 2, HALF_ROT // 2))
    return jnp.swapaxes(w, -3, -2).reshape(lead + (-1,))


def _prep_layer(w_in, q_norm, k_norm):
    d = w_in.shape[0]
    n_heads = d // HEAD_DIM
    ha = n_heads // 2
    qa_w = ha * HEAD_DIM
    kv_w = (ha // KV_GROUP) * HEAD_DIM
    o_ka = qa_w
    o_va = o_ka + kv_w
    o_qb = o_va + kv_w
    o_kb = o_qb + qa_w
    o_vb = o_kb + kv_w
    o_ga = o_vb + kv_w
    w_qk = _rot_layout(_cast_cols(w_in, 0, o_va))
    w_v_t = jnp.concatenate([_cast_cols(w_in, o_va, kv_w), _cast_cols(w_in, o_vb, kv_w)], axis=1).T
    w_qkb = _cast_cols(w_in, o_qb, o_vb - o_qb)
    w_gates = _cast_cols(w_in, o_ga, w_in.shape[1] - o_ga)
    gain = jnp.concatenate([jnp.tile(_rot_layout(q_norm), ha),
                            jnp.tile(_rot_layout(k_norm), ha // KV_GROUP)]).reshape(1, -1)
    q_scale = jnp.full((qa_w,), LOG2_E / math.sqrt(HEAD_DIM), F32)
    scale = jnp.concatenate([q_scale, jnp.ones((kv_w,), F32)]).reshape(1, -1)
    return dict(w_qk=w_qk, w_v_t=w_v_t, w_qkb=w_qkb, w_gates=w_gates, gain=gain, scale=scale,
                ha=ha, kv_w=kv_w, qa_w=qa_w)


def _encoder_layer(x, mod, p, late_w, sink_row, win_bias, ln1_g, ln1_b, ln2_g, ln2_b, cos_t, sin_t, alpha):
    b, t, d = x.shape
    x2 = x.reshape(b * t, d)
    mod3 = mod.reshape(b * N_MOD, 1, d)
    ha, kv_w, qa_w = p["ha"], p["kv_w"], p["qa_w"]

    h = _ln_mod(x2, mod3, t, 1, 0)
    qk = _qk_proj(h, p["w_qk"], p["gain"], p["scale"], cos_t, sin_t, t)
    va_t, vb_t = _proj_t(h, p["w_v_t"], kv_w // HEAD_DIM)
    qkb = _proj(h, p["w_qkb"], p["scale"])
    gates = _proj(h, p["w_gates"], None)
    if late_w[0].dtype == BF16:
        ya, _ = _global_attention(qk, va_t, b, t, ha)
    else:
        ya, late_w = _global_attention(qk, va_t, b, t, ha, side_casts=late_w)
    w_a, w_b, w_o, w_g, w_u, w_d = late_w
    yb = _window_attention(qkb, vb_t, win_bias, sink_row, b, t, ha, k_col_block=qa_w // kv_w,
                           vt_row_block=0)
    merged = _merge(ya, yb, w_a, w_b, gates)
    z = _out_proj(merged, w_o, x2, mod3, t, 2, alpha)
    x1, h2 = _ln_pair(z, ln1_g, ln1_b, mod3, t, 4, 3)
    a = _ffn_up(h2, w_g, w_u)
    y = _ffn_down(a, w_d, x1, ln2_g, ln2_b, mod3, t, 5, alpha)
    return y.reshape(b, t, d), late_w


def kernel(x_prompt, x_sample, c_prompt, c_sample, w_ada, b_ada, w_in, q_norm_a, k_norm_a, sink_b,
           w_br_a, w_br_b, w_o, ln1_g, ln1_b, w_ffn_gate, w_ffn_up, w_ffn_down, ln2_g, ln2_b):
    depth = w_ada.shape[0]
    alpha = float((2.0 * depth) ** 0.25)
    d = x_prompt.shape[-1]
    bp, bs = c_prompt.shape[0], c_sample.shape[0]
    rows = -(-(bp + bs) // 8) * 8
    cos_t, sin_t = _rope_tables(max(x_prompt.shape[1], x_sample.shape[1]))
    y_p, y_s = x_prompt, x_sample
    c_all = jnp.concatenate([c_prompt, c_sample, jnp.zeros((rows - bp - bs, d), F32)], axis=0)
    win_bias = _window_bias(sink_b.shape[1])
    for l in range(depth):
        mod = _ada(c_all, w_ada[l], b_ada[l]).reshape(rows, N_MOD, d)
        p = _prep_layer(w_in[l], q_norm_a[l], k_norm_a[l])
        late_w = (w_br_a[l], w_br_b[l], w_o[l], w_ffn_gate[l], w_ffn_up[l], w_ffn_down[l])
        sink_row = jnp.repeat(sink_b[l], HEAD_DIM).reshape(1, -1)
        args = (sink_row, win_bias, ln1_g[l], ln1_b[l], ln2_g[l], ln2_b[l], cos_t, sin_t, alpha)
        y_p, late_w = _encoder_layer(y_p, mod[:bp], p, late_w, *args)
        y_s, _ = _encoder_layer(y_s, mod[bp:bp + bs], p, late_w, *args)
    return (y_p, y_s)
```

```python
import functools
import math

import jax
import jax.numpy as jnp
from jax import lax
from jax.experimental import pallas as pl
from jax.experimental.pallas import tpu as pltpu

F32 = jnp.float32
BF16 = jnp.bfloat16

HEAD_DIM = 128
GRID_W = 64
Q_BLOCK = 128
WINDOW = 128
HALF_ROT = HEAD_DIM // 2
ROPE_THETA = 10000.0
N_MOD = 6
LN_EPS = 1e-5
RMS_EPS = 1e-6
KV_GROUP = 4
MASK_VALUE = -1e30
LOG2_E = 1.4426950408889634

ONES_ROWS = 16
V_ROWS = HEAD_DIM + ONES_ROWS
LANE = 128
VMEM_CAP_BYTES = 60 * 1024 * 1024
CAST_BLOCK_BYTES = 4 * 1024 * 1024


def _tile(pref, *dims):
    t = (min(pref, *dims) // LANE) * LANE
    while t >= LANE:
        if all(d % t == 0 for d in dims):
            return t
        t -= LANE
    return min(dims)


def _params(semantics, vmem_bytes):
    limit = int(min(max(vmem_bytes, 16 * 1024 * 1024), VMEM_CAP_BYTES))
    return pltpu.CompilerParams(dimension_semantics=semantics, vmem_limit_bytes=limit)


def _ln(x):
    mu = jnp.mean(x, axis=-1, keepdims=True)
    xc = x - mu
    return xc * lax.rsqrt(jnp.mean(xc * xc, axis=-1, keepdims=True) + LN_EPS)


def _ada_kernel(c_ref, w_ref, b_ref, o_ref):
    c = c_ref[...]
    a = (c * jax.nn.sigmoid(c)).astype(BF16)
    o_ref[...] = jnp.dot(a, w_ref[...].astype(BF16), preferred_element_type=F32) + b_ref[...]


def _ada(c_pad, w_ada, b_ada):
    rows, d = c_pad.shape
    n = w_ada.shape[1]
    tn = _tile(512, n)
    return pl.pallas_call(
        _ada_kernel,
        grid=(n // tn,),
        in_specs=[pl.BlockSpec((rows, d), lambda j: (0, 0)),
                  pl.BlockSpec((d, tn), lambda j: (0, j)),
                  pl.BlockSpec((1, tn), lambda j: (0, j))],
        out_specs=pl.BlockSpec((rows, tn), lambda j: (0, j)),
        out_shape=jax.ShapeDtypeStruct((rows, n), F32),
        compiler_params=_params(("parallel",), 2 * d * tn * 4 + d * tn * 2 + (4 << 20)),
        name="ada_mod",
    )(c_pad, w_ada, b_ada.reshape(1, n))


def _ln_mod_kernel(x_ref, sc_ref, sh_ref, o_ref):
    y = _ln(x_ref[...])
    o_ref[...] = (y * (1.0 + sc_ref[0]) + sh_ref[0]).astype(o_ref.dtype)


def _ln_mod(x2, mod3, t_seq, i_scale, i_shift):
    n, d = x2.shape
    tm = _tile(256, t_seq)
    per = t_seq // tm
    return pl.pallas_call(
        _ln_mod_kernel,
        grid=(n // tm,),
        in_specs=[pl.BlockSpec((tm, d), lambda i: (i, 0)),
                  pl.BlockSpec((1, 1, d), lambda i: ((i // per) * N_MOD + i_scale, 0, 0)),
                  pl.BlockSpec((1, 1, d), lambda i: ((i // per) * N_MOD + i_shift, 0, 0))],
        out_specs=pl.BlockSpec((tm, d), lambda i: (i, 0)),
        out_shape=jax.ShapeDtypeStruct((n, d), BF16),
        compiler_params=_params(("parallel",), 2 * tm * d * 6 + 4 * tm * d * 4),
        name="ln_mod",
    )(x2, mod3, mod3)


def _qk_kernel(h_ref, w_ref, g_ref, s_ref, cos_ref, sin_ref, o_ref, *, chunk):
    h = h_ref[...]
    cos = cos_ref[...]
    sin = sin_ref[...]
    for c in range(o_ref.shape[1] // chunk):
        acc = jnp.dot(h, w_ref[:, c * chunk:(c + 1) * chunk], preferred_element_type=F32)
        for hh in range(chunk // HEAD_DIM):
            sl = slice(c * chunk + hh * HEAD_DIM, c * chunk + (hh + 1) * HEAD_DIM)
            x = acc[:, hh * HEAD_DIM:(hh + 1) * HEAD_DIM]
            y = x * lax.rsqrt(jnp.mean(x * x, axis=-1, keepdims=True) + RMS_EPS) * g_ref[:, sl]
            partner = pltpu.roll(y, HALF_ROT, 1)
            o_ref[:, sl] = ((y * cos + partner * sin) * s_ref[:, sl]).astype(o_ref.dtype)


def _qk_proj(h, w_qk, gain, scale, cos_t, sin_t, t_seq):
    n, d = h.shape
    nw = w_qk.shape[1]
    tm = _tile(512, t_seq)
    per = t_seq // tm
    return pl.pallas_call(
        functools.partial(_qk_kernel, chunk=_tile(256, nw)),
        grid=(n // tm,),
        in_specs=[pl.BlockSpec((tm, d), lambda i: (i, 0)),
                  pl.BlockSpec((d, nw), lambda i: (0, 0), pipeline_mode=pl.Buffered(1)),
                  pl.BlockSpec((1, nw), lambda i: (0, 0)),
                  pl.BlockSpec((1, nw), lambda i: (0, 0)),
                  pl.BlockSpec((tm, HEAD_DIM), lambda i: (i % per, 0)),
                  pl.BlockSpec((tm, HEAD_DIM), lambda i: (i % per, 0))],
        out_specs=pl.BlockSpec((tm, nw), lambda i: (i, 0)),
        out_shape=jax.ShapeDtypeStruct((n, nw), BF16),
        compiler_params=_params(("parallel",),
                                d * nw * 2 + 2 * (tm * d * 2 + tm * nw * 2 + 2 * tm * HEAD_DIM * 4)
                                + 16 * tm * 256 * 4),
        name="qk_proj_rope",
    )(h, w_qk, gain, scale, cos_t, sin_t)


def _cast_kernel(w_ref, o_ref):
    o_ref[...] = w_ref[...].astype(o_ref.dtype)


def _cast_cols(w, col0=0, ncols=None):
    rows, width = w.shape
    ncols = width if ncols is None else ncols
    if col0 == 0 and ncols == width:
        tc = width
    else:
        tc = _tile(2048, ncols, *((col0,) if col0 else ()))
    tr = max(8, min(rows, (CAST_BLOCK_BYTES // (4 * tc)) // 8 * 8))
    while rows % tr:
        tr -= 8
    c0 = col0 // tc
    return pl.pallas_call(
        _cast_kernel,
        grid=(rows // tr, ncols // tc),
        in_specs=[pl.BlockSpec((tr, tc), lambda i, j: (i, c0 + j))],
        out_specs=pl.BlockSpec((tr, tc), lambda i, j: (i, j)),
        out_shape=jax.ShapeDtypeStruct((rows, ncols), BF16),
        compiler_params=_params(("parallel", "parallel"), 2 * tr * tc * 6 + tr * tc * 4),
        name="cast_bf16",
    )(w)


def _mm_scaled_kernel(x_ref, w_ref, s_ref, o_ref):
    acc = jnp.dot(x_ref[...], w_ref[...], preferred_element_type=F32)
    o_ref[...] = (acc * s_ref[...]).astype(o_ref.dtype)


def _mm_kernel(x_ref, w_ref, o_ref):
    o_ref[...] = jnp.dot(x_ref[...], w_ref[...], preferred_element_type=F32).astype(o_ref.dtype)


def _proj(x, w, col_scale, tn_pref=512):
    n, d = x.shape
    nw = w.shape[1]
    tm = _tile(1024, n)
    tn = _tile(tn_pref, nw)
    scaled = col_scale is not None
    in_specs = [pl.BlockSpec((tm, d), lambda i, j: (i, 0)),
                pl.BlockSpec((d, tn), lambda i, j: (0, j))]
    if scaled:
        in_specs.append(pl.BlockSpec((1, tn), lambda i, j: (0, j)))
    return pl.pallas_call(
        _mm_scaled_kernel if scaled else _mm_kernel,
        grid=(n // tm, nw // tn),
        in_specs=in_specs,
        out_specs=pl.BlockSpec((tm, tn), lambda i, j: (i, j)),
        out_shape=jax.ShapeDtypeStruct((n, nw), BF16),
        compiler_params=_params(("parallel", "arbitrary"),
                                2 * (tm * d * 2 + d * tn * 2 + tm * tn * 2) + 2 * tm * tn * 4),
        name="proj",
    )(*((x, w, col_scale) if scaled else (x, w)))


def _mm_t_kernel(wt_ref, x_ref, oa_ref, ob_ref):
    res = lax.dot_general(wt_ref[...], x_ref[...], (((1,), (1,)), ((), ())), preferred_element_type=F32)
    nb = ob_ref.shape[0]
    tm = x_ref.shape[0]
    for j in range(oa_ref.shape[0] // V_ROWS):
        oa_ref[j * V_ROWS:j * V_ROWS + HEAD_DIM, :] = res[j * HEAD_DIM:(j + 1) * HEAD_DIM, :].astype(oa_ref.dtype)
        oa_ref[j * V_ROWS + HEAD_DIM:(j + 1) * V_ROWS, :] = jnp.ones((ONES_ROWS, tm), oa_ref.dtype)
    ob_ref[...] = res[res.shape[0] - nb:, :].astype(ob_ref.dtype)


def _proj_t(x, w_t, n_a_heads):
    n, d = x.shape
    nw = w_t.shape[0]
    nb = nw - n_a_heads * HEAD_DIM
    tm = _tile(1024, n)
    return pl.pallas_call(
        _mm_t_kernel,
        grid=(n // tm,),
        in_specs=[pl.BlockSpec((nw, d), lambda i: (0, 0)),
                  pl.BlockSpec((tm, d), lambda i: (i, 0))],
        out_specs=[pl.BlockSpec((n_a_heads * V_ROWS, tm), lambda i: (0, i)),
                   pl.BlockSpec((nb, tm), lambda i: (0, i))],
        out_shape=[jax.ShapeDtypeStruct((n_a_heads * V_ROWS, n), BF16),
                   jax.ShapeDtypeStruct((nb, n), BF16)],
        compiler_params=_params(("parallel",),
                                2 * (tm * d * 2 + nw * d * 2 + 2 * nw * tm * 2) + 2 * nw * tm * 4),
        name="proj_t",
    )(w_t, x)


def _flash_kernel(q_ref, k_ref, vt_ref, *refs, tq, tk, nk, n_side):
    side_in = refs[:n_side]
    o_ref = refs[n_side]
    side_out = refs[n_side + 1:2 * n_side + 1]
    qs_ref, sa_ref, sb_ref, m_ref, acc_ref = refs[2 * n_side + 1:]
    for w_ref, wo_ref in zip(side_in, side_out):
        wo_ref[...] = w_ref[...].astype(wo_ref.dtype)
    for g in range(KV_GROUP):
        qs_ref[g * tq:(g + 1) * tq, :] = q_ref[:, g * HEAD_DIM:(g + 1) * HEAD_DIM]
    m_ref[...] = jnp.full(m_ref.shape, -jnp.inf, F32)
    acc_ref[...] = jnp.zeros(acc_ref.shape, F32)

    def scores(idx, s_ref):
        start = pl.multiple_of(idx * tk, tk)
        s_ref[...] = lax.dot_general(k_ref[pl.ds(start, tk), :], qs_ref[...], (((1,), (1,)), ((), ())),
                                     preferred_element_type=F32)

    def update(idx, s_ref):
        start = pl.multiple_of(idx * tk, tk)
        s = s_ref[...]
        m_prev = m_ref[...]
        m_new = jnp.maximum(m_prev, jnp.max(s, axis=0, keepdims=True))
        alpha = jnp.exp2(m_prev - m_new)
        p = jnp.exp2(s - m_new).astype(BF16)
        acc_ref[...] = alpha * acc_ref[...] + jnp.dot(vt_ref[:, pl.ds(start, tk)], p,
                                                      preferred_element_type=F32)
        m_ref[...] = m_new

    scores(0, sa_ref)
    pairs = (nk - 1) // 2

    def body(j, carry):
        scores(2 * j + 1, sb_ref)
        update(2 * j, sa_ref)
        scores(2 * j + 2, sa_ref)
        update(2 * j + 1, sb_ref)
        return carry

    lax.fori_loop(0, pairs, body, 0)
    if nk - 2 * pairs == 2:
        scores(nk - 1, sb_ref)
        update(nk - 2, sa_ref)
        update(nk - 1, sb_ref)
    else:
        update(nk - 1, sa_ref)
    out = (acc_ref[:HEAD_DIM, :] / acc_ref[HEAD_DIM:HEAD_DIM + 1, :]).T
    for g in range(KV_GROUP):
        o_ref[:, g * HEAD_DIM:(g + 1) * HEAD_DIM] = out[g * tq:(g + 1) * tq, :].astype(o_ref.dtype)


def _side_rows(rows, steps):
    need = -(-rows // steps)
    for br in range(-(-need // ONES_ROWS) * ONES_ROWS, rows, ONES_ROWS):
        if rows % br == 0:
            return br
    return rows


def _global_attention(qk, v_t, batch, t_seq, n_q_heads, side_casts=()):
    n = qk.shape[0]
    n_kv = n_q_heads // KV_GROUP
    tq = _tile(512, t_seq)
    tk = _tile(512, t_seq)
    nq = t_seq // tq
    nk = t_seq // tk
    gw = KV_GROUP * HEAD_DIM
    kern = functools.partial(_flash_kernel, tq=tq, tk=tk, nk=nk, n_side=len(side_casts))
    rows = KV_GROUP * tq
    steps = batch * n_kv * nq
    side_specs, side_shapes, side_bytes = [], [], 0
    for w in side_casts:
        br = _side_rows(w.shape[0], steps)
        last = pl.cdiv(w.shape[0], br) - 1
        index = functools.partial(lambda b, h, i, last: (jnp.minimum((b * n_kv + h) * nq + i, last), 0), last=last)
        side_specs.append(pl.BlockSpec((br, w.shape[1]), index))
        side_shapes.append(jax.ShapeDtypeStruct(w.shape, BF16))
        side_bytes += 2 * br * w.shape[1] * 6
    outs = pl.pallas_call(
        kern,
        grid=(batch, n_kv, nq),
        in_specs=[pl.BlockSpec((tq, gw), lambda b, h, i: (b * nq + i, h)),
                  pl.BlockSpec((t_seq, HEAD_DIM), lambda b, h, i: (b, n_q_heads + h)),
                  pl.BlockSpec((V_ROWS, t_seq), lambda b, h, i: (h, b))] + side_specs,
        out_specs=[pl.BlockSpec((tq, gw), lambda b, h, i: (b * nq + i, h))] + side_specs,
        out_shape=[jax.ShapeDtypeStruct((n, n_q_heads * HEAD_DIM), BF16)] + side_shapes,
        scratch_shapes=[pltpu.VMEM((rows, HEAD_DIM), BF16),
                        pltpu.VMEM((tk, rows), F32),
                        pltpu.VMEM((tk, rows), F32),
                        pltpu.VMEM((1, rows), F32),
                        pltpu.VMEM((V_ROWS, rows), F32)],
        compiler_params=_params(("arbitrary", "arbitrary", "arbitrary"),
                                2 * (2 * tq * gw * 2 + 2 * t_seq * HEAD_DIM * 2) + side_bytes
                                + rows * HEAD_DIM * 6 + 16 * rows * 4 + 8 * rows * tk * 4),
        name="global_attention",
    )(qk, qk, v_t, *side_casts)
    return outs[0], tuple(outs[1:])


def _window_bias(n_heads):
    key = jnp.arange(3 * Q_BLOCK, dtype=jnp.int32)[:, None] - Q_BLOCK
    qpos = jnp.arange(Q_BLOCK, dtype=jnp.int32)[None, :]
    dist = jnp.abs(qpos - key)
    slopes = 2.0 ** (-8.0 * jnp.arange(1, n_heads + 1, dtype=F32) / n_heads)
    bias = -slopes[:, None, None] * dist.astype(F32)[None] * LOG2_E
    n_kv = n_heads // KV_GROUP
    bias = bias.reshape(n_kv, KV_GROUP, 3 * Q_BLOCK, Q_BLOCK).transpose(0, 2, 1, 3)
    bias = bias.reshape(n_kv, 3 * Q_BLOCK, KV_GROUP * Q_BLOCK)
    in_window = (dist <= WINDOW)
    in_window = jnp.tile(in_window, (1, KV_GROUP))[None]
    row = jnp.arange(3 * Q_BLOCK)[None, :, None]
    not_prev = row >= Q_BLOCK
    not_next = row < 2 * Q_BLOCK
    variants = [in_window, in_window & not_prev, in_window & not_next, in_window & not_prev & not_next]
    return jnp.stack([jnp.where(v, bias, MASK_VALUE) for v in variants])


def _window_kernel(q_ref, kp_ref, kc_ref, kn_ref, vp_ref, vc_ref, vn_ref, bias_ref, sink_ref, o_ref, *,
                   n_kv):
    gw = KV_GROUP * HEAD_DIM
    for kv in range(n_kv):
        ks = slice(kv * HEAD_DIM, (kv + 1) * HEAD_DIM)
        kcat = jnp.concatenate([kp_ref[:, ks], kc_ref[:, ks], kn_ref[:, ks]], axis=0)
        vtcat = jnp.concatenate([vp_ref[ks, :], vc_ref[ks, :], vn_ref[ks, :]], axis=1)
        qs = jnp.concatenate([q_ref[:, (kv * KV_GROUP + g) * HEAD_DIM:(kv * KV_GROUP + g + 1) * HEAD_DIM]
                              for g in range(KV_GROUP)], axis=0)
        s = lax.dot_general(kcat, qs, (((1,), (1,)), ((), ())), preferred_element_type=F32)
        s = s + bias_ref[0, kv]
        sink = sink_ref[:, kv * gw:(kv + 1) * gw] * LOG2_E
        m = jnp.maximum(jnp.max(s, axis=0, keepdims=True), sink)
        p = jnp.exp2(s - m)
        den = jnp.sum(p, axis=0, keepdims=True) + jnp.exp2(sink - m)
        pv = jnp.dot(vtcat, p.astype(BF16), preferred_element_type=F32)
        out = (pv / den).T
        for g in range(KV_GROUP):
            hd = kv * KV_GROUP + g
            o_ref[:, hd * HEAD_DIM:(hd + 1) * HEAD_DIM] = out[g * Q_BLOCK:(g + 1) * Q_BLOCK, :].astype(o_ref.dtype)


def _window_attention(rest, v_t, bias, sink_row, batch, t_seq, n_heads, k_col_block, vt_row_block):
    n = rest.shape[0]
    nb = t_seq // Q_BLOCK
    n_kv = n_heads // KV_GROUP
    qw = n_heads * HEAD_DIM
    kw = n_kv * HEAD_DIM

    def prev(b, i):
        return b * nb + jnp.maximum(i - 1, 0)

    def cur(b, i):
        return b * nb + i

    def nxt(b, i):
        return b * nb + jnp.minimum(i + 1, nb - 1)

    def variant(b, i):
        return (i == 0).astype(jnp.int32) + 2 * (i == nb - 1).astype(jnp.int32)

    k_specs = [pl.BlockSpec((Q_BLOCK, kw), functools.partial(lambda b, i, f: (f(b, i), k_col_block), f=f))
               for f in (prev, cur, nxt)]
    v_specs = [pl.BlockSpec((kw, Q_BLOCK), functools.partial(lambda b, i, f: (vt_row_block, f(b, i)), f=f))
               for f in (prev, cur, nxt)]
    return pl.pallas_call(
        functools.partial(_window_kernel, n_kv=n_kv),
        grid=(batch, nb),
        in_specs=[pl.BlockSpec((Q_BLOCK, qw), lambda b, i: (b * nb + i, 0))] + k_specs + v_specs
        + [pl.BlockSpec((1,) + bias.shape[1:], lambda b, i: (variant(b, i), 0, 0, 0)),
           pl.BlockSpec((1, qw), lambda b, i: (0, 0))],
        out_specs=pl.BlockSpec((Q_BLOCK, qw), lambda b, i: (b * nb + i, 0)),
        out_shape=jax.ShapeDtypeStruct((n, qw), BF16),
        compiler_params=_params(("parallel", "arbitrary"),
                                2 * (bias[0].size * 4 + 2 * Q_BLOCK * qw * 2 + 12 * Q_BLOCK * kw)
                                + 16 * 3 * Q_BLOCK * KV_GROUP * Q_BLOCK * 4),
        name="window_attention",
    )(rest, rest, rest, rest, v_t, v_t, v_t, bias, sink_row)


def _merge_kernel(ya_ref, yb_ref, wa_ref, wb_ref, ga_ref, gb_ref, o_ref):
    a = jnp.dot(ya_ref[...], wa_ref[...], preferred_element_type=F32)
    b = jnp.dot(yb_ref[...], wb_ref[...], preferred_element_type=F32)
    ga = jax.nn.sigmoid(ga_ref[...].astype(F32))
    gb = jax.nn.sigmoid(gb_ref[...].astype(F32))
    o_ref[...] = (ga * a + gb * b).astype(o_ref.dtype)


def _merge(ya, yb, w_a, w_b, gates):
    n, ka = ya.shape
    kb = yb.shape[1]
    d = w_a.shape[1]
    tm = _tile(1024, n)
    tn = _tile(512, d)
    ga_blk = 0
    gb_blk = d // tn
    return pl.pallas_call(
        _merge_kernel,
        grid=(n // tm, d // tn),
        in_specs=[pl.BlockSpec((tm, ka), lambda i, j: (i, 0)),
                  pl.BlockSpec((tm, kb), lambda i, j: (i, 0)),
                  pl.BlockSpec((ka, tn), lambda i, j: (0, j)),
                  pl.BlockSpec((kb, tn), lambda i, j: (0, j)),
                  pl.BlockSpec((tm, tn), lambda i, j: (i, ga_blk + j)),
                  pl.BlockSpec((tm, tn), lambda i, j: (i, gb_blk + j))],
        out_specs=pl.BlockSpec((tm, tn), lambda i, j: (i, j)),
        out_shape=jax.ShapeDtypeStruct((n, d), BF16),
        compiler_params=_params(("parallel", "arbitrary"),
                                2 * (tm * (ka + kb) * 2 + (ka + kb) * tn * 2 + 3 * tm * tn * 2)
                                + 6 * tm * tn * 4),
        name="branch_merge",
    )(ya, yb, w_a, w_b, gates, gates)


def _resid_kernel(m_ref, w_ref, x_ref, g_ref, o_ref, *, alpha):
    acc = jnp.dot(m_ref[...], w_ref[...], preferred_element_type=F32)
    o_ref[...] = alpha * x_ref[...] + g_ref[0] * acc


def _out_proj(merged, w_o, x2, mod3, t_seq, i_gate, alpha):
    n, d = x2.shape
    k = merged.shape[1]
    tm = _tile(1024, t_seq)
    tn = _tile(512, d)
    per = t_seq // tm
    return pl.pallas_call(
        functools.partial(_resid_kernel, alpha=alpha),
        grid=(n // tm, d // tn),
        in_specs=[pl.BlockSpec((tm, k), lambda i, j: (i, 0)),
                  pl.BlockSpec((k, tn), lambda i, j: (0, j)),
                  pl.BlockSpec((tm, tn), lambda i, j: (i, j)),
                  pl.BlockSpec((1, 1, tn), lambda i, j: ((i // per) * N_MOD + i_gate, 0, j))],
        out_specs=pl.BlockSpec((tm, tn), lambda i, j: (i, j)),
        out_shape=jax.ShapeDtypeStruct((n, d), F32),
        compiler_params=_params(("parallel", "arbitrary"),
                                2 * (tm * k * 2 + k * tn * 2 + 2 * tm * tn * 4) + 2 * tm * tn * 4),
        name="out_proj_residual",
    )(merged, w_o, x2, mod3)


def _ln_pair_kernel(z_ref, g_ref, b_ref, sc_ref, sh_ref, x1_ref, h_ref):
    x1 = _ln(z_ref[...]) * g_ref[...] + b_ref[...]
    x1_ref[...] = x1
    h_ref[...] = (_ln(x1) * (1.0 + sc_ref[0]) + sh_ref[0]).astype(h_ref.dtype)


def _ln_pair(z, ln_g, ln_b, mod3, t_seq, i_scale, i_shift):
    n, d = z.shape
    tm = _tile(256, t_seq)
    per = t_seq // tm
    return pl.pallas_call(
        _ln_pair_kernel,
        grid=(n // tm,),
        in_specs=[pl.BlockSpec((tm, d), lambda i: (i, 0)),
                  pl.BlockSpec((1, d), lambda i: (0, 0)),
                  pl.BlockSpec((1, d), lambda i: (0, 0)),
                  pl.BlockSpec((1, 1, d), lambda i: ((i // per) * N_MOD + i_scale, 0, 0)),
                  pl.BlockSpec((1, 1, d), lambda i: ((i // per) * N_MOD + i_shift, 0, 0))],
        out_specs=[pl.BlockSpec((tm, d), lambda i: (i, 0)),
                   pl.BlockSpec((tm, d), lambda i: (i, 0))],
        out_shape=[jax.ShapeDtypeStruct((n, d), F32), jax.ShapeDtypeStruct((n, d), BF16)],
        compiler_params=_params(("parallel",), 2 * tm * d * 10 + 6 * tm * d * 4),
        name="ln1_ln2mod",
    )(z, ln_g.reshape(1, d), ln_b.reshape(1, d), mod3, mod3)


def _ffn_up_kernel(h_ref, wg_ref, wu_ref, o_ref):
    h = h_ref[...]
    g = jnp.dot(h, wg_ref[...], preferred_element_type=F32)
    u = jnp.dot(h, wu_ref[...], preferred_element_type=F32)
    o_ref[...] = (g * jax.nn.sigmoid(g) * u).astype(o_ref.dtype)


def _ffn_up(h, w_g, w_u):
    n, d = h.shape
    ff = w_g.shape[1]
    tm = _tile(1024, n)
    tn = min(512, ff)
    return pl.pallas_call(
        _ffn_up_kernel,
        grid=(n // tm, pl.cdiv(ff, tn)),
        in_specs=[pl.BlockSpec((tm, d), lambda i, j: (i, 0)),
                  pl.BlockSpec((d, tn), lambda i, j: (0, j)),
                  pl.BlockSpec((d, tn), lambda i, j: (0, j))],
        out_specs=pl.BlockSpec((tm, tn), lambda i, j: (i, j)),
        out_shape=jax.ShapeDtypeStruct((n, ff), BF16),
        compiler_params=_params(("parallel", "arbitrary"),
                                2 * (tm * d * 2 + d * 2 * tn * 2 + tm * tn * 2) + 3 * tm * 2 * tn * 4),
        name="ffn_up",
    )(h, w_g, w_u)


def _ffn_down_kernel(a_ref, w_ref, x_ref, gate_ref, g_ref, b_ref, o_ref, *, alpha, nk, last, rows):
    k = pl.program_id(1)
    tk = a_ref.shape[1]

    @pl.when(k == 0)
    def _():
        o_ref[...] = jnp.zeros(o_ref.shape, F32)

    if last == tk:
        o_ref[...] += jnp.dot(a_ref[...], w_ref[...], preferred_element_type=F32)
    else:
        @pl.when(k < nk - 1)
        def _():
            o_ref[...] += jnp.dot(a_ref[...], w_ref[...], preferred_element_type=F32)

        @pl.when(k == nk - 1)
        def _():
            o_ref[...] += jnp.dot(a_ref[:, :last], w_ref[:last, :], preferred_element_type=F32)

    @pl.when(k == nk - 1)
    def _():
        def chunk(r, carry):
            sl = pl.ds(pl.multiple_of(r * rows, rows), rows)
            z = alpha * x_ref[sl, :] + gate_ref[0] * o_ref[sl, :]
            o_ref[sl, :] = _ln(z) * g_ref[...] + b_ref[...]
            return carry

        lax.fori_loop(0, o_ref.shape[0] // rows, chunk, 0)


def _ffn_down(a, w_d, x1, ln_g, ln_b, mod3, t_seq, i_gate, alpha):
    n, ff = a.shape
    d = w_d.shape[1]
    tm = _tile(512, t_seq)
    tk = min(1024, ff)
    nk = pl.cdiv(ff, tk)
    last = ff - (nk - 1) * tk
    assert last % LANE == 0, "the partial contraction block must stay lane-aligned"
    per = t_seq // tm
    return pl.pallas_call(
        functools.partial(_ffn_down_kernel, alpha=alpha, nk=nk, last=last, rows=_tile(LANE, tm)),
        grid=(n // tm, nk),
        in_specs=[pl.BlockSpec((tm, tk), lambda i, k: (i, k)),
                  pl.BlockSpec((tk, d), lambda i, k: (k, 0)),
                  pl.BlockSpec((tm, d), lambda i, k: (i, 0)),
                  pl.BlockSpec((1, 1, d), lambda i, k: ((i // per) * N_MOD + i_gate, 0, 0)),
                  pl.BlockSpec((1, d), lambda i, k: (0, 0)),
                  pl.BlockSpec((1, d), lambda i, k: (0, 0))],
        out_specs=pl.BlockSpec((tm, d), lambda i, k: (i, 0)),
        out_shape=jax.ShapeDtypeStruct((n, d), F32),
        compiler_params=_params(("parallel", "arbitrary"),
                                2 * (tm * tk * 2 + tk * d * 2 + 2 * tm * d * 4) + 2 * tm * d * 4),
        name="ffn_down_ln",
    )(a, w_d, x1, mod3, ln_g.reshape(1, d), ln_b.reshape(1, d))


def _rope_tables(t_max):
    rows = t_max // GRID_W
    row = jnp.repeat(jnp.arange(rows, dtype=F32), GRID_W)
    col = jnp.tile(jnp.arange(GRID_W, dtype=F32), rows)
    inv = 1.0 / (ROPE_THETA ** (jnp.arange(0, HALF_ROT, 2, dtype=F32) / HALF_ROT))
    ang_r = row[:, None] * inv[None, :]
    ang_c = col[:, None] * inv[None, :]
    cr, sr, cc, sc = jnp.cos(ang_r), jnp.sin(ang_r), jnp.cos(ang_c), jnp.sin(ang_c)
    cos_t = jnp.concatenate([cr, cc, cr, cc], axis=-1)
    sin_t = jnp.concatenate([-sr, -sc, sr, sc], axis=-1)
    return cos_t, sin_t


def _rot_layout(w):
    lead = w.shape[:-1]
    w = w.reshape(lead + (-1, 2, 2, HALF_ROT // 2))
    return jnp.swapaxes(w, -3, -2).reshape(lead + (-1,))


def _prep_layer(w_in, q_norm, k_norm):
    d = w_in.shape[0]
    n_heads = d // HEAD_DIM
    ha = n_heads // 2
    qa_w = ha * HEAD_DIM
    kv_w = (ha // KV_GROUP) * HEAD_DIM
    o_ka = qa_w
    o_va = o_ka + kv_w
    o_qb = o_va + kv_w
    o_kb = o_qb + qa_w
    o_vb = o_kb + kv_w
    o_ga = o_vb + kv_w
    w_qk = _rot_layout(_cast_cols(w_in, 0, o_va))
    w_v_t = jnp.concatenate([_cast_cols(w_in, o_va, kv_w), _cast_cols(w_in, o_vb, kv_w)], axis=1).T
    w_qkb = _cast_cols(w_in, o_qb, o_vb - o_qb)
    w_gates = _cast_cols(w_in, o_ga, w_in.shape[1] - o_ga)
    gain = jnp.concatenate([jnp.tile(_rot_layout(q_norm), ha),
                            jnp.tile(_rot_layout(k_norm), ha // KV_GROUP)]).reshape(1, -1)
    q_scale = jnp.full((qa_w,), LOG2_E / math.sqrt(HEAD_DIM), F32)
    scale = jnp.concatenate([q_scale, jnp.ones((kv_w,), F32)]).reshape(1, -1)
    return dict(w_qk=w_qk, w_v_t=w_v_t, w_qkb=w_qkb, w_gates=w_gates, gain=gain, scale=scale,
                ha=ha, kv_w=kv_w, qa_w=qa_w)


def _encoder_layer(x, mod, p, late_w, sink_row, win_bias, ln1_g, ln1_b, ln2_g, ln2_b, cos_t, sin_t, alpha):
    b, t, d = x.shape
    x2 = x.reshape(b * t, d)
    mod3 = mod.reshape(b * N_MOD, 1, d)
    ha, kv_w, qa_w = p["ha"], p["kv_w"], p["qa_w"]

    h = _ln_mod(x2, mod3, t, 1, 0)
    qk = _qk_proj(h, p["w_qk"], p["gain"], p["scale"], cos_t, sin_t, t)
    va_t, vb_t = _proj_t(h, p["w_v_t"], kv_w // HEAD_DIM)
    qkb = _proj(h, p["w_qkb"], p["scale"])
    gates = _proj(h, p["w_gates"], None, tn_pref=1024)
    if late_w[0].dtype == BF16:
        ya, _ = _global_attention(qk, va_t, b, t, ha)
    else:
        ya, late_w = _global_attention(qk, va_t, b, t, ha, side_casts=late_w)
    w_a, w_b, w_o, w_g, w_u, w_d = late_w
    yb = _window_attention(qkb, vb_t, win_bias, sink_row, b, t, ha, k_col_block=qa_w // kv_w,
                           vt_row_block=0)
    merged = _merge(ya, yb, w_a, w_b, gates)
    z = _out_proj(merged, w_o, x2, mod3, t, 2, alpha)
    x1, h2 = _ln_pair(z, ln1_g, ln1_b, mod3, t, 4, 3)
    a = _ffn_up(h2, w_g, w_u)
    y = _ffn_down(a, w_d, x1, ln2_g, ln2_b, mod3, t, 5, alpha)
    return y.reshape(b, t, d), late_w


def kernel(x_prompt, x_sample, c_prompt, c_sample, w_ada, b_ada, w_in, q_norm_a, k_norm_a, sink_b,
           w_br_a, w_br_b, w_o, ln1_g, ln1_b, w_ffn_gate, w_ffn_up, w_ffn_down, ln2_g, ln2_b):
    depth = w_ada.shape[0]
    alpha = float((2.0 * depth) ** 0.25)
    d = x_prompt.shape[-1]
    bp, bs = c_prompt.shape[0], c_sample.shape[0]
    rows = -(-(bp + bs) // 8) * 8
    cos_t, sin_t = _rope_tables(max(x_prompt.shape[1], x_sample.shape[1]))
    y_p, y_s = x_prompt, x_sample
    c_all = jnp.concatenate([c_prompt, c_sample, jnp.zeros((rows - bp - bs, d), F32)], axis=0)
    win_bias = _window_bias(sink_b.shape[1])
    for l in range(depth):
        mod = _ada(c_all, w_ada[l], b_ada[l]).reshape(rows, N_MOD, d)
        p = _prep_layer(w_in[l], q_norm_a[l], k_norm_a[l])
        late_w = (w_br_a[l], w_br_b[l], w_o[l], w_ffn_gate[l], w_ffn_up[l], w_ffn_down[l])
        sink_row = jnp.repeat(sink_b[l], HEAD_DIM).reshape(1, -1)
        args = (sink_row, win_bias, ln1_g[l], ln1_b[l], ln2_g[l], ln2_b[l], cos_t, sin_t, alpha)
        y_p, late_w = _encoder_layer(y_p, mod[:bp], p, late_w, *args)
        y_s, _ = _encoder_layer(y_s, mod[bp:bp + bs], p, late_w, *args)
    return (y_p, y_s)
```

```python
import functools
import math

import jax
import jax.numpy as jnp
from jax import lax
from jax.experimental import pallas as pl
from jax.experimental.pallas import tpu as pltpu

F32 = jnp.float32
BF16 = jnp.bfloat16

HEAD_DIM = 128
GRID_W = 64
Q_BLOCK = 128
WINDOW = 128
HALF_ROT = HEAD_DIM // 2
ROPE_THETA = 10000.0
N_MOD = 6
LN_EPS = 1e-5
RMS_EPS = 1e-6
KV_GROUP = 4
MASK_VALUE = -1e30
LOG2_E = 1.4426950408889634

ONES_ROWS = 16
V_ROWS = HEAD_DIM + ONES_ROWS
LANE = 128
VMEM_CAP_BYTES = 60 * 1024 * 1024
CAST_BLOCK_BYTES = 4 * 1024 * 1024


def _tile(pref, *dims):
    t = (min(pref, *dims) // LANE) * LANE
    while t >= LANE:
        if all(d % t == 0 for d in dims):
            return t
        t -= LANE
    return min(dims)


def _params(semantics, vmem_bytes):
    limit = int(min(max(vmem_bytes, 16 * 1024 * 1024), VMEM_CAP_BYTES))
    return pltpu.CompilerParams(dimension_semantics=semantics, vmem_limit_bytes=limit)


def _ln(x):
    mu = jnp.mean(x, axis=-1, keepdims=True)
    xc = x - mu
    return xc * lax.rsqrt(jnp.mean(xc * xc, axis=-1, keepdims=True) + LN_EPS)


def _ada_kernel(c_ref, w_ref, b_ref, o_ref):
    c = c_ref[...]
    a = (c * jax.nn.sigmoid(c)).astype(BF16)
    o_ref[...] = jnp.dot(a, w_ref[...].astype(BF16), preferred_element_type=F32) + b_ref[...]


def _ada(c_pad, w_ada, b_ada):
    rows, d = c_pad.shape
    n = w_ada.shape[1]
    tn = _tile(512, n)
    return pl.pallas_call(
        _ada_kernel,
        grid=(n // tn,),
        in_specs=[pl.BlockSpec((rows, d), lambda j: (0, 0)),
                  pl.BlockSpec((d, tn), lambda j: (0, j)),
                  pl.BlockSpec((1, tn), lambda j: (0, j))],
        out_specs=pl.BlockSpec((rows, tn), lambda j: (0, j)),
        out_shape=jax.ShapeDtypeStruct((rows, n), F32),
        compiler_params=_params(("parallel",), 2 * d * tn * 4 + d * tn * 2 + (4 << 20)),
        name="ada_mod",
    )(c_pad, w_ada, b_ada.reshape(1, n))


def _ln_mod_kernel(x_ref, sc_ref, sh_ref, o_ref):
    y = _ln(x_ref[...])
    o_ref[...] = (y * (1.0 + sc_ref[0]) + sh_ref[0]).astype(o_ref.dtype)


def _ln_mod(x2, mod3, t_seq, i_scale, i_shift):
    n, d = x2.shape
    tm = _tile(512, t_seq)
    per = t_seq // tm
    return pl.pallas_call(
        _ln_mod_kernel,
        grid=(n // tm,),
        in_specs=[pl.BlockSpec((tm, d), lambda i: (i, 0)),
                  pl.BlockSpec((1, 1, d), lambda i: ((i // per) * N_MOD + i_scale, 0, 0)),
                  pl.BlockSpec((1, 1, d), lambda i: ((i // per) * N_MOD + i_shift, 0, 0))],
        out_specs=pl.BlockSpec((tm, d), lambda i: (i, 0)),
        out_shape=jax.ShapeDtypeStruct((n, d), BF16),
        compiler_params=_params(("parallel",), 2 * tm * d * 6 + 4 * tm * d * 4),
        name="ln_mod",
    )(x2, mod3, mod3)


def _qk_kernel(h_ref, w_ref, g_ref, s_ref, cos_ref, sin_ref, o_ref, *, chunk):
    h = h_ref[...]
    cos = cos_ref[...]
    sin = sin_ref[...]
    for c in range(o_ref.shape[1] // chunk):
        acc = jnp.dot(h, w_ref[:, c * chunk:(c + 1) * chunk], preferred_element_type=F32)
        for hh in range(chunk // HEAD_DIM):
            sl = slice(c * chunk + hh * HEAD_DIM, c * chunk + (hh + 1) * HEAD_DIM)
            x = acc[:, hh * HEAD_DIM:(hh + 1) * HEAD_DIM]
            y = x * lax.rsqrt(jnp.mean(x * x, axis=-1, keepdims=True) + RMS_EPS) * g_ref[:, sl]
            partner = pltpu.roll(y, HALF_ROT, 1)
            o_ref[:, sl] = ((y * cos + partner * sin) * s_ref[:, sl]).astype(o_ref.dtype)


def _qk_proj(h, w_qk, gain, scale, cos_t, sin_t, t_seq):
    n, d = h.shape
    nw = w_qk.shape[1]
    tm = _tile(512, t_seq)
    per = t_seq // tm
    return pl.pallas_call(
        functools.partial(_qk_kernel, chunk=_tile(256, nw)),
        grid=(n // tm,),
        in_specs=[pl.BlockSpec((tm, d), lambda i: (i, 0)),
                  pl.BlockSpec((d, nw), lambda i: (0, 0), pipeline_mode=pl.Buffered(1)),
                  pl.BlockSpec((1, nw), lambda i: (0, 0)),
                  pl.BlockSpec((1, nw), lambda i: (0, 0)),
                  pl.BlockSpec((tm, HEAD_DIM), lambda i: (i % per, 0)),
                  pl.BlockSpec((tm, HEAD_DIM), lambda i: (i % per, 0))],
        out_specs=pl.BlockSpec((tm, nw), lambda i: (i, 0)),
        out_shape=jax.ShapeDtypeStruct((n, nw), BF16),
        compiler_params=_params(("parallel",),
                                d * nw * 2 + 2 * (tm * d * 2 + tm * nw * 2 + 2 * tm * HEAD_DIM * 4)
                                + 16 * tm * 256 * 4),
        name="qk_proj_rope",
    )(h, w_qk, gain, scale, cos_t, sin_t)


def _cast_kernel(w_ref, o_ref):
    o_ref[...] = w_ref[...].astype(o_ref.dtype)


def _cast_cols(w, col0=0, ncols=None):
    rows, width = w.shape
    ncols = width if ncols is None else ncols
    if col0 == 0 and ncols == width:
        tc = width
    else:
        tc = _tile(2048, ncols, *((col0,) if col0 else ()))
    tr = max(8, min(rows, (CAST_BLOCK_BYTES // (4 * tc)) // 8 * 8))
    while rows % tr:
        tr -= 8
    c0 = col0 // tc
    return pl.pallas_call(
        _cast_kernel,
        grid=(rows // tr, ncols // tc),
        in_specs=[pl.BlockSpec((tr, tc), lambda i, j: (i, c0 + j))],
        out_specs=pl.BlockSpec((tr, tc), lambda i, j: (i, j)),
        out_shape=jax.ShapeDtypeStruct((rows, ncols), BF16),
        compiler_params=_params(("parallel", "parallel"), 2 * tr * tc * 6 + tr * tc * 4),
        name="cast_bf16",
    )(w)


def _mm_scaled_kernel(x_ref, w_ref, s_ref, o_ref):
    acc = jnp.dot(x_ref[...], w_ref[...], preferred_element_type=F32)
    o_ref[...] = (acc * s_ref[...]).astype(o_ref.dtype)


def _mm_kernel(x_ref, w_ref, o_ref):
    o_ref[...] = jnp.dot(x_ref[...], w_ref[...], preferred_element_type=F32).astype(o_ref.dtype)


def _proj(x, w, col_scale, tn_pref=512):
    n, d = x.shape
    nw = w.shape[1]
    tm = _tile(1024, n)
    tn = _tile(tn_pref, nw)
    scaled = col_scale is not None
    in_specs = [pl.BlockSpec((tm, d), lambda i, j: (i, 0)),
                pl.BlockSpec((d, tn), lambda i, j: (0, j))]
    if scaled:
        in_specs.append(pl.BlockSpec((1, tn), lambda i, j: (0, j)))
    return pl.pallas_call(
        _mm_scaled_kernel if scaled else _mm_kernel,
        grid=(n // tm, nw // tn),
        in_specs=in_specs,
        out_specs=pl.BlockSpec((tm, tn), lambda i, j: (i, j)),
        out_shape=jax.ShapeDtypeStruct((n, nw), BF16),
        compiler_params=_params(("parallel", "arbitrary"),
                                2 * (tm * d * 2 + d * tn * 2 + tm * tn * 2) + 2 * tm * tn * 4),
        name="proj",
    )(*((x, w, col_scale) if scaled else (x, w)))


def _mm_t_kernel(wt_ref, x_ref, oa_ref, ob_ref):
    res = lax.dot_general(wt_ref[...], x_ref[...], (((1,), (1,)), ((), ())), preferred_element_type=F32)
    nb = ob_ref.shape[0]
    tm = x_ref.shape[0]
    for j in range(oa_ref.shape[0] // V_ROWS):
        oa_ref[j * V_ROWS:j * V_ROWS + HEAD_DIM, :] = res[j * HEAD_DIM:(j + 1) * HEAD_DIM, :].astype(oa_ref.dtype)
        oa_ref[j * V_ROWS + HEAD_DIM:(j + 1) * V_ROWS, :] = jnp.ones((ONES_ROWS, tm), oa_ref.dtype)
    ob_ref[...] = res[res.shape[0] - nb:, :].astype(ob_ref.dtype)


def _proj_t(x, w_t, n_a_heads):
    n, d = x.shape
    nw = w_t.shape[0]
    nb = nw - n_a_heads * HEAD_DIM
    tm = _tile(1024, n)
    return pl.pallas_call(
        _mm_t_kernel,
        grid=(n // tm,),
        in_specs=[pl.BlockSpec((nw, d), lambda i: (0, 0)),
                  pl.BlockSpec((tm, d), lambda i: (i, 0))],
        out_specs=[pl.BlockSpec((n_a_heads * V_ROWS, tm), lambda i: (0, i)),
                   pl.BlockSpec((nb, tm), lambda i: (0, i))],
        out_shape=[jax.ShapeDtypeStruct((n_a_heads * V_ROWS, n), BF16),
                   jax.ShapeDtypeStruct((nb, n), BF16)],
        compiler_params=_params(("parallel",),
                                2 * (tm * d * 2 + nw * d * 2 + 2 * nw * tm * 2) + 2 * nw * tm * 4),
        name="proj_t",
    )(w_t, x)


def _flash_kernel(q_ref, k_ref, vt_ref, *refs, tq, tk, nk, n_side):
    side_in = refs[:n_side]
    o_ref = refs[n_side]
    side_out = refs[n_side + 1:2 * n_side + 1]
    qs_ref, sa_ref, sb_ref, m_ref, acc_ref = refs[2 * n_side + 1:]
    for w_ref, wo_ref in zip(side_in, side_out):
        wo_ref[...] = w_ref[...].astype(wo_ref.dtype)
    for g in range(KV_GROUP):
        qs_ref[g * tq:(g + 1) * tq, :] = q_ref[:, g * HEAD_DIM:(g + 1) * HEAD_DIM]
    m_ref[...] = jnp.full(m_ref.shape, -jnp.inf, F32)
    acc_ref[...] = jnp.zeros(acc_ref.shape, F32)

    def scores(idx, s_ref):
        start = pl.multiple_of(idx * tk, tk)
        s_ref[...] = lax.dot_general(k_ref[pl.ds(start, tk), :], qs_ref[...], (((1,), (1,)), ((), ())),
                                     preferred_element_type=F32)

    def update(idx, s_ref):
        start = pl.multiple_of(idx * tk, tk)
        s = s_ref[...]
        m_prev = m_ref[...]
        m_new = jnp.maximum(m_prev, jnp.max(s, axis=0, keepdims=True))
        alpha = jnp.exp2(m_prev - m_new)
        p = jnp.exp2(s - m_new).astype(BF16)
        acc_ref[...] = alpha * acc_ref[...] + jnp.dot(vt_ref[:, pl.ds(start, tk)], p,
                                                      preferred_element_type=F32)
        m_ref[...] = m_new

    scores(0, sa_ref)
    pairs = (nk - 1) // 2

    def body(j, carry):
        scores(2 * j + 1, sb_ref)
        update(2 * j, sa_ref)
        scores(2 * j + 2, sa_ref)
        update(2 * j + 1, sb_ref)
        return carry

    lax.fori_loop(0, pairs, body, 0)
    if nk - 2 * pairs == 2:
        scores(nk - 1, sb_ref)
        update(nk - 2, sa_ref)
        update(nk - 1, sb_ref)
    else:
        update(nk - 1, sa_ref)
    out = (acc_ref[:HEAD_DIM, :] / acc_ref[HEAD_DIM:HEAD_DIM + 1, :]).T
    for g in range(KV_GROUP):
        o_ref[:, g * HEAD_DIM:(g + 1) * HEAD_DIM] = out[g * tq:(g + 1) * tq, :].astype(o_ref.dtype)


def _side_rows(rows, steps):
    need = -(-rows // steps)
    for br in range(-(-need // ONES_ROWS) * ONES_ROWS, rows, ONES_ROWS):
        if rows % br == 0:
            return br
    return rows


def _global_attention(qk, v_t, batch, t_seq, n_q_heads, side_casts=()):
    n = qk.shape[0]
    n_kv = n_q_heads // KV_GROUP
    tq = _tile(512 if side_casts else 1024, t_seq)
    tk = _tile(512, t_seq)
    nq = t_seq // tq
    nk = t_seq // tk
    gw = KV_GROUP * HEAD_DIM
    kern = functools.partial(_flash_kernel, tq=tq, tk=tk, nk=nk, n_side=len(side_casts))
    rows = KV_GROUP * tq
    steps = batch * n_kv * nq
    side_specs, side_shapes, side_bytes = [], [], 0
    for w in side_casts:
        br = _side_rows(w.shape[0], steps)
        last = pl.cdiv(w.shape[0], br) - 1
        index = functools.partial(lambda b, h, i, last: (jnp.minimum((b * n_kv + h) * nq + i, last), 0), last=last)
        side_specs.append(pl.BlockSpec((br, w.shape[1]), index))
        side_shapes.append(jax.ShapeDtypeStruct(w.shape, BF16))
        side_bytes += 2 * br * w.shape[1] * 6
    outs = pl.pallas_call(
        kern,
        grid=(batch, n_kv, nq),
        in_specs=[pl.BlockSpec((tq, gw), lambda b, h, i: (b * nq + i, h)),
                  pl.BlockSpec((t_seq, HEAD_DIM), lambda b, h, i: (b, n_q_heads + h)),
                  pl.BlockSpec((V_ROWS, t_seq), lambda b, h, i: (h, b))] + side_specs,
        out_specs=[pl.BlockSpec((tq, gw), lambda b, h, i: (b * nq + i, h))] + side_specs,
        out_shape=[jax.ShapeDtypeStruct((n, n_q_heads * HEAD_DIM), BF16)] + side_shapes,
        scratch_shapes=[pltpu.VMEM((rows, HEAD_DIM), BF16),
                        pltpu.VMEM((tk, rows), F32),
                        pltpu.VMEM((tk, rows), F32),
                        pltpu.VMEM((1, rows), F32),
                        pltpu.VMEM((V_ROWS, rows), F32)],
        compiler_params=_params(("arbitrary", "arbitrary", "arbitrary"),
                                2 * (2 * tq * gw * 2 + 2 * t_seq * HEAD_DIM * 2) + side_bytes
                                + rows * HEAD_DIM * 6 + 16 * rows * 4 + 8 * rows * tk * 4),
        name="global_attention",
    )(qk, qk, v_t, *side_casts)
    return outs[0], tuple(outs[1:])


def _window_bias(n_heads):
    key = jnp.arange(3 * Q_BLOCK, dtype=jnp.int32)[:, None] - Q_BLOCK
    qpos = jnp.arange(Q_BLOCK, dtype=jnp.int32)[None, :]
    dist = jnp.abs(qpos - key)
    slopes = 2.0 ** (-8.0 * jnp.arange(1, n_heads + 1, dtype=F32) / n_heads)
    bias = -slopes[:, None, None] * dist.astype(F32)[None] * LOG2_E
    n_kv = n_heads // KV_GROUP
    bias = bias.reshape(n_kv, KV_GROUP, 3 * Q_BLOCK, Q_BLOCK).transpose(0, 2, 1, 3)
    bias = bias.reshape(n_kv, 3 * Q_BLOCK, KV_GROUP * Q_BLOCK)
    in_window = (dist <= WINDOW)
    in_window = jnp.tile(in_window, (1, KV_GROUP))[None]
    row = jnp.arange(3 * Q_BLOCK)[None, :, None]
    not_prev = row >= Q_BLOCK
    not_next = row < 2 * Q_BLOCK
    variants = [in_window, in_window & not_prev, in_window & not_next, in_window & not_prev & not_next]
    return jnp.stack([jnp.where(v, bias, MASK_VALUE) for v in variants])


def _window_kernel(q_ref, kp_ref, kc_ref, kn_ref, vp_ref, vc_ref, vn_ref, bias_ref, sink_ref, o_ref, *,
                   n_kv):
    gw = KV_GROUP * HEAD_DIM
    for kv in range(n_kv):
        ks = slice(kv * HEAD_DIM, (kv + 1) * HEAD_DIM)
        kcat = jnp.concatenate([kp_ref[:, ks], kc_ref[:, ks], kn_ref[:, ks]], axis=0)
        vtcat = jnp.concatenate([vp_ref[ks, :], vc_ref[ks, :], vn_ref[ks, :]], axis=1)
        qs = jnp.concatenate([q_ref[:, (kv * KV_GROUP + g) * HEAD_DIM:(kv * KV_GROUP + g + 1) * HEAD_DIM]
                              for g in range(KV_GROUP)], axis=0)
        s = lax.dot_general(kcat, qs, (((1,), (1,)), ((), ())), preferred_element_type=F32)
        s = s + bias_ref[0, kv]
        sink = sink_ref[:, kv * gw:(kv + 1) * gw] * LOG2_E
        m = jnp.maximum(jnp.max(s, axis=0, keepdims=True), sink)
        p = jnp.exp2(s - m)
        den = jnp.sum(p, axis=0, keepdims=True) + jnp.exp2(sink - m)
        pv = jnp.dot(vtcat, p.astype(BF16), preferred_element_type=F32)
        out = (pv / den).T
        for g in range(KV_GROUP):
            hd = kv * KV_GROUP + g
            o_ref[:, hd * HEAD_DIM:(hd + 1) * HEAD_DIM] = out[g * Q_BLOCK:(g + 1) * Q_BLOCK, :].astype(o_ref.dtype)


def _window_attention(rest, v_t, bias, sink_row, batch, t_seq, n_heads, k_col_block, vt_row_block):
    n = rest.shape[0]
    nb = t_seq // Q_BLOCK
    n_kv = n_heads // KV_GROUP
    qw = n_heads * HEAD_DIM
    kw = n_kv * HEAD_DIM

    def prev(b, i):
        return b * nb + jnp.maximum(i - 1, 0)

    def cur(b, i):
        return b * nb + i

    def nxt(b, i):
        return b * nb + jnp.minimum(i + 1, nb - 1)

    def variant(b, i):
        return (i == 0).astype(jnp.int32) + 2 * (i == nb - 1).astype(jnp.int32)

    k_specs = [pl.BlockSpec((Q_BLOCK, kw), functools.partial(lambda b, i, f: (f(b, i), k_col_block), f=f))
               for f in (prev, cur, nxt)]
    v_specs = [pl.BlockSpec((kw, Q_BLOCK), functools.partial(lambda b, i, f: (vt_row_block, f(b, i)), f=f))
               for f in (prev, cur, nxt)]
    return pl.pallas_call(
        functools.partial(_window_kernel, n_kv=n_kv),
        grid=(batch, nb),
        in_specs=[pl.BlockSpec((Q_BLOCK, qw), lambda b, i: (b * nb + i, 0))] + k_specs + v_specs
        + [pl.BlockSpec((1,) + bias.shape[1:], lambda b, i: (variant(b, i), 0, 0, 0)),
           pl.BlockSpec((1, qw), lambda b, i: (0, 0))],
        out_specs=pl.BlockSpec((Q_BLOCK, qw), lambda b, i: (b * nb + i, 0)),
        out_shape=jax.ShapeDtypeStruct((n, qw), BF16),
        compiler_params=_params(("parallel", "arbitrary"),
                                2 * (bias[0].size * 4 + 2 * Q_BLOCK * qw * 2 + 12 * Q_BLOCK * kw)
                                + 16 * 3 * Q_BLOCK * KV_GROUP * Q_BLOCK * 4),
        name="window_attention",
    )(rest, rest, rest, rest, v_t, v_t, v_t, bias, sink_row)


def _merge_kernel(ya_ref, yb_ref, wa_ref, wb_ref, ga_ref, gb_ref, o_ref):
    a = jnp.dot(ya_ref[...], wa_ref[...], preferred_element_type=F32)
    b = jnp.dot(yb_ref[...], wb_ref[...], preferred_element_type=F32)
    ga = jax.nn.sigmoid(ga_ref[...].astype(F32))
    gb = jax.nn.sigmoid(gb_ref[...].astype(F32))
    o_ref[...] = (ga * a + gb * b).astype(o_ref.dtype)


def _merge(ya, yb, w_a, w_b, gates):
    n, ka = ya.shape
    kb = yb.shape[1]
    d = w_a.shape[1]
    tm = _tile(1024, n)
    tn = _tile(512, d)
    ga_blk = 0
    gb_blk = d // tn
    return pl.pallas_call(
        _merge_kernel,
        grid=(n // tm, d // tn),
        in_specs=[pl.BlockSpec((tm, ka), lambda i, j: (i, 0)),
                  pl.BlockSpec((tm, kb), lambda i, j: (i, 0)),
                  pl.BlockSpec((ka, tn), lambda i, j: (0, j)),
                  pl.BlockSpec((kb, tn), lambda i, j: (0, j)),
                  pl.BlockSpec((tm, tn), lambda i, j: (i, ga_blk + j)),
                  pl.BlockSpec((tm, tn), lambda i, j: (i, gb_blk + j))],
        out_specs=pl.BlockSpec((tm, tn), lambda i, j: (i, j)),
        out_shape=jax.ShapeDtypeStruct((n, d), BF16),
        compiler_params=_params(("parallel", "arbitrary"),
                                2 * (tm * (ka + kb) * 2 + (ka + kb) * tn * 2 + 3 * tm * tn * 2)
                                + 6 * tm * tn * 4),
        name="branch_merge",
    )(ya, yb, w_a, w_b, gates, gates)


def _resid_kernel(m_ref, w_ref, x_ref, g_ref, o_ref, *, alpha):
    acc = jnp.dot(m_ref[...], w_ref[...], preferred_element_type=F32)
    o_ref[...] = alpha * x_ref[...] + g_ref[0] * acc


def _out_proj(merged, w_o, x2, mod3, t_seq, i_gate, alpha):
    n, d = x2.shape
    k = merged.shape[1]
    tm = _tile(1024, t_seq)
    tn = _tile(512, d)
    per = t_seq // tm
    return pl.pallas_call(
        functools.partial(_resid_kernel, alpha=alpha),
        grid=(n // tm, d // tn),
        in_specs=[pl.BlockSpec((tm, k), lambda i, j: (i, 0)),
                  pl.BlockSpec((k, tn), lambda i, j: (0, j)),
                  pl.BlockSpec((tm, tn), lambda i, j: (i, j)),
                  pl.BlockSpec((1, 1, tn), lambda i, j: ((i // per) * N_MOD + i_gate, 0, j))],
        out_specs=pl.BlockSpec((tm, tn), lambda i, j: (i, j)),
        out_shape=jax.ShapeDtypeStruct((n, d), F32),
        compiler_params=_params(("parallel", "arbitrary"),
                                2 * (tm * k * 2 + k * tn * 2 + 2 * tm * tn * 4) + 2 * tm * tn * 4),
        name="out_proj_residual",
    )(merged, w_o, x2, mod3)


def _ln_pair_kernel(z_ref, g_ref, b_ref, sc_ref, sh_ref, x1_ref, h_ref):
    x1 = _ln(z_ref[...]) * g_ref[...] + b_ref[...]
    x1_ref[...] = x1
    h_ref[...] = (_ln(x1) * (1.0 + sc_ref[0]) + sh_ref[0]).astype(h_ref.dtype)


def _ln_pair(z, ln_g, ln_b, mod3, t_seq, i_scale, i_shift):
    n, d = z.shape
    tm = _tile(256, t_seq)
    per = t_seq // tm
    return pl.pallas_call(
        _ln_pair_kernel,
        grid=(n // tm,),
        in_specs=[pl.BlockSpec((tm, d), lambda i: (i, 0)),
                  pl.BlockSpec((1, d), lambda i: (0, 0)),
                  pl.BlockSpec((1, d), lambda i: (0, 0)),
                  pl.BlockSpec((1, 1, d), lambda i: ((i // per) * N_MOD + i_scale, 0, 0)),
                  pl.BlockSpec((1, 1, d), lambda i: ((i // per) * N_MOD + i_shift, 0, 0))],
        out_specs=[pl.BlockSpec((tm, d), lambda i: (i, 0)),
                   pl.BlockSpec((tm, d), lambda i: (i, 0))],
        out_shape=[jax.ShapeDtypeStruct((n, d), F32), jax.ShapeDtypeStruct((n, d), BF16)],
        compiler_params=_params(("parallel",), 2 * tm * d * 10 + 6 * tm * d * 4),
        name="ln1_ln2mod",
    )(z, ln_g.reshape(1, d), ln_b.reshape(1, d), mod3, mod3)


def _ffn_up_kernel(h_ref, wg_ref, wu_ref, o_ref):
    h = h_ref[...]
    g = jnp.dot(h, wg_ref[...], preferred_element_type=F32)
    u = jnp.dot(h, wu_ref[...], preferred_element_type=F32)
    o_ref[...] = (g * jax.nn.sigmoid(g) * u).astype(o_ref.dtype)


def _ffn_up(h, w_g, w_u):
    n, d = h.shape
    ff = w_g.shape[1]
    tm = _tile(1024, n)
    tn = min(512, ff)
    return pl.pallas_call(
        _ffn_up_kernel,
        grid=(n // tm, pl.cdiv(ff, tn)),
        in_specs=[pl.BlockSpec((tm, d), lambda i, j: (i, 0)),
                  pl.BlockSpec((d, tn), lambda i, j: (0, j)),
                  pl.BlockSpec((d, tn), lambda i, j: (0, j))],
        out_specs=pl.BlockSpec((tm, tn), lambda i, j: (i, j)),
        out_shape=jax.ShapeDtypeStruct((n, ff), BF16),
        compiler_params=_params(("parallel", "arbitrary"),
                                2 * (tm * d * 2 + d * 2 * tn * 2 + tm * tn * 2) + 3 * tm * 2 * tn * 4),
        name="ffn_up",
    )(h, w_g, w_u)


def _ffn_down_kernel(a_ref, w_ref, x_ref, gate_ref, g_ref, b_ref, o_ref, *, alpha, nk, last, rows):
    k = pl.program_id(1)
    tk = a_ref.shape[1]

    @pl.when(k == 0)
    def _():
        o_ref[...] = jnp.zeros(o_ref.shape, F32)

    if last == tk:
        o_ref[...] += jnp.dot(a_ref[...], w_ref[...], preferred_element_type=F32)
    else:
        @pl.when(k < nk - 1)
        def _():
            o_ref[...] += jnp.dot(a_ref[...], w_ref[...], preferred_element_type=F32)

        @pl.when(k == nk - 1)
        def _():
            o_ref[...] += jnp.dot(a_ref[:, :last], w_ref[:last, :], preferred_element_type=F32)

    @pl.when(k == nk - 1)
    def _():
        def chunk(r, carry):
            sl = pl.ds(pl.multiple_of(r * rows, rows), rows)
            z = alpha * x_ref[sl, :] + gate_ref[0] * o_ref[sl, :]
            o_ref[sl, :] = _ln(z) * g_ref[...] + b_ref[...]
            return carry

        lax.fori_loop(0, o_ref.shape[0] // rows, chunk, 0)


def _ffn_down(a, w_d, x1, ln_g, ln_b, mod3, t_seq, i_gate, alpha):
    n, ff = a.shape
    d = w_d.shape[1]
    tm = _tile(512, t_seq)
    tk = min(1024, ff)
    nk = pl.cdiv(ff, tk)
    last = ff - (nk - 1) * tk
    assert last % LANE == 0, "the partial contraction block must stay lane-aligned"
    per = t_seq // tm
    return pl.pallas_call(
        functools.partial(_ffn_down_kernel, alpha=alpha, nk=nk, last=last, rows=_tile(LANE, tm)),
        grid=(n // tm, nk),
        in_specs=[pl.BlockSpec((tm, tk), lambda i, k: (i, k)),
                  pl.BlockSpec((tk, d), lambda i, k: (k, 0)),
                  pl.BlockSpec((tm, d), lambda i, k: (i, 0)),
                  pl.BlockSpec((1, 1, d), lambda i, k: ((i // per) * N_MOD + i_gate, 0, 0)),
                  pl.BlockSpec((1, d), lambda i, k: (0, 0)),
                  pl.BlockSpec((1, d), lambda i, k: (0, 0))],
        out_specs=pl.BlockSpec((tm, d), lambda i, k: (i, 0)),
        out_shape=jax.ShapeDtypeStruct((n, d), F32),
        compiler_params=_params(("parallel", "arbitrary"),
                                2 * (tm * tk * 2 + tk * d * 2 + 2 * tm * d * 4) + 2 * tm * d * 4),
        name="ffn_down_ln",
    )(a, w_d, x1, mod3, ln_g.reshape(1, d), ln_b.reshape(1, d))


def _rope_tables(t_max):
    rows = t_max // GRID_W
    row = jnp.repeat(jnp.arange(rows, dtype=F32), GRID_W)
    col = jnp.tile(jnp.arange(GRID_W, dtype=F32), rows)
    inv = 1.0 / (ROPE_THETA ** (jnp.arange(0, HALF_ROT, 2, dtype=F32) / HALF_ROT))
    ang_r = row[:, None] * inv[None, :]
    ang_c = col[:, None] * inv[None, :]
    cr, sr, cc, sc = jnp.cos(ang_r), jnp.sin(ang_r), jnp.cos(ang_c), jnp.sin(ang_c)
    cos_t = jnp.concatenate([cr, cc, cr, cc], axis=-1)
    sin_t = jnp.concatenate([-sr, -sc, sr, sc], axis=-1)
    return cos_t, sin_t


def _rot_layout(w):
    lead = w.shape[:-1]
    w = w.reshape(lead + (-1, 2, 2, HALF_ROT // 2))
    return jnp.swapaxes(w, -3, -2).reshape(lead + (-1,))


def _prep_layer(w_in, q_norm, k_norm):
    d = w_in.shape[0]
    n_heads = d // HEAD_DIM
    ha = n_heads // 2
    qa_w = ha * HEAD_DIM
    kv_w = (ha // KV_GROUP) * HEAD_DIM
    o_ka = qa_w
    o_va = o_ka + kv_w
    o_qb = o_va + kv_w
    o_kb = o_qb + qa_w
    o_vb = o_kb + kv_w
    o_ga = o_vb + kv_w
    w_qk = _rot_layout(_cast_cols(w_in, 0, o_va))
    w_v_t = jnp.concatenate([_cast_cols(w_in, o_va, kv_w), _cast_cols(w_in, o_vb, kv_w)], axis=1).T
    w_qkb = _cast_cols(w_in, o_qb, o_vb - o_qb)
    w_gates = _cast_cols(w_in, o_ga, w_in.shape[1] - o_ga)
    gain = jnp.concatenate([jnp.tile(_rot_layout(q_norm), ha),
                            jnp.tile(_rot_layout(k_norm), ha // KV_GROUP)]).reshape(1, -1)
    q_scale = jnp.full((qa_w,), LOG2_E / math.sqrt(HEAD_DIM), F32)
    scale = jnp.concatenate([q_scale, jnp.ones((kv_w,), F32)]).reshape(1, -1)
    return dict(w_qk=w_qk, w_v_t=w_v_t, w_qkb=w_qkb, w_gates=w_gates, gain=gain, scale=scale,
                ha=ha, kv_w=kv_w, qa_w=qa_w)


def _encoder_layer(x, mod, p, late_w, sink_row, win_bias, ln1_g, ln1_b, ln2_g, ln2_b, cos_t, sin_t, alpha):
    b, t, d = x.shape
    x2 = x.reshape(b * t, d)
    mod3 = mod.reshape(b * N_MOD, 1, d)
    ha, kv_w, qa_w = p["ha"], p["kv_w"], p["qa_w"]

    h = _ln_mod(x2, mod3, t, 1, 0)
    qk = _qk_proj(h, p["w_qk"], p["gain"], p["scale"], cos_t, sin_t, t)
    va_t, vb_t = _proj_t(h, p["w_v_t"], kv_w // HEAD_DIM)
    qkb = _proj(h, p["w_qkb"], p["scale"])
    gates = _proj(h, p["w_gates"], None, tn_pref=1024)
    if late_w[0].dtype == BF16:
        ya, _ = _global_attention(qk, va_t, b, t, ha)
    else:
        ya, late_w = _global_attention(qk, va_t, b, t, ha, side_casts=late_w)
    w_a, w_b, w_o, w_g, w_u, w_d = late_w
    yb = _window_attention(qkb, vb_t, win_bias, sink_row, b, t, ha, k_col_block=qa_w // kv_w,
                           vt_row_block=0)
    merged = _merge(ya, yb, w_a, w_b, gates)
    z = _out_proj(merged, w_o, x2, mod3, t, 2, alpha)
    x1, h2 = _ln_pair(z, ln1_g, ln1_b, mod3, t, 4, 3)
    a = _ffn_up(h2, w_g, w_u)
    y = _ffn_down(a, w_d, x1, ln2_g, ln2_b, mod3, t, 5, alpha)
    return y.reshape(b, t, d), late_w


def kernel(x_prompt, x_sample, c_prompt, c_sample, w_ada, b_ada, w_in, q_norm_a, k_norm_a, sink_b,
           w_br_a, w_br_b, w_o, ln1_g, ln1_b, w_ffn_gate, w_ffn_up, w_ffn_down, ln2_g, ln2_b):
    depth = w_ada.shape[0]
    alpha = float((2.0 * depth) ** 0.25)
    d = x_prompt.shape[-1]
    bp, bs = c_prompt.shape[0], c_sample.shape[0]
    rows = -(-(bp + bs) // 8) * 8
    cos_t, sin_t = _rope_tables(max(x_prompt.shape[1], x_sample.shape[1]))
    y_p, y_s = x_prompt, x_sample
    c_all = jnp.concatenate([c_prompt, c_sample, jnp.zeros((rows - bp - bs, d), F32)], axis=0)
    win_bias = _window_bias(sink_b.shape[1])
    for l in range(depth):
        mod = _ada(c_all, w_ada[l], b_ada[l]).reshape(rows, N_MOD, d)
        p = _prep_layer(w_in[l], q_norm_a[l], k_norm_a[l])
        late_w = (w_br_a[l], w_br_b[l], w_o[l], w_ffn_gate[l], w_ffn_up[l], w_ffn_down[l])
        sink_row = jnp.repeat(sink_b[l], HEAD_DIM).reshape(1, -1)
        args = (sink_row, win_bias, ln1_g[l], ln1_b[l], ln2_g[l], ln2_b[l], cos_t, sin_t, alpha)
        y_p, late_w = _encoder_layer(y_p, mod[:bp], p, late_w, *args)
        y_s, _ = _encoder_layer(y_s, mod[bp:bp + bs], p, late_w, *args)
    return (y_p, y_s)
```

```python
import functools
import math

import jax
import jax.numpy as jnp
from jax import lax
from jax.experimental import pallas as pl
from jax.experimental.pallas import tpu as pltpu

F32 = jnp.float32
BF16 = jnp.bfloat16

HEAD_DIM = 128
GRID_W = 64
Q_BLOCK = 128
WINDOW = 128
HALF_ROT = HEAD_DIM // 2
ROPE_THETA = 10000.0
N_MOD = 6
LN_EPS = 1e-5
RMS_EPS = 1e-6
KV_GROUP = 4
MASK_VALUE = -1e30
LOG2_E = 1.4426950408889634

ONES_ROWS = 16
V_ROWS = HEAD_DIM + ONES_ROWS
LANE = 128
VMEM_CAP_BYTES = 60 * 1024 * 1024
CAST_BLOCK_BYTES = 4 * 1024 * 1024


def _tile(pref, *dims):
    t = (min(pref, *dims) // LANE) * LANE
    while t >= LANE:
        if all(d % t == 0 for d in dims):
            return t
        t -= LANE
    return min(dims)


def _params(semantics, vmem_bytes):
    limit = int(min(max(vmem_bytes, 16 * 1024 * 1024), VMEM_CAP_BYTES))
    return pltpu.CompilerParams(dimension_semantics=semantics, vmem_limit_bytes=limit)


def _ln(x):
    mu = jnp.mean(x, axis=-1, keepdims=True)
    xc = x - mu
    return xc * lax.rsqrt(jnp.mean(xc * xc, axis=-1, keepdims=True) + LN_EPS)


def _ada_kernel(c_ref, w_ref, b_ref, o_ref):
    c = c_ref[...]
    a = (c * jax.nn.sigmoid(c)).astype(BF16)
    o_ref[...] = jnp.dot(a, w_ref[...].astype(BF16), preferred_element_type=F32) + b_ref[...]


def _ada(c_pad, w_ada, b_ada):
    rows, d = c_pad.shape
    n = w_ada.shape[1]
    tn = _tile(512, n)
    return pl.pallas_call(
        _ada_kernel,
        grid=(n // tn,),
        in_specs=[pl.BlockSpec((rows, d), lambda j: (0, 0)),
                  pl.BlockSpec((d, tn), lambda j: (0, j)),
                  pl.BlockSpec((1, tn), lambda j: (0, j))],
        out_specs=pl.BlockSpec((rows, tn), lambda j: (0, j)),
        out_shape=jax.ShapeDtypeStruct((rows, n), F32),
        compiler_params=_params(("parallel",), 2 * d * tn * 4 + d * tn * 2 + (4 << 20)),
        name="ada_mod",
    )(c_pad, w_ada, b_ada.reshape(1, n))


def _ln_mod_kernel(x_ref, sc_ref, sh_ref, o_ref):
    y = _ln(x_ref[...])
    o_ref[...] = (y * (1.0 + sc_ref[0]) + sh_ref[0]).astype(o_ref.dtype)


def _ln_mod(x2, mod3, t_seq, i_scale, i_shift):
    n, d = x2.shape
    tm = _tile(512, t_seq)
    per = t_seq // tm
    return pl.pallas_call(
        _ln_mod_kernel,
        grid=(n // tm,),
        in_specs=[pl.BlockSpec((tm, d), lambda i: (i, 0)),
                  pl.BlockSpec((1, 1, d), lambda i: ((i // per) * N_MOD + i_scale, 0, 0)),
                  pl.BlockSpec((1, 1, d), lambda i: ((i // per) * N_MOD + i_shift, 0, 0))],
        out_specs=pl.BlockSpec((tm, d), lambda i: (i, 0)),
        out_shape=jax.ShapeDtypeStruct((n, d), BF16),
        compiler_params=_params(("parallel",), 2 * tm * d * 6 + 4 * tm * d * 4),
        name="ln_mod",
    )(x2, mod3, mod3)


def _qk_kernel(h_ref, w_ref, g_ref, s_ref, cos_ref, sin_ref, o_ref, *, chunk):
    h = h_ref[...]
    cos = cos_ref[...]
    sin = sin_ref[...]
    for c in range(o_ref.shape[1] // chunk):
        acc = jnp.dot(h, w_ref[:, c * chunk:(c + 1) * chunk], preferred_element_type=F32)
        for hh in range(chunk // HEAD_DIM):
            sl = slice(c * chunk + hh * HEAD_DIM, c * chunk + (hh + 1) * HEAD_DIM)
            x = acc[:, hh * HEAD_DIM:(hh + 1) * HEAD_DIM]
            y = x * lax.rsqrt(jnp.mean(x * x, axis=-1, keepdims=True) + RMS_EPS) * g_ref[:, sl]
            partner = pltpu.roll(y, HALF_ROT, 1)
            o_ref[:, sl] = ((y * cos + partner * sin) * s_ref[:, sl]).astype(o_ref.dtype)


def _qk_proj(h, w_qk, gain, scale, cos_t, sin_t, t_seq):
    n, d = h.shape
    nw = w_qk.shape[1]
    tm = _tile(512, t_seq)
    per = t_seq // tm
    return pl.pallas_call(
        functools.partial(_qk_kernel, chunk=_tile(256, nw)),
        grid=(n // tm,),
        in_specs=[pl.BlockSpec((tm, d), lambda i: (i, 0)),
                  pl.BlockSpec((d, nw), lambda i: (0, 0), pipeline_mode=pl.Buffered(1)),
                  pl.BlockSpec((1, nw), lambda i: (0, 0)),
                  pl.BlockSpec((1, nw), lambda i: (0, 0)),
                  pl.BlockSpec((tm, HEAD_DIM), lambda i: (i % per, 0)),
                  pl.BlockSpec((tm, HEAD_DIM), lambda i: (i % per, 0))],
        out_specs=pl.BlockSpec((tm, nw), lambda i: (i, 0)),
        out_shape=jax.ShapeDtypeStruct((n, nw), BF16),
        compiler_params=_params(("parallel",),
                                d * nw * 2 + 2 * (tm * d * 2 + tm * nw * 2 + 2 * tm * HEAD_DIM * 4)
                                + 16 * tm * 256 * 4),
        name="qk_proj_rope",
    )(h, w_qk, gain, scale, cos_t, sin_t)


def _cast_kernel(w_ref, o_ref):
    o_ref[...] = w_ref[...].astype(o_ref.dtype)


def _cast_cols(w, col0=0, ncols=None):
    rows, width = w.shape
    ncols = width if ncols is None else ncols
    if col0 == 0 and ncols == width:
        tc = width
    else:
        tc = _tile(2048, ncols, *((col0,) if col0 else ()))
    tr = max(8, min(rows, (CAST_BLOCK_BYTES // (4 * tc)) // 8 * 8))
    while rows % tr:
        tr -= 8
    c0 = col0 // tc
    return pl.pallas_call(
        _cast_kernel,
        grid=(rows // tr, ncols // tc),
        in_specs=[pl.BlockSpec((tr, tc), lambda i, j: (i, c0 + j))],
        out_specs=pl.BlockSpec((tr, tc), lambda i, j: (i, j)),
        out_shape=jax.ShapeDtypeStruct((rows, ncols), BF16),
        compiler_params=_params(("parallel", "parallel"), 2 * tr * tc * 6 + tr * tc * 4),
        name="cast_bf16",
    )(w)


def _mm_scaled_kernel(x_ref, w_ref, s_ref, o_ref):
    acc = jnp.dot(x_ref[...], w_ref[...], preferred_element_type=F32)
    o_ref[...] = (acc * s_ref[...]).astype(o_ref.dtype)


def _mm_kernel(x_ref, w_ref, o_ref):
    o_ref[...] = jnp.dot(x_ref[...], w_ref[...], preferred_element_type=F32).astype(o_ref.dtype)


def _proj(x, w, col_scale, tn_pref=512):
    n, d = x.shape
    nw = w.shape[1]
    tm = _tile(1024, n)
    tn = _tile(tn_pref, nw)
    scaled = col_scale is not None
    in_specs = [pl.BlockSpec((tm, d), lambda i, j: (i, 0)),
                pl.BlockSpec((d, tn), lambda i, j: (0, j))]
    if scaled:
        in_specs.append(pl.BlockSpec((1, tn), lambda i, j: (0, j)))
    return pl.pallas_call(
        _mm_scaled_kernel if scaled else _mm_kernel,
        grid=(n // tm, nw // tn),
        in_specs=in_specs,
        out_specs=pl.BlockSpec((tm, tn), lambda i, j: (i, j)),
        out_shape=jax.ShapeDtypeStruct((n, nw), BF16),
        compiler_params=_params(("parallel", "arbitrary"),
                                2 * (tm * d * 2 + d * tn * 2 + tm * tn * 2) + 2 * tm * tn * 4),
        name="proj",
    )(*((x, w, col_scale) if scaled else (x, w)))


def _mm_t_kernel(wt_ref, x_ref, oa_ref, ob_ref):
    res = lax.dot_general(wt_ref[...], x_ref[...], (((1,), (1,)), ((), ())), preferred_element_type=F32)
    nb = ob_ref.shape[0]
    tm = x_ref.shape[0]
    for j in range(oa_ref.shape[0] // V_ROWS):
        oa_ref[j * V_ROWS:j * V_ROWS + HEAD_DIM, :] = res[j * HEAD_DIM:(j + 1) * HEAD_DIM, :].astype(oa_ref.dtype)
        oa_ref[j * V_ROWS + HEAD_DIM:(j + 1) * V_ROWS, :] = jnp.ones((ONES_ROWS, tm), oa_ref.dtype)
    ob_ref[...] = res[res.shape[0] - nb:, :].astype(ob_ref.dtype)


def _proj_t(x, w_t, n_a_heads):
    n, d = x.shape
    nw = w_t.shape[0]
    nb = nw - n_a_heads * HEAD_DIM
    tm = _tile(1024, n)
    return pl.pallas_call(
        _mm_t_kernel,
        grid=(n // tm,),
        in_specs=[pl.BlockSpec((nw, d), lambda i: (0, 0)),
                  pl.BlockSpec((tm, d), lambda i: (i, 0))],
        out_specs=[pl.BlockSpec((n_a_heads * V_ROWS, tm), lambda i: (0, i)),
                   pl.BlockSpec((nb, tm), lambda i: (0, i))],
        out_shape=[jax.ShapeDtypeStruct((n_a_heads * V_ROWS, n), BF16),
                   jax.ShapeDtypeStruct((nb, n), BF16)],
        compiler_params=_params(("parallel",),
                                2 * (tm * d * 2 + nw * d * 2 + 2 * nw * tm * 2) + 2 * nw * tm * 4),
        name="proj_t",
    )(w_t, x)


def _flash_kernel(q_ref, k_ref, vt_ref, *refs, tq, tk, nk, n_side):
    side_in = refs[:n_side]
    o_ref = refs[n_side]
    side_out = refs[n_side + 1:2 * n_side + 1]
    qs_ref, sa_ref, sb_ref, m_ref, acc_ref = refs[2 * n_side + 1:]
    for w_ref, wo_ref in zip(side_in, side_out):
        wo_ref[...] = w_ref[...].astype(wo_ref.dtype)
    for g in range(KV_GROUP):
        qs_ref[g * tq:(g + 1) * tq, :] = q_ref[:, g * HEAD_DIM:(g + 1) * HEAD_DIM]
    m_ref[...] = jnp.full(m_ref.shape, -jnp.inf, F32)
    acc_ref[...] = jnp.zeros(acc_ref.shape, F32)

    def scores(idx, s_ref):
        start = pl.multiple_of(idx * tk, tk)
        s_ref[...] = lax.dot_general(k_ref[pl.ds(start, tk), :], qs_ref[...], (((1,), (1,)), ((), ())),
                                     preferred_element_type=F32)

    def update(idx, s_ref):
        start = pl.multiple_of(idx * tk, tk)
        s = s_ref[...]
        m_prev = m_ref[...]
        m_new = jnp.maximum(m_prev, jnp.max(s, axis=0, keepdims=True))
        alpha = jnp.exp2(m_prev - m_new)
        p = jnp.exp2(s - m_new).astype(BF16)
        acc_ref[...] = alpha * acc_ref[...] + jnp.dot(vt_ref[:, pl.ds(start, tk)], p,
                                                      preferred_element_type=F32)
        m_ref[...] = m_new

    scores(0, sa_ref)
    pairs = (nk - 1) // 2

    def body(j, carry):
        scores(2 * j + 1, sb_ref)
        update(2 * j, sa_ref)
        scores(2 * j + 2, sa_ref)
        update(2 * j + 1, sb_ref)
        return carry

    lax.fori_loop(0, pairs, body, 0)
    if nk - 2 * pairs == 2:
        scores(nk - 1, sb_ref)
        update(nk - 2, sa_ref)
        update(nk - 1, sb_ref)
    else:
        update(nk - 1, sa_ref)
    out = (acc_ref[:HEAD_DIM, :] / acc_ref[HEAD_DIM:HEAD_DIM + 1, :]).T
    for g in range(KV_GROUP):
        o_ref[:, g * HEAD_DIM:(g + 1) * HEAD_DIM] = out[g * tq:(g + 1) * tq, :].astype(o_ref.dtype)


def _side_rows(rows, steps):
    need = -(-rows // steps)
    for br in range(-(-need // ONES_ROWS) * ONES_ROWS, rows, ONES_ROWS):
        if rows % br == 0:
            return br
    return rows


def _global_attention(qk, v_t, batch, t_seq, n_q_heads, side_casts=()):
    n = qk.shape[0]
    n_kv = n_q_heads // KV_GROUP
    tq = _tile(512 if side_casts else 1024, t_seq)
    tk = _tile(512, t_seq)
    nq = t_seq // tq
    nk = t_seq // tk
    gw = KV_GROUP * HEAD_DIM
    kern = functools.partial(_flash_kernel, tq=tq, tk=tk, nk=nk, n_side=len(side_casts))
    rows = KV_GROUP * tq
    steps = batch * n_kv * nq
    side_specs, side_shapes, side_bytes = [], [], 0
    for w in side_casts:
        br = _side_rows(w.shape[0], steps)
        last = pl.cdiv(w.shape[0], br) - 1
        index = functools.partial(lambda b, h, i, last: (jnp.minimum((b * n_kv + h) * nq + i, last), 0), last=last)
        side_specs.append(pl.BlockSpec((br, w.shape[1]), index))
        side_shapes.append(jax.ShapeDtypeStruct(w.shape, BF16))
        side_bytes += 2 * br * w.shape[1] * 6
    outs = pl.pallas_call(
        kern,
        grid=(batch, n_kv, nq),
        in_specs=[pl.BlockSpec((tq, gw), lambda b, h, i: (b * nq + i, h)),
                  pl.BlockSpec((t_seq, HEAD_DIM), lambda b, h, i: (b, n_q_heads + h)),
                  pl.BlockSpec((V_ROWS, t_seq), lambda b, h, i: (h, b))] + side_specs,
        out_specs=[pl.BlockSpec((tq, gw), lambda b, h, i: (b * nq + i, h))] + side_specs,
        out_shape=[jax.ShapeDtypeStruct((n, n_q_heads * HEAD_DIM), BF16)] + side_shapes,
        scratch_shapes=[pltpu.VMEM((rows, HEAD_DIM), BF16),
                        pltpu.VMEM((tk, rows), F32),
                        pltpu.VMEM((tk, rows), F32),
                        pltpu.VMEM((1, rows), F32),
                        pltpu.VMEM((V_ROWS, rows), F32)],
        compiler_params=_params(("arbitrary", "arbitrary", "arbitrary"),
                                2 * (2 * tq * gw * 2 + 2 * t_seq * HEAD_DIM * 2) + side_bytes
                                + rows * HEAD_DIM * 6 + 16 * rows * 4 + 8 * rows * tk * 4),
        name="global_attention",
    )(qk, qk, v_t, *side_casts)
    return outs[0], tuple(outs[1:])


def _window_bias(n_heads):
    key = jnp.arange(3 * Q_BLOCK, dtype=jnp.int32)[:, None] - Q_BLOCK
    qpos = jnp.arange(Q_BLOCK, dtype=jnp.int32)[None, :]
    dist = jnp.abs(qpos - key)
    slopes = 2.0 ** (-8.0 * jnp.arange(1, n_heads + 1, dtype=F32) / n_heads)
    bias = -slopes[:, None, None] * dist.astype(F32)[None] * LOG2_E
    n_kv = n_heads // KV_GROUP
    bias = bias.reshape(n_kv, KV_GROUP, 3 * Q_BLOCK, Q_BLOCK).transpose(0, 2, 1, 3)
    bias = bias.reshape(n_kv, 3 * Q_BLOCK, KV_GROUP * Q_BLOCK)
    in_window = (dist <= WINDOW)
    in_window = jnp.tile(in_window, (1, KV_GROUP))[None]
    row = jnp.arange(3 * Q_BLOCK)[None, :, None]
    not_prev = row >= Q_BLOCK
    not_next = row < 2 * Q_BLOCK
    variants = [in_window, in_window & not_prev, in_window & not_next, in_window & not_prev & not_next]
    return jnp.stack([jnp.where(v, bias, MASK_VALUE) for v in variants])


def _window_kernel(q_ref, kp_ref, kc_ref, kn_ref, vp_ref, vc_ref, vn_ref, *refs, n_kv, n_sub):
    bias_refs = refs[:n_sub]
    sink_ref, o_ref = refs[n_sub:]
    gw = KV_GROUP * HEAD_DIM
    for kv in range(n_kv):
        ks = slice(kv * HEAD_DIM, (kv + 1) * HEAD_DIM)
        k_all = jnp.concatenate([kp_ref[:, ks], kc_ref[:, ks], kn_ref[:, ks]], axis=0)
        vt_all = jnp.concatenate([vp_ref[ks, :], vc_ref[ks, :], vn_ref[ks, :]], axis=1)
        sink = sink_ref[:, kv * gw:(kv + 1) * gw] * LOG2_E
        for sub in range(n_sub):
            rows = slice(sub * Q_BLOCK, (sub + 1) * Q_BLOCK)
            keys = slice(sub * Q_BLOCK, (sub + 3) * Q_BLOCK)
            qs = jnp.concatenate([q_ref[rows, (kv * KV_GROUP + g) * HEAD_DIM:(kv * KV_GROUP + g + 1) * HEAD_DIM]
                                  for g in range(KV_GROUP)], axis=0)
            s = lax.dot_general(k_all[keys], qs, (((1,), (1,)), ((), ())), preferred_element_type=F32)
            s = s + bias_refs[sub][0, kv]
            m = jnp.maximum(jnp.max(s, axis=0, keepdims=True), sink)
            p = jnp.exp2(s - m)
            den = jnp.sum(p, axis=0, keepdims=True) + jnp.exp2(sink - m)
            pv = jnp.dot(vt_all[:, keys], p.astype(BF16), preferred_element_type=F32)
            out = (pv / den).T
            for g in range(KV_GROUP):
                hd = kv * KV_GROUP + g
                o_ref[rows, hd * HEAD_DIM:(hd + 1) * HEAD_DIM] = (
                    out[g * Q_BLOCK:(g + 1) * Q_BLOCK, :].astype(o_ref.dtype))


def _window_attention(rest, v_t, bias, sink_row, batch, t_seq, n_heads, k_col_block, vt_row_block):
    n = rest.shape[0]
    nb = t_seq // Q_BLOCK
    n_sub = max(s for s in (4, 2, 1) if nb % s == 0)
    steps = nb // n_sub
    n_kv = n_heads // KV_GROUP
    qw = n_heads * HEAD_DIM
    kw = n_kv * HEAD_DIM
    span = n_sub * Q_BLOCK

    def prev(b, i):
        return b * nb + jnp.maximum(i * n_sub - 1, 0)

    def cur(b, i):
        return b * steps + i

    def nxt(b, i):
        return b * nb + jnp.minimum((i + 1) * n_sub, nb - 1)

    def variant(b, i, sub):
        first = (i == 0).astype(jnp.int32) if sub == 0 else 0
        last = 2 * (i == steps - 1).astype(jnp.int32) if sub == n_sub - 1 else 0
        return first + last

    k_specs = [pl.BlockSpec((Q_BLOCK, kw), lambda b, i: (prev(b, i), k_col_block)),
               pl.BlockSpec((span, kw), lambda b, i: (cur(b, i), k_col_block)),
               pl.BlockSpec((Q_BLOCK, kw), lambda b, i: (nxt(b, i), k_col_block))]
    v_specs = [pl.BlockSpec((kw, Q_BLOCK), lambda b, i: (vt_row_block, prev(b, i))),
               pl.BlockSpec((kw, span), lambda b, i: (vt_row_block, cur(b, i))),
               pl.BlockSpec((kw, Q_BLOCK), lambda b, i: (vt_row_block, nxt(b, i)))]
    bias_specs = [pl.BlockSpec((1,) + bias.shape[1:],
                               functools.partial(lambda b, i, sub: (variant(b, i, sub), 0, 0, 0), sub=sub))
                  for sub in range(n_sub)]
    return pl.pallas_call(
        functools.partial(_window_kernel, n_kv=n_kv, n_sub=n_sub),
        grid=(batch, steps),
        in_specs=[pl.BlockSpec((span, qw), lambda b, i: (cur(b, i), 0))] + k_specs + v_specs + bias_specs
        + [pl.BlockSpec((1, qw), lambda b, i: (0, 0))],
        out_specs=pl.BlockSpec((span, qw), lambda b, i: (cur(b, i), 0)),
        out_shape=jax.ShapeDtypeStruct((n, qw), BF16),
        compiler_params=_params(("parallel", "arbitrary"),
                                2 * (n_sub * bias[0].size * 4 + 2 * span * qw * 2 + 4 * (span + 2 * Q_BLOCK) * kw)
                                + 16 * 3 * Q_BLOCK * KV_GROUP * Q_BLOCK * 4),
        name="window_attention",
    )(rest, rest, rest, rest, v_t, v_t, v_t, *([bias] * n_sub), sink_row)


def _merge_kernel(ya_ref, yb_ref, wa_ref, wb_ref, ga_ref, gb_ref, o_ref):
    a = jnp.dot(ya_ref[...], wa_ref[...], preferred_element_type=F32)
    b = jnp.dot(yb_ref[...], wb_ref[...], preferred_element_type=F32)
    ga = jax.nn.sigmoid(ga_ref[...].astype(F32))
    gb = jax.nn.sigmoid(gb_ref[...].astype(F32))
    o_ref[...] = (ga * a + gb * b).astype(o_ref.dtype)


def _merge(ya, yb, w_a, w_b, gates):
    n, ka = ya.shape
    kb = yb.shape[1]
    d = w_a.shape[1]
    tm = _tile(1024, n)
    tn = _tile(512, d)
    ga_blk = 0
    gb_blk = d // tn
    return pl.pallas_call(
        _merge_kernel,
        grid=(n // tm, d // tn),
        in_specs=[pl.BlockSpec((tm, ka), lambda i, j: (i, 0)),
                  pl.BlockSpec((tm, kb), lambda i, j: (i, 0)),
                  pl.BlockSpec((ka, tn), lambda i, j: (0, j)),
                  pl.BlockSpec((kb, tn), lambda i, j: (0, j)),
                  pl.BlockSpec((tm, tn), lambda i, j: (i, ga_blk + j)),
                  pl.BlockSpec((tm, tn), lambda i, j: (i, gb_blk + j))],
        out_specs=pl.BlockSpec((tm, tn), lambda i, j: (i, j)),
        out_shape=jax.ShapeDtypeStruct((n, d), BF16),
        compiler_params=_params(("parallel", "arbitrary"),
                                2 * (tm * (ka + kb) * 2 + (ka + kb) * tn * 2 + 3 * tm * tn * 2)
                                + 6 * tm * tn * 4),
        name="branch_merge",
    )(ya, yb, w_a, w_b, gates, gates)


def _resid_kernel(m_ref, w_ref, x_ref, g_ref, o_ref, *, alpha):
    acc = jnp.dot(m_ref[...], w_ref[...], preferred_element_type=F32)
    o_ref[...] = alpha * x_ref[...] + g_ref[0] * acc


def _out_proj(merged, w_o, x2, mod3, t_seq, i_gate, alpha):
    n, d = x2.shape
    k = merged.shape[1]
    tm = _tile(1024, t_seq)
    tn = _tile(512, d)
    per = t_seq // tm
    return pl.pallas_call(
        functools.partial(_resid_kernel, alpha=alpha),
        grid=(n // tm, d // tn),
        in_specs=[pl.BlockSpec((tm, k), lambda i, j: (i, 0)),
                  pl.BlockSpec((k, tn), lambda i, j: (0, j)),
                  pl.BlockSpec((tm, tn), lambda i, j: (i, j)),
                  pl.BlockSpec((1, 1, tn), lambda i, j: ((i // per) * N_MOD + i_gate, 0, j))],
        out_specs=pl.BlockSpec((tm, tn), lambda i, j: (i, j)),
        out_shape=jax.ShapeDtypeStruct((n, d), F32),
        compiler_params=_params(("parallel", "arbitrary"),
                                2 * (tm * k * 2 + k * tn * 2 + 2 * tm * tn * 4) + 2 * tm * tn * 4),
        name="out_proj_residual",
    )(merged, w_o, x2, mod3)


def _ln_pair_kernel(z_ref, g_ref, b_ref, sc_ref, sh_ref, x1_ref, h_ref):
    x1 = _ln(z_ref[...]) * g_ref[...] + b_ref[...]
    x1_ref[...] = x1
    h_ref[...] = (_ln(x1) * (1.0 + sc_ref[0]) + sh_ref[0]).astype(h_ref.dtype)


def _ln_pair(z, ln_g, ln_b, mod3, t_seq, i_scale, i_shift):
    n, d = z.shape
    tm = _tile(256, t_seq)
    per = t_seq // tm
    return pl.pallas_call(
        _ln_pair_kernel,
        grid=(n // tm,),
        in_specs=[pl.BlockSpec((tm, d), lambda i: (i, 0)),
                  pl.BlockSpec((1, d), lambda i: (0, 0)),
                  pl.BlockSpec((1, d), lambda i: (0, 0)),
                  pl.BlockSpec((1, 1, d), lambda i: ((i // per) * N_MOD + i_scale, 0, 0)),
                  pl.BlockSpec((1, 1, d), lambda i: ((i // per) * N_MOD + i_shift, 0, 0))],
        out_specs=[pl.BlockSpec((tm, d), lambda i: (i, 0)),
                   pl.BlockSpec((tm, d), lambda i: (i, 0))],
        out_shape=[jax.ShapeDtypeStruct((n, d), F32), jax.ShapeDtypeStruct((n, d), BF16)],
        compiler_params=_params(("parallel",), 2 * tm * d * 10 + 6 * tm * d * 4),
        name="ln1_ln2mod",
    )(z, ln_g.reshape(1, d), ln_b.reshape(1, d), mod3, mod3)


def _ffn_up_kernel(h_ref, wg_ref, wu_ref, o_ref):
    h = h_ref[...]
    g = jnp.dot(h, wg_ref[...], preferred_element_type=F32)
    u = jnp.dot(h, wu_ref[...], preferred_element_type=F32)
    o_ref[...] = (g * jax.nn.sigmoid(g) * u).astype(o_ref.dtype)


def _ffn_up(h, w_g, w_u):
    n, d = h.shape
    ff = w_g.shape[1]
    tm = _tile(1024, n)
    tn = min(512, ff)
    return pl.pallas_call(
        _ffn_up_kernel,
        grid=(n // tm, pl.cdiv(ff, tn)),
        in_specs=[pl.BlockSpec((tm, d), lambda i, j: (i, 0)),
                  pl.BlockSpec((d, tn), lambda i, j: (0, j)),
                  pl.BlockSpec((d, tn), lambda i, j: (0, j))],
        out_specs=pl.BlockSpec((tm, tn), lambda i, j: (i, j)),
        out_shape=jax.ShapeDtypeStruct((n, ff), BF16),
        compiler_params=_params(("parallel", "arbitrary"),
                                2 * (tm * d * 2 + d * 2 * tn * 2 + tm * tn * 2) + 3 * tm * 2 * tn * 4),
        name="ffn_up",
    )(h, w_g, w_u)


def _ffn_down_kernel(a_ref, w_ref, x_ref, gate_ref, g_ref, b_ref, o_ref, *, alpha, nk, last, rows):
    k = pl.program_id(1)
    tk = a_ref.shape[1]

    @pl.when(k == 0)
    def _():
        o_ref[...] = jnp.zeros(o_ref.shape, F32)

    if last == tk:
        o_ref[...] += jnp.dot(a_ref[...], w_ref[...], preferred_element_type=F32)
    else:
        @pl.when(k < nk - 1)
        def _():
            o_ref[...] += jnp.dot(a_ref[...], w_ref[...], preferred_element_type=F32)

        @pl.when(k == nk - 1)
        def _():
            o_ref[...] += jnp.dot(a_ref[:, :last], w_ref[:last, :], preferred_element_type=F32)

    @pl.when(k == nk - 1)
    def _():
        def chunk(r, carry):
            sl = pl.ds(pl.multiple_of(r * rows, rows), rows)
            z = alpha * x_ref[sl, :] + gate_ref[0] * o_ref[sl, :]
            o_ref[sl, :] = _ln(z) * g_ref[...] + b_ref[...]
            return carry

        lax.fori_loop(0, o_ref.shape[0] // rows, chunk, 0)


def _ffn_down(a, w_d, x1, ln_g, ln_b, mod3, t_seq, i_gate, alpha):
    n, ff = a.shape
    d = w_d.shape[1]
    tm = _tile(512, t_seq)
    tk = min(1024, ff)
    nk = pl.cdiv(ff, tk)
    last = ff - (nk - 1) * tk
    assert last % LANE == 0, "the partial contraction block must stay lane-aligned"
    per = t_seq // tm
    return pl.pallas_call(
        functools.partial(_ffn_down_kernel, alpha=alpha, nk=nk, last=last, rows=_tile(LANE, tm)),
        grid=(n // tm, nk),
        in_specs=[pl.BlockSpec((tm, tk), lambda i, k: (i, k)),
                  pl.BlockSpec((tk, d), lambda i, k: (k, 0)),
                  pl.BlockSpec((tm, d), lambda i, k: (i, 0)),
                  pl.BlockSpec((1, 1, d), lambda i, k: ((i // per) * N_MOD + i_gate, 0, 0)),
                  pl.BlockSpec((1, d), lambda i, k: (0, 0)),
                  pl.BlockSpec((1, d), lambda i, k: (0, 0))],
        out_specs=pl.BlockSpec((tm, d), lambda i, k: (i, 0)),
        out_shape=jax.ShapeDtypeStruct((n, d), F32),
        compiler_params=_params(("parallel", "arbitrary"),
                                2 * (tm * tk * 2 + tk * d * 2 + 2 * tm * d * 4) + 2 * tm * d * 4),
        name="ffn_down_ln",
    )(a, w_d, x1, mod3, ln_g.reshape(1, d), ln_b.reshape(1, d))


def _rope_tables(t_max):
    rows = t_max // GRID_W
    row = jnp.repeat(jnp.arange(rows, dtype=F32), GRID_W)
    col = jnp.tile(jnp.arange(GRID_W, dtype=F32), rows)
    inv = 1.0 / (ROPE_THETA ** (jnp.arange(0, HALF_ROT, 2, dtype=F32) / HALF_ROT))
    ang_r = row[:, None] * inv[None, :]
    ang_c = col[:, None] * inv[None, :]
    cr, sr, cc, sc = jnp.cos(ang_r), jnp.sin(ang_r), jnp.cos(ang_c), jnp.sin(ang_c)
    cos_t = jnp.concatenate([cr, cc, cr, cc], axis=-1)
    sin_t = jnp.concatenate([-sr, -sc, sr, sc], axis=-1)
    return cos_t, sin_t


def _rot_layout(w):
    lead = w.shape[:-1]
    w = w.reshape(lead + (-1, 2, 2, HALF_ROT // 2))
    return jnp.swapaxes(w, -3, -2).reshape(lead + (-1,))


def _prep_layer(w_in, q_norm, k_norm):
    d = w_in.shape[0]
    n_heads = d // HEAD_DIM
    ha = n_heads // 2
    qa_w = ha * HEAD_DIM
    kv_w = (ha // KV_GROUP) * HEAD_DIM
    o_ka = qa_w
    o_va = o_ka + kv_w
    o_qb = o_va + kv_w
    o_kb = o_qb + qa_w
    o_vb = o_kb + kv_w
    o_ga = o_vb + kv_w
    w_qk = _rot_layout(_cast_cols(w_in, 0, o_va))
    w_v_t = jnp.concatenate([_cast_cols(w_in, o_va, kv_w), _cast_cols(w_in, o_vb, kv_w)], axis=1).T
    w_qkb = _cast_cols(w_in, o_qb, o_vb - o_qb)
    w_gates = _cast_cols(w_in, o_ga, w_in.shape[1] - o_ga)
    gain = jnp.concatenate([jnp.tile(_rot_layout(q_norm), ha),
                            jnp.tile(_rot_layout(k_norm), ha // KV_GROUP)]).reshape(1, -1)
    q_scale = jnp.full((qa_w,), LOG2_E / math.sqrt(HEAD_DIM), F32)
    scale = jnp.concatenate([q_scale, jnp.ones((kv_w,), F32)]).reshape(1, -1)
    return dict(w_qk=w_qk, w_v_t=w_v_t, w_qkb=w_qkb, w_gates=w_gates, gain=gain, scale=scale,
                ha=ha, kv_w=kv_w, qa_w=qa_w)


def _encoder_layer(x, mod, p, late_w, sink_row, win_bias, ln1_g, ln1_b, ln2_g, ln2_b, cos_t, sin_t, alpha):
    b, t, d = x.shape
    x2 = x.reshape(b * t, d)
    mod3 = mod.reshape(b * N_MOD, 1, d)
    ha, kv_w, qa_w = p["ha"], p["kv_w"], p["qa_w"]

    h = _ln_mod(x2, mod3, t, 1, 0)
    qk = _qk_proj(h, p["w_qk"], p["gain"], p["scale"], cos_t, sin_t, t)
    va_t, vb_t = _proj_t(h, p["w_v_t"], kv_w // HEAD_DIM)
    qkb = _proj(h, p["w_qkb"], p["scale"])
    gates = _proj(h, p["w_gates"], None, tn_pref=1024)
    if late_w[0].dtype == BF16:
        ya, _ = _global_attention(qk, va_t, b, t, ha)
    else:
        ya, late_w = _global_attention(qk, va_t, b, t, ha, side_casts=late_w)
    w_a, w_b, w_o, w_g, w_u, w_d = late_w
    yb = _window_attention(qkb, vb_t, win_bias, sink_row, b, t, ha, k_col_block=qa_w // kv_w,
                           vt_row_block=0)
    merged = _merge(ya, yb, w_a, w_b, gates)
    z = _out_proj(merged, w_o, x2, mod3, t, 2, alpha)
    x1, h2 = _ln_pair(z, ln1_g, ln1_b, mod3, t, 4, 3)
    a = _ffn_up(h2, w_g, w_u)
    y = _ffn_down(a, w_d, x1, ln2_g, ln2_b, mod3, t, 5, alpha)
    return y.reshape(b, t, d), late_w


def kernel(x_prompt, x_sample, c_prompt, c_sample, w_ada, b_ada, w_in, q_norm_a, k_norm_a, sink_b,
           w_br_a, w_br_b, w_o, ln1_g, ln1_b, w_ffn_gate, w_ffn_up, w_ffn_down, ln2_g, ln2_b):
    depth = w_ada.shape[0]
    alpha = float((2.0 * depth) ** 0.25)
    d = x_prompt.shape[-1]
    bp, bs = c_prompt.shape[0], c_sample.shape[0]
    rows = -(-(bp + bs) // 8) * 8
    cos_t, sin_t = _rope_tables(max(x_prompt.shape[1], x_sample.shape[1]))
    y_p, y_s = x_prompt, x_sample
    c_all = jnp.concatenate([c_prompt, c_sample, jnp.zeros((rows - bp - bs, d), F32)], axis=0)
    win_bias = _window_bias(sink_b.shape[1])
    for l in range(depth):
        mod = _ada(c_all, w_ada[l], b_ada[l]).reshape(rows, N_MOD, d)
        p = _prep_layer(w_in[l], q_norm_a[l], k_norm_a[l])
        late_w = (w_br_a[l], w_br_b[l], w_o[l], w_ffn_gate[l], w_ffn_up[l], w_ffn_down[l])
        sink_row = jnp.repeat(sink_b[l], HEAD_DIM).reshape(1, -1)
        args = (sink_row, win_bias, ln1_g[l], ln1_b[l], ln2_g[l], ln2_b[l], cos_t, sin_t, alpha)
        y_p, late_w = _encoder_layer(y_p, mod[:bp], p, late_w, *args)
        y_s, _ = _encoder_layer(y_s, mod[bp:bp + bs], p, late_w, *args)
    return (y_p, y_s)
```

```python
import functools
import math

import jax
import jax.numpy as jnp
from jax import lax
from jax.experimental import pallas as pl
from jax.experimental.pallas import tpu as pltpu

F32 = jnp.float32
BF16 = jnp.bfloat16

HEAD_DIM = 128
GRID_W = 64
Q_BLOCK = 128
WINDOW = 128
HALF_ROT = HEAD_DIM // 2
ROPE_THETA = 10000.0
N_MOD = 6
LN_EPS = 1e-5
RMS_EPS = 1e-6
KV_GROUP = 4
MASK_VALUE = -1e30
LOG2_E = 1.4426950408889634

ONES_ROWS = 16
V_ROWS = HEAD_DIM + ONES_ROWS
LANE = 128
VMEM_CAP_BYTES = 60 * 1024 * 1024
CAST_BLOCK_BYTES = 4 * 1024 * 1024


def _tile(pref, *dims):
    t = (min(pref, *dims) // LANE) * LANE
    while t >= LANE:
        if all(d % t == 0 for d in dims):
            return t
        t -= LANE
    return min(dims)


def _params(semantics, vmem_bytes):
    limit = int(min(max(vmem_bytes, 16 * 1024 * 1024), VMEM_CAP_BYTES))
    return pltpu.CompilerParams(dimension_semantics=semantics, vmem_limit_bytes=limit)


def _ln(x):
    mu = jnp.mean(x, axis=-1, keepdims=True)
    xc = x - mu
    return xc * lax.rsqrt(jnp.mean(xc * xc, axis=-1, keepdims=True) + LN_EPS)


def _ada_kernel(c_ref, w_ref, b_ref, o_ref):
    c = c_ref[...]
    a = (c * jax.nn.sigmoid(c)).astype(BF16)
    o_ref[...] = jnp.dot(a, w_ref[...].astype(BF16), preferred_element_type=F32) + b_ref[...]


def _ada(c_pad, w_ada, b_ada):
    rows, d = c_pad.shape
    n = w_ada.shape[1]
    tn = _tile(512, n)
    return pl.pallas_call(
        _ada_kernel,
        grid=(n // tn,),
        in_specs=[pl.BlockSpec((rows, d), lambda j: (0, 0)),
                  pl.BlockSpec((d, tn), lambda j: (0, j)),
                  pl.BlockSpec((1, tn), lambda j: (0, j))],
        out_specs=pl.BlockSpec((rows, tn), lambda j: (0, j)),
        out_shape=jax.ShapeDtypeStruct((rows, n), F32),
        compiler_params=_params(("parallel",), 2 * d * tn * 4 + d * tn * 2 + (4 << 20)),
        name="ada_mod",
    )(c_pad, w_ada, b_ada.reshape(1, n))


def _ln_mod_kernel(x_ref, sc_ref, sh_ref, o_ref):
    y = _ln(x_ref[...])
    o_ref[...] = (y * (1.0 + sc_ref[0]) + sh_ref[0]).astype(o_ref.dtype)


def _ln_mod(x2, mod3, t_seq, i_scale, i_shift):
    n, d = x2.shape
    tm = _tile(512, t_seq)
    per = t_seq // tm
    return pl.pallas_call(
        _ln_mod_kernel,
        grid=(n // tm,),
        in_specs=[pl.BlockSpec((tm, d), lambda i: (i, 0)),
                  pl.BlockSpec((1, 1, d), lambda i: ((i // per) * N_MOD + i_scale, 0, 0)),
                  pl.BlockSpec((1, 1, d), lambda i: ((i // per) * N_MOD + i_shift, 0, 0))],
        out_specs=pl.BlockSpec((tm, d), lambda i: (i, 0)),
        out_shape=jax.ShapeDtypeStruct((n, d), BF16),
        compiler_params=_params(("parallel",), 2 * tm * d * 6 + 4 * tm * d * 4),
        name="ln_mod",
    )(x2, mod3, mod3)


def _qk_kernel(h_ref, w_ref, g_ref, s_ref, cos_ref, sin_ref, q_ref, k_ref, *, chunk):
    h = h_ref[...]
    cos = cos_ref[...]
    sin = sin_ref[...]
    n_q = q_ref.shape[1] // HEAD_DIM
    for c in range(w_ref.shape[1] // chunk):
        acc = jnp.dot(h, w_ref[:, c * chunk:(c + 1) * chunk], preferred_element_type=F32)
        for hh in range(chunk // HEAD_DIM):
            head = c * (chunk // HEAD_DIM) + hh
            sl = slice(head * HEAD_DIM, (head + 1) * HEAD_DIM)
            x = acc[:, hh * HEAD_DIM:(hh + 1) * HEAD_DIM]
            y = x * lax.rsqrt(jnp.mean(x * x, axis=-1, keepdims=True) + RMS_EPS) * g_ref[:, sl]
            partner = pltpu.roll(y, HALF_ROT, 1)
            out = ((y * cos + partner * sin) * s_ref[:, sl]).astype(q_ref.dtype)
            if head < n_q:
                q_ref[:, sl] = out
            else:
                k_ref[head - n_q] = out


def _qk_proj(h, w_qk, gain, scale, cos_t, sin_t, t_seq, n_q_heads):
    n, d = h.shape
    nw = w_qk.shape[1]
    qw = n_q_heads * HEAD_DIM
    n_kv = (nw - qw) // HEAD_DIM
    tm = _tile(512, t_seq)
    per = t_seq // tm
    return pl.pallas_call(
        functools.partial(_qk_kernel, chunk=_tile(256, nw)),
        grid=(n // tm,),
        in_specs=[pl.BlockSpec((tm, d), lambda i: (i, 0)),
                  pl.BlockSpec((d, nw), lambda i: (0, 0), pipeline_mode=pl.Buffered(1)),
                  pl.BlockSpec((1, nw), lambda i: (0, 0)),
                  pl.BlockSpec((1, nw), lambda i: (0, 0)),
                  pl.BlockSpec((tm, HEAD_DIM), lambda i: (i % per, 0)),
                  pl.BlockSpec((tm, HEAD_DIM), lambda i: (i % per, 0))],
        out_specs=[pl.BlockSpec((tm, qw), lambda i: (i, 0)),
                   pl.BlockSpec((n_kv, tm, HEAD_DIM), lambda i: (0, i, 0))],
        out_shape=[jax.ShapeDtypeStruct((n, qw), BF16),
                   jax.ShapeDtypeStruct((n_kv, n, HEAD_DIM), BF16)],
        compiler_params=_params(("parallel",),
                                d * nw * 2 + 2 * (tm * d * 2 + tm * nw * 2 + 2 * tm * HEAD_DIM * 4)
                                + 16 * tm * 256 * 4),
        name="qk_proj_rope",
    )(h, w_qk, gain, scale, cos_t, sin_t)


def _cast_kernel(w_ref, o_ref):
    o_ref[...] = w_ref[...].astype(o_ref.dtype)


def _cast_cols(w, col0=0, ncols=None):
    rows, width = w.shape
    ncols = width if ncols is None else ncols
    if col0 == 0 and ncols == width:
        tc = width
    else:
        tc = _tile(2048, ncols, *((col0,) if col0 else ()))
    tr = max(8, min(rows, (CAST_BLOCK_BYTES // (4 * tc)) // 8 * 8))
    while rows % tr:
        tr -= 8
    c0 = col0 // tc
    return pl.pallas_call(
        _cast_kernel,
        grid=(rows // tr, ncols // tc),
        in_specs=[pl.BlockSpec((tr, tc), lambda i, j: (i, c0 + j))],
        out_specs=pl.BlockSpec((tr, tc), lambda i, j: (i, j)),
        out_shape=jax.ShapeDtypeStruct((rows, ncols), BF16),
        compiler_params=_params(("parallel", "parallel"), 2 * tr * tc * 6 + tr * tc * 4),
        name="cast_bf16",
    )(w)


def _mm_scaled_kernel(x_ref, w_ref, s_ref, o_ref):
    acc = jnp.dot(x_ref[...], w_ref[...], preferred_element_type=F32)
    o_ref[...] = (acc * s_ref[...]).astype(o_ref.dtype)


def _mm_kernel(x_ref, w_ref, o_ref):
    o_ref[...] = jnp.dot(x_ref[...], w_ref[...], preferred_element_type=F32).astype(o_ref.dtype)


def _proj(x, w, col_scale, tn_pref=512):
    n, d = x.shape
    nw = w.shape[1]
    tm = _tile(1024, n)
    tn = _tile(tn_pref, nw)
    scaled = col_scale is not None
    in_specs = [pl.BlockSpec((tm, d), lambda i, j: (i, 0)),
                pl.BlockSpec((d, tn), lambda i, j: (0, j))]
    if scaled:
        in_specs.append(pl.BlockSpec((1, tn), lambda i, j: (0, j)))
    return pl.pallas_call(
        _mm_scaled_kernel if scaled else _mm_kernel,
        grid=(n // tm, nw // tn),
        in_specs=in_specs,
        out_specs=pl.BlockSpec((tm, tn), lambda i, j: (i, j)),
        out_shape=jax.ShapeDtypeStruct((n, nw), BF16),
        compiler_params=_params(("parallel", "arbitrary"),
                                2 * (tm * d * 2 + d * tn * 2 + tm * tn * 2) + 2 * tm * tn * 4),
        name="proj",
    )(*((x, w, col_scale) if scaled else (x, w)))


def _mm_t_kernel(wt_ref, x_ref, oa_ref, ob_ref):
    res = lax.dot_general(wt_ref[...], x_ref[...], (((1,), (1,)), ((), ())), preferred_element_type=F32)
    nb = ob_ref.shape[1]
    tm = x_ref.shape[0]
    for j in range(oa_ref.shape[0] // V_ROWS):
        oa_ref[j * V_ROWS:j * V_ROWS + HEAD_DIM, :] = res[j * HEAD_DIM:(j + 1) * HEAD_DIM, :].astype(oa_ref.dtype)
        oa_ref[j * V_ROWS + HEAD_DIM:(j + 1) * V_ROWS, :] = jnp.ones((ONES_ROWS, tm), oa_ref.dtype)
    for j in range(ob_ref.shape[0]):
        ob_ref[j] = res[res.shape[0] - nb:, j * Q_BLOCK:(j + 1) * Q_BLOCK].astype(ob_ref.dtype)


def _proj_t(x, w_t, n_a_heads):
    n, d = x.shape
    nw = w_t.shape[0]
    nb = nw - n_a_heads * HEAD_DIM
    tm = _tile(1024, n)
    return pl.pallas_call(
        _mm_t_kernel,
        grid=(n // tm,),
        in_specs=[pl.BlockSpec((nw, d), lambda i: (0, 0)),
                  pl.BlockSpec((tm, d), lambda i: (i, 0))],
        out_specs=[pl.BlockSpec((n_a_heads * V_ROWS, tm), lambda i: (0, i)),
                   pl.BlockSpec((tm // Q_BLOCK, nb, Q_BLOCK), lambda i: (i, 0, 0))],
        out_shape=[jax.ShapeDtypeStruct((n_a_heads * V_ROWS, n), BF16),
                   jax.ShapeDtypeStruct((n // Q_BLOCK, nb, Q_BLOCK), BF16)],
        compiler_params=_params(("parallel",),
                                2 * (tm * d * 2 + nw * d * 2 + 2 * nw * tm * 2) + 2 * nw * tm * 4),
        name="proj_t",
    )(w_t, x)


def _flash_kernel(q_ref, k_ref, vt_ref, *refs, tq, tk, nk, n_side):
    side_in = refs[:n_side]
    o_ref = refs[n_side]
    side_out = refs[n_side + 1:2 * n_side + 1]
    qs_ref, sa_ref, sb_ref, m_ref, acc_ref = refs[2 * n_side + 1:]
    for w_ref, wo_ref in zip(side_in, side_out):
        wo_ref[...] = w_ref[...].astype(wo_ref.dtype)
    for g in range(KV_GROUP):
        qs_ref[g * tq:(g + 1) * tq, :] = q_ref[:, g * HEAD_DIM:(g + 1) * HEAD_DIM]
    m_ref[...] = jnp.full(m_ref.shape, -jnp.inf, F32)
    acc_ref[...] = jnp.zeros(acc_ref.shape, F32)

    def scores(idx, s_ref):
        start = pl.multiple_of(idx * tk, tk)
        s_ref[...] = lax.dot_general(k_ref[0, pl.ds(start, tk), :], qs_ref[...], (((1,), (1,)), ((), ())),
                                     preferred_element_type=F32)

    def update(idx, s_ref):
        start = pl.multiple_of(idx * tk, tk)
        s = s_ref[...]
        m_prev = m_ref[...]
        m_new = jnp.maximum(m_prev, jnp.max(s, axis=0, keepdims=True))
        alpha = jnp.exp2(m_prev - m_new)
        p = jnp.exp2(s - m_new).astype(BF16)
        acc_ref[...] = alpha * acc_ref[...] + jnp.dot(vt_ref[:, pl.ds(start, tk)], p,
                                                      preferred_element_type=F32)
        m_ref[...] = m_new

    scores(0, sa_ref)
    pairs = (nk - 1) // 2

    def body(j, carry):
        scores(2 * j + 1, sb_ref)
        update(2 * j, sa_ref)
        scores(2 * j + 2, sa_ref)
        update(2 * j + 1, sb_ref)
        return carry

    lax.fori_loop(0, pairs, body, 0)
    if nk - 2 * pairs == 2:
        scores(nk - 1, sb_ref)
        update(nk - 2, sa_ref)
        update(nk - 1, sb_ref)
    else:
        update(nk - 1, sa_ref)
    out = (acc_ref[:HEAD_DIM, :] / acc_ref[HEAD_DIM:HEAD_DIM + 1, :]).T
    for g in range(KV_GROUP):
        o_ref[:, g * HEAD_DIM:(g + 1) * HEAD_DIM] = out[g * tq:(g + 1) * tq, :].astype(o_ref.dtype)


def _side_rows(rows, steps):
    need = -(-rows // steps)
    for br in range(-(-need // ONES_ROWS) * ONES_ROWS, rows, ONES_ROWS):
        if rows % br == 0:
            return br
    return rows


def _global_attention(q, k_hm, v_t, batch, t_seq, n_q_heads, side_casts=()):
    n = q.shape[0]
    n_kv = n_q_heads // KV_GROUP
    tq = _tile(512 if side_casts else 1024, t_seq)
    tk = _tile(512, t_seq)
    nq = t_seq // tq
    nk = t_seq // tk
    gw = KV_GROUP * HEAD_DIM
    kern = functools.partial(_flash_kernel, tq=tq, tk=tk, nk=nk, n_side=len(side_casts))
    rows = KV_GROUP * tq
    steps = batch * n_kv * nq
    side_specs, side_shapes, side_bytes = [], [], 0
    for w in side_casts:
        br = _side_rows(w.shape[0], steps)
        last = pl.cdiv(w.shape[0], br) - 1
        index = functools.partial(lambda b, h, i, last: (jnp.minimum((b * n_kv + h) * nq + i, last), 0), last=last)
        side_specs.append(pl.BlockSpec((br, w.shape[1]), index))
        side_shapes.append(jax.ShapeDtypeStruct(w.shape, BF16))
        side_bytes += 2 * br * w.shape[1] * 6
    outs = pl.pallas_call(
        kern,
        grid=(batch, n_kv, nq),
        in_specs=[pl.BlockSpec((tq, gw), lambda b, h, i: (b * nq + i, h)),
                  pl.BlockSpec((1, t_seq, HEAD_DIM), lambda b, h, i: (h, b, 0)),
                  pl.BlockSpec((V_ROWS, t_seq), lambda b, h, i: (h, b))] + side_specs,
        out_specs=[pl.BlockSpec((tq, gw), lambda b, h, i: (b * nq + i, h))] + side_specs,
        out_shape=[jax.ShapeDtypeStruct((n, n_q_heads * HEAD_DIM), BF16)] + side_shapes,
        scratch_shapes=[pltpu.VMEM((rows, HEAD_DIM), BF16),
                        pltpu.VMEM((tk, rows), F32),
                        pltpu.VMEM((tk, rows), F32),
                        pltpu.VMEM((1, rows), F32),
                        pltpu.VMEM((V_ROWS, rows), F32)],
        compiler_params=_params(("arbitrary", "arbitrary", "arbitrary"),
                                2 * (2 * tq * gw * 2 + 2 * t_seq * HEAD_DIM * 2) + side_bytes
                                + rows * HEAD_DIM * 6 + 16 * rows * 4 + 8 * rows * tk * 4),
        name="global_attention",
    )(q, k_hm, v_t, *side_casts)
    return outs[0], tuple(outs[1:])


def _window_bias(n_heads):
    key = jnp.arange(3 * Q_BLOCK, dtype=jnp.int32)[:, None] - Q_BLOCK
    qpos = jnp.arange(Q_BLOCK, dtype=jnp.int32)[None, :]
    dist = jnp.abs(qpos - key)
    slopes = 2.0 ** (-8.0 * jnp.arange(1, n_heads + 1, dtype=F32) / n_heads)
    bias = -slopes[:, None, None] * dist.astype(F32)[None] * LOG2_E
    n_kv = n_heads // KV_GROUP
    bias = bias.reshape(n_kv, KV_GROUP, 3 * Q_BLOCK, Q_BLOCK).transpose(0, 2, 1, 3)
    bias = bias.reshape(n_kv, 3 * Q_BLOCK, KV_GROUP * Q_BLOCK)
    in_window = (dist <= WINDOW)
    in_window = jnp.tile(in_window, (1, KV_GROUP))[None]
    row = jnp.arange(3 * Q_BLOCK)[None, :, None]
    not_prev = row >= Q_BLOCK
    not_next = row < 2 * Q_BLOCK
    variants = [in_window, in_window & not_prev, in_window & not_next, in_window & not_prev & not_next]
    return jnp.stack([jnp.where(v, bias, MASK_VALUE) for v in variants])


def _window_kernel(q_ref, kp_ref, kc_ref, kn_ref, vp_ref, vc_ref, vn_ref, *refs, n_kv, n_sub):
    bias_refs = refs[:n_sub]
    sink_ref, o_ref = refs[n_sub:]
    gw = KV_GROUP * HEAD_DIM
    for kv in range(n_kv):
        ks = slice(kv * HEAD_DIM, (kv + 1) * HEAD_DIM)
        k_all = jnp.concatenate([kp_ref[:, ks], kc_ref[:, ks], kn_ref[:, ks]], axis=0)
        vt_all = jnp.concatenate([vp_ref[0, ks, :]] + [vc_ref[j, ks, :] for j in range(n_sub)]
                                 + [vn_ref[0, ks, :]], axis=1)
        sink = sink_ref[:, kv * gw:(kv + 1) * gw] * LOG2_E
        for sub in range(n_sub):
            rows = slice(sub * Q_BLOCK, (sub + 1) * Q_BLOCK)
            keys = slice(sub * Q_BLOCK, (sub + 3) * Q_BLOCK)
            qs = jnp.concatenate([q_ref[rows, (kv * KV_GROUP + g) * HEAD_DIM:(kv * KV_GROUP + g + 1) * HEAD_DIM]
                                  for g in range(KV_GROUP)], axis=0)
            s = lax.dot_general(k_all[keys], qs, (((1,), (1,)), ((), ())), preferred_element_type=F32)
            s = s + bias_refs[sub][0, kv]
            m = jnp.maximum(jnp.max(s, axis=0, keepdims=True), sink)
            p = jnp.exp2(s - m)
            den = jnp.sum(p, axis=0, keepdims=True) + jnp.exp2(sink - m)
            pv = jnp.dot(vt_all[:, keys], p.astype(BF16), preferred_element_type=F32)
            out = (pv / den).T
            for g in range(KV_GROUP):
                hd = kv * KV_GROUP + g
                o_ref[rows, hd * HEAD_DIM:(hd + 1) * HEAD_DIM] = (
                    out[g * Q_BLOCK:(g + 1) * Q_BLOCK, :].astype(o_ref.dtype))


def _window_attention(rest, v_t, bias, sink_row, batch, t_seq, n_heads, k_col_block):
    n = rest.shape[0]
    nb = t_seq // Q_BLOCK
    n_sub = max(s for s in (4, 2, 1) if nb % s == 0)
    steps = nb // n_sub
    n_kv = n_heads // KV_GROUP
    qw = n_heads * HEAD_DIM
    kw = n_kv * HEAD_DIM
    span = n_sub * Q_BLOCK

    def prev(b, i):
        return b * nb + jnp.maximum(i * n_sub - 1, 0)

    def cur(b, i):
        return b * steps + i

    def nxt(b, i):
        return b * nb + jnp.minimum((i + 1) * n_sub, nb - 1)

    def variant(b, i, sub):
        first = (i == 0).astype(jnp.int32) if sub == 0 else 0
        last = 2 * (i == steps - 1).astype(jnp.int32) if sub == n_sub - 1 else 0
        return first + last

    k_specs = [pl.BlockSpec((Q_BLOCK, kw), lambda b, i: (prev(b, i), k_col_block)),
               pl.BlockSpec((span, kw), lambda b, i: (cur(b, i), k_col_block)),
               pl.BlockSpec((Q_BLOCK, kw), lambda b, i: (nxt(b, i), k_col_block))]
    v_specs = [pl.BlockSpec((1, kw, Q_BLOCK), lambda b, i: (prev(b, i), 0, 0)),
               pl.BlockSpec((n_sub, kw, Q_BLOCK), lambda b, i: (cur(b, i), 0, 0)),
               pl.BlockSpec((1, kw, Q_BLOCK), lambda b, i: (nxt(b, i), 0, 0))]
    bias_specs = [pl.BlockSpec((1,) + bias.shape[1:],
                               functools.partial(lambda b, i, sub: (variant(b, i, sub), 0, 0, 0), sub=sub))
                  for sub in range(n_sub)]
    return pl.pallas_call(
        functools.partial(_window_kernel, n_kv=n_kv, n_sub=n_sub),
        grid=(batch, steps),
        in_specs=[pl.BlockSpec((span, qw), lambda b, i: (cur(b, i), 0))] + k_specs + v_specs + bias_specs
        + [pl.BlockSpec((1, qw), lambda b, i: (0, 0))],
        out_specs=pl.BlockSpec((span, qw), lambda b, i: (cur(b, i), 0)),
        out_shape=jax.ShapeDtypeStruct((n, qw), BF16),
        compiler_params=_params(("parallel", "arbitrary"),
                                2 * (n_sub * bias[0].size * 4 + 2 * span * qw * 2 + 4 * (span + 2 * Q_BLOCK) * kw)
                                + 16 * 3 * Q_BLOCK * KV_GROUP * Q_BLOCK * 4),
        name="window_attention",
    )(rest, rest, rest, rest, v_t, v_t, v_t, *([bias] * n_sub), sink_row)


def _merge_kernel(ya_ref, yb_ref, wa_ref, wb_ref, ga_ref, gb_ref, o_ref):
    a = jnp.dot(ya_ref[...], wa_ref[...], preferred_element_type=F32)
    b = jnp.dot(yb_ref[...], wb_ref[...], preferred_element_type=F32)
    ga = jax.nn.sigmoid(ga_ref[...].astype(F32))
    gb = jax.nn.sigmoid(gb_ref[...].astype(F32))
    o_ref[...] = (ga * a + gb * b).astype(o_ref.dtype)


def _merge(ya, yb, w_a, w_b, gates):
    n, ka = ya.shape
    kb = yb.shape[1]
    d = w_a.shape[1]
    tm = _tile(1024, n)
    tn = _tile(512, d)
    ga_blk = 0
    gb_blk = d // tn
    return pl.pallas_call(
        _merge_kernel,
        grid=(n // tm, d // tn),
        in_specs=[pl.BlockSpec((tm, ka), lambda i, j: (i, 0)),
                  pl.BlockSpec((tm, kb), lambda i, j: (i, 0)),
                  pl.BlockSpec((ka, tn), lambda i, j: (0, j)),
                  pl.BlockSpec((kb, tn), lambda i, j: (0, j)),
                  pl.BlockSpec((tm, tn), lambda i, j: (i, ga_blk + j)),
                  pl.BlockSpec((tm, tn), lambda i, j: (i, gb_blk + j))],
        out_specs=pl.BlockSpec((tm, tn), lambda i, j: (i, j)),
        out_shape=jax.ShapeDtypeStruct((n, d), BF16),
        compiler_params=_params(("parallel", "arbitrary"),
                                2 * (tm * (ka + kb) * 2 + (ka + kb) * tn * 2 + 3 * tm * tn * 2)
                                + 6 * tm * tn * 4),
        name="branch_merge",
    )(ya, yb, w_a, w_b, gates, gates)


def _resid_kernel(m_ref, w_ref, x_ref, g_ref, o_ref, *, alpha):
    acc = jnp.dot(m_ref[...], w_ref[...], preferred_element_type=F32)
    o_ref[...] = alpha * x_ref[...] + g_ref[0] * acc


def _out_proj(merged, w_o, x2, mod3, t_seq, i_gate, alpha):
    n, d = x2.shape
    k = merged.shape[1]
    tm = _tile(1024, t_seq)
    tn = _tile(1024, d)
    per = t_seq // tm
    return pl.pallas_call(
        functools.partial(_resid_kernel, alpha=alpha),
        grid=(n // tm, d // tn),
        in_specs=[pl.BlockSpec((tm, k), lambda i, j: (i, 0)),
                  pl.BlockSpec((k, tn), lambda i, j: (0, j)),
                  pl.BlockSpec((tm, tn), lambda i, j: (i, j)),
                  pl.BlockSpec((1, 1, tn), lambda i, j: ((i // per) * N_MOD + i_gate, 0, j))],
        out_specs=pl.BlockSpec((tm, tn), lambda i, j: (i, j)),
        out_shape=jax.ShapeDtypeStruct((n, d), F32),
        compiler_params=_params(("parallel", "arbitrary"),
                                2 * (tm * k * 2 + k * tn * 2 + 2 * tm * tn * 4) + 2 * tm * tn * 4),
        name="out_proj_residual",
    )(merged, w_o, x2, mod3)


def _ln_pair_kernel(z_ref, g_ref, b_ref, sc_ref, sh_ref, x1_ref, h_ref):
    x1 = _ln(z_ref[...]) * g_ref[...] + b_ref[...]
    x1_ref[...] = x1
    h_ref[...] = (_ln(x1) * (1.0 + sc_ref[0]) + sh_ref[0]).astype(h_ref.dtype)


def _ln_pair(z, ln_g, ln_b, mod3, t_seq, i_scale, i_shift):
    n, d = z.shape
    tm = _tile(256, t_seq)
    per = t_seq // tm
    return pl.pallas_call(
        _ln_pair_kernel,
        grid=(n // tm,),
        in_specs=[pl.BlockSpec((tm, d), lambda i: (i, 0)),
                  pl.BlockSpec((1, d), lambda i: (0, 0)),
                  pl.BlockSpec((1, d), lambda i: (0, 0)),
                  pl.BlockSpec((1, 1, d), lambda i: ((i // per) * N_MOD + i_scale, 0, 0)),
                  pl.BlockSpec((1, 1, d), lambda i: ((i // per) * N_MOD + i_shift, 0, 0))],
        out_specs=[pl.BlockSpec((tm, d), lambda i: (i, 0)),
                   pl.BlockSpec((tm, d), lambda i: (i, 0))],
        out_shape=[jax.ShapeDtypeStruct((n, d), F32), jax.ShapeDtypeStruct((n, d), BF16)],
        compiler_params=_params(("parallel",), 2 * tm * d * 10 + 6 * tm * d * 4),
        name="ln1_ln2mod",
    )(z, ln_g.reshape(1, d), ln_b.reshape(1, d), mod3, mod3)


def _ffn_up_kernel(h_ref, wg_ref, wu_ref, o_ref):
    h = h_ref[...]
    g = jnp.dot(h, wg_ref[...], preferred_element_type=F32)
    u = jnp.dot(h, wu_ref[...], preferred_element_type=F32)
    o_ref[...] = (g * jax.nn.sigmoid(g) * u).astype(o_ref.dtype)


def _ffn_up(h, w_g, w_u):
    n, d = h.shape
    ff = w_g.shape[1]
    tm = _tile(1024, n)
    tn = min(512, ff)
    return pl.pallas_call(
        _ffn_up_kernel,
        grid=(n // tm, pl.cdiv(ff, tn)),
        in_specs=[pl.BlockSpec((tm, d), lambda i, j: (i, 0)),
                  pl.BlockSpec((d, tn), lambda i, j: (0, j)),
                  pl.BlockSpec((d, tn), lambda i, j: (0, j))],
        out_specs=pl.BlockSpec((tm, tn), lambda i, j: (i, j)),
        out_shape=jax.ShapeDtypeStruct((n, ff), BF16),
        compiler_params=_params(("parallel", "arbitrary"),
                                2 * (tm * d * 2 + d * 2 * tn * 2 + tm * tn * 2) + 3 * tm * 2 * tn * 4),
        name="ffn_up",
    )(h, w_g, w_u)


def _ffn_down_kernel(a_ref, w_ref, x_ref, gate_ref, g_ref, b_ref, o_ref, *, alpha, nk, last, rows):
    k = pl.program_id(1)
    tk = a_ref.shape[1]

    @pl.when(k == 0)
    def _():
        o_ref[...] = jnp.zeros(o_ref.shape, F32)

    if last == tk:
        o_ref[...] += jnp.dot(a_ref[...], w_ref[...], preferred_element_type=F32)
    else:
        @pl.when(k < nk - 1)
        def _():
            o_ref[...] += jnp.dot(a_ref[...], w_ref[...], preferred_element_type=F32)

        @pl.when(k == nk - 1)
        def _():
            o_ref[...] += jnp.dot(a_ref[:, :last], w_ref[:last, :], preferred_element_type=F32)

    @pl.when(k == nk - 1)
    def _():
        def chunk(r, carry):
            sl = pl.ds(pl.multiple_of(r * rows, rows), rows)
            z = alpha * x_ref[sl, :] + gate_ref[0] * o_ref[sl, :]
            o_ref[sl, :] = _ln(z) * g_ref[...] + b_ref[...]
            return carry

        lax.fori_loop(0, o_ref.shape[0] // rows, chunk, 0)


def _ffn_down(a, w_d, x1, ln_g, ln_b, mod3, t_seq, i_gate, alpha):
    n, ff = a.shape
    d = w_d.shape[1]
    tm = _tile(512, t_seq)
    tk = min(1024, ff)
    nk = pl.cdiv(ff, tk)
    last = ff - (nk - 1) * tk
    assert last % LANE == 0, "the partial contraction block must stay lane-aligned"
    per = t_seq // tm
    return pl.pallas_call(
        functools.partial(_ffn_down_kernel, alpha=alpha, nk=nk, last=last, rows=_tile(LANE, tm)),
        grid=(n // tm, nk),
        in_specs=[pl.BlockSpec((tm, tk), lambda i, k: (i, k)),
                  pl.BlockSpec((tk, d), lambda i, k: (k, 0)),
                  pl.BlockSpec((tm, d), lambda i, k: (i, 0)),
                  pl.BlockSpec((1, 1, d), lambda i, k: ((i // per) * N_MOD + i_gate, 0, 0)),
                  pl.BlockSpec((1, d), lambda i, k: (0, 0)),
                  pl.BlockSpec((1, d), lambda i, k: (0, 0))],
        out_specs=pl.BlockSpec((tm, d), lambda i, k: (i, 0)),
        out_shape=jax.ShapeDtypeStruct((n, d), F32),
        compiler_params=_params(("parallel", "arbitrary"),
                                2 * (tm * tk * 2 + tk * d * 2 + 2 * tm * d * 4) + 2 * tm * d * 4),
        name="ffn_down_ln",
    )(a, w_d, x1, mod3, ln_g.reshape(1, d), ln_b.reshape(1, d))


def _rope_tables(t_max):
    rows = t_max // GRID_W
    row = jnp.repeat(jnp.arange(rows, dtype=F32), GRID_W)
    col = jnp.tile(jnp.arange(GRID_W, dtype=F32), rows)
    inv = 1.0 / (ROPE_THETA ** (jnp.arange(0, HALF_ROT, 2, dtype=F32) / HALF_ROT))
    ang_r = row[:, None] * inv[None, :]
    ang_c = col[:, None] * inv[None, :]
    cr, sr, cc, sc = jnp.cos(ang_r), jnp.sin(ang_r), jnp.cos(ang_c), jnp.sin(ang_c)
    cos_t = jnp.concatenate([cr, cc, cr, cc], axis=-1)
    sin_t = jnp.concatenate([-sr, -sc, sr, sc], axis=-1)
    return cos_t, sin_t


def _rot_layout(w):
    lead = w.shape[:-1]
    w = w.reshape(lead + (-1, 2, 2, HALF_ROT // 2))
    return jnp.swapaxes(w, -3, -2).reshape(lead + (-1,))


def _prep_layer(w_in, q_norm, k_norm):
    d = w_in.shape[0]
    n_heads = d // HEAD_DIM
    ha = n_heads // 2
    qa_w = ha * HEAD_DIM
    kv_w = (ha // KV_GROUP) * HEAD_DIM
    o_ka = qa_w
    o_va = o_ka + kv_w
    o_qb = o_va + kv_w
    o_kb = o_qb + qa_w
    o_vb = o_kb + kv_w
    o_ga = o_vb + kv_w
    w_qk = _rot_layout(_cast_cols(w_in, 0, o_va))
    w_v_t = jnp.concatenate([_cast_cols(w_in, o_va, kv_w), _cast_cols(w_in, o_vb, kv_w)], axis=1).T
    w_qkb = _cast_cols(w_in, o_qb, o_vb - o_qb)
    w_gates = _cast_cols(w_in, o_ga, w_in.shape[1] - o_ga)
    gain = jnp.concatenate([jnp.tile(_rot_layout(q_norm), ha),
                            jnp.tile(_rot_layout(k_norm), ha // KV_GROUP)]).reshape(1, -1)
    q_scale = jnp.full((qa_w,), LOG2_E / math.sqrt(HEAD_DIM), F32)
    scale = jnp.concatenate([q_scale, jnp.ones((kv_w,), F32)]).reshape(1, -1)
    return dict(w_qk=w_qk, w_v_t=w_v_t, w_qkb=w_qkb, w_gates=w_gates, gain=gain, scale=scale,
                ha=ha, kv_w=kv_w, qa_w=qa_w)


def _encoder_layer(x, mod, p, late_w, sink_row, win_bias, ln1_g, ln1_b, ln2_g, ln2_b, cos_t, sin_t, alpha):
    b, t, d = x.shape
    x2 = x.reshape(b * t, d)
    mod3 = mod.reshape(b * N_MOD, 1, d)
    ha, kv_w, qa_w = p["ha"], p["kv_w"], p["qa_w"]

    h = _ln_mod(x2, mod3, t, 1, 0)
    qa, ka = _qk_proj(h, p["w_qk"], p["gain"], p["scale"], cos_t, sin_t, t, ha)
    va_t, vb_t = _proj_t(h, p["w_v_t"], kv_w // HEAD_DIM)
    qkb = _proj(h, p["w_qkb"], p["scale"])
    gates = _proj(h, p["w_gates"], None, tn_pref=1024)
    if late_w[0].dtype == BF16:
        ya, _ = _global_attention(qa, ka, va_t, b, t, ha)
    else:
        ya, late_w = _global_attention(qa, ka, va_t, b, t, ha, side_casts=late_w)
    w_a, w_b, w_o, w_g, w_u, w_d = late_w
    yb = _window_attention(qkb, vb_t, win_bias, sink_row, b, t, ha, k_col_block=qa_w // kv_w)
    merged = _merge(ya, yb, w_a, w_b, gates)
    z = _out_proj(merged, w_o, x2, mod3, t, 2, alpha)
    x1, h2 = _ln_pair(z, ln1_g, ln1_b, mod3, t, 4, 3)
    a = _ffn_up(h2, w_g, w_u)
    y = _ffn_down(a, w_d, x1, ln2_g, ln2_b, mod3, t, 5, alpha)
    return y.reshape(b, t, d), late_w


def kernel(x_prompt, x_sample, c_prompt, c_sample, w_ada, b_ada, w_in, q_norm_a, k_norm_a, sink_b,
           w_br_a, w_br_b, w_o, ln1_g, ln1_b, w_ffn_gate, w_ffn_up, w_ffn_down, ln2_g, ln2_b):
    depth = w_ada.shape[0]
    alpha = float((2.0 * depth) ** 0.25)
    d = x_prompt.shape[-1]
    bp, bs = c_prompt.shape[0], c_sample.shape[0]
    rows = -(-(bp + bs) // 8) * 8
    cos_t, sin_t = _rope_tables(max(x_prompt.shape[1], x_sample.shape[1]))
    y_p, y_s = x_prompt, x_sample
    c_all = jnp.concatenate([c_prompt, c_sample, jnp.zeros((rows - bp - bs, d), F32)], axis=0)
    win_bias = _window_bias(sink_b.shape[1])
    for l in range(depth):
        mod = _ada(c_all, w_ada[l], b_ada[l]).reshape(rows, N_MOD, d)
        p = _prep_layer(w_in[l], q_norm_a[l], k_norm_a[l])
        late_w = (w_br_a[l], w_br_b[l], w_o[l], w_ffn_gate[l], w_ffn_up[l], w_ffn_down[l])
        sink_row = jnp.repeat(sink_b[l], HEAD_DIM).reshape(1, -1)
        args = (sink_row, win_bias, ln1_g[l], ln1_b[l], ln2_g[l], ln2_b[l], cos_t, sin_t, alpha)
        y_p, late_w = _encoder_layer(y_p, mod[:bp], p, late_w, *args)
        y_s, _ = _encoder_layer(y_s, mod[bp:bp + bs], p, late_w, *args)
    return (y_p, y_s)
```

```python
import functools
import math

import jax
import jax.numpy as jnp
from jax import lax
from jax.experimental import pallas as pl
from jax.experimental.pallas import tpu as pltpu

F32 = jnp.float32
BF16 = jnp.bfloat16

HEAD_DIM = 128
GRID_W = 64
Q_BLOCK = 128
WINDOW = 128
HALF_ROT = HEAD_DIM // 2
ROPE_THETA = 10000.0
N_MOD = 6
LN_EPS = 1e-5
RMS_EPS = 1e-6
KV_GROUP = 4
MASK_VALUE = -1e30
LOG2_E = 1.4426950408889634

ONES_ROWS = 16
V_ROWS = HEAD_DIM + ONES_ROWS
LANE = 128
VMEM_CAP_BYTES = 60 * 1024 * 1024
CAST_BLOCK_BYTES = 4 * 1024 * 1024


def _tile(pref, *dims):
    t = (min(pref, *dims) // LANE) * LANE
    while t >= LANE:
        if all(d % t == 0 for d in dims):
            return t
        t -= LANE
    return min(dims)


def _params(semantics, vmem_bytes):
    limit = int(min(max(vmem_bytes, 16 * 1024 * 1024), VMEM_CAP_BYTES))
    return pltpu.CompilerParams(dimension_semantics=semantics, vmem_limit_bytes=limit)


def _ln(x):
    mu = jnp.mean(x, axis=-1, keepdims=True)
    xc = x - mu
    return xc * lax.rsqrt(jnp.mean(xc * xc, axis=-1, keepdims=True) + LN_EPS)


def _ada_kernel(c_ref, w_ref, b_ref, o_ref):
    c = c_ref[...]
    a = (c * jax.nn.sigmoid(c)).astype(BF16)
    o_ref[...] = jnp.dot(a, w_ref[...].astype(BF16), preferred_element_type=F32) + b_ref[...]


def _ada(c_pad, w_ada, b_ada):
    rows, d = c_pad.shape
    n = w_ada.shape[1]
    tn = _tile(512, n)
    return pl.pallas_call(
        _ada_kernel,
        grid=(n // tn,),
        in_specs=[pl.BlockSpec((rows, d), lambda j: (0, 0)),
                  pl.BlockSpec((d, tn), lambda j: (0, j)),
                  pl.BlockSpec((1, tn), lambda j: (0, j))],
        out_specs=pl.BlockSpec((rows, tn), lambda j: (0, j)),
        out_shape=jax.ShapeDtypeStruct((rows, n), F32),
        compiler_params=_params(("parallel",), 2 * d * tn * 4 + d * tn * 2 + (4 << 20)),
        name="ada_mod",
    )(c_pad, w_ada, b_ada.reshape(1, n))


def _ln_mod_kernel(x_ref, sc_ref, sh_ref, o_ref):
    y = _ln(x_ref[...])
    o_ref[...] = (y * (1.0 + sc_ref[0]) + sh_ref[0]).astype(o_ref.dtype)


def _ln_mod(x2, mod3, t_seq, i_scale, i_shift):
    n, d = x2.shape
    tm = _tile(512, t_seq)
    per = t_seq // tm
    return pl.pallas_call(
        _ln_mod_kernel,
        grid=(n // tm,),
        in_specs=[pl.BlockSpec((tm, d), lambda i: (i, 0)),
                  pl.BlockSpec((1, 1, d), lambda i: ((i // per) * N_MOD + i_scale, 0, 0)),
                  pl.BlockSpec((1, 1, d), lambda i: ((i // per) * N_MOD + i_shift, 0, 0))],
        out_specs=pl.BlockSpec((tm, d), lambda i: (i, 0)),
        out_shape=jax.ShapeDtypeStruct((n, d), BF16),
        compiler_params=_params(("parallel",), 2 * tm * d * 6 + 4 * tm * d * 4),
        name="ln_mod",
    )(x2, mod3, mod3)


def _qk_kernel(h_ref, w_ref, g_ref, s_ref, cos_ref, sin_ref, q_ref, k_ref, *, chunk):
    h = h_ref[...]
    cos = cos_ref[...]
    sin = sin_ref[...]
    n_q = q_ref.shape[1] // HEAD_DIM
    for c in range(w_ref.shape[1] // chunk):
        acc = jnp.dot(h, w_ref[:, c * chunk:(c + 1) * chunk], preferred_element_type=F32)
        for hh in range(chunk // HEAD_DIM):
            head = c * (chunk // HEAD_DIM) + hh
            sl = slice(head * HEAD_DIM, (head + 1) * HEAD_DIM)
            x = acc[:, hh * HEAD_DIM:(hh + 1) * HEAD_DIM]
            y = x * lax.rsqrt(jnp.mean(x * x, axis=-1, keepdims=True) + RMS_EPS) * g_ref[:, sl]
            partner = pltpu.roll(y, HALF_ROT, 1)
            out = ((y * cos + partner * sin) * s_ref[:, sl]).astype(q_ref.dtype)
            if head < n_q:
                q_ref[:, sl] = out
            else:
                k_ref[head - n_q] = out


def _qk_proj(h, w_qk, gain, scale, cos_t, sin_t, t_seq, n_q_heads):
    n, d = h.shape
    nw = w_qk.shape[1]
    qw = n_q_heads * HEAD_DIM
    n_kv = (nw - qw) // HEAD_DIM
    tm = _tile(512, t_seq)
    per = t_seq // tm
    return pl.pallas_call(
        functools.partial(_qk_kernel, chunk=_tile(256, nw)),
        grid=(n // tm,),
        in_specs=[pl.BlockSpec((tm, d), lambda i: (i, 0)),
                  pl.BlockSpec((d, nw), lambda i: (0, 0), pipeline_mode=pl.Buffered(1)),
                  pl.BlockSpec((1, nw), lambda i: (0, 0)),
                  pl.BlockSpec((1, nw), lambda i: (0, 0)),
                  pl.BlockSpec((tm, HEAD_DIM), lambda i: (i % per, 0)),
                  pl.BlockSpec((tm, HEAD_DIM), lambda i: (i % per, 0))],
        out_specs=[pl.BlockSpec((tm, qw), lambda i: (i, 0)),
                   pl.BlockSpec((n_kv, tm, HEAD_DIM), lambda i: (0, i, 0))],
        out_shape=[jax.ShapeDtypeStruct((n, qw), BF16),
                   jax.ShapeDtypeStruct((n_kv, n, HEAD_DIM), BF16)],
        compiler_params=_params(("parallel",),
                                d * nw * 2 + 2 * (tm * d * 2 + tm * nw * 2 + 2 * tm * HEAD_DIM * 4)
                                + 16 * tm * 256 * 4),
        name="qk_proj_rope",
    )(h, w_qk, gain, scale, cos_t, sin_t)


def _cast_kernel(w_ref, o_ref):
    o_ref[...] = w_ref[...].astype(o_ref.dtype)


def _cast_cols(w, col0=0, ncols=None):
    rows, width = w.shape
    ncols = width if ncols is None else ncols
    if col0 == 0 and ncols == width:
        tc = width
    else:
        tc = _tile(2048, ncols, *((col0,) if col0 else ()))
    tr = max(8, min(rows, (CAST_BLOCK_BYTES // (4 * tc)) // 8 * 8))
    while rows % tr:
        tr -= 8
    c0 = col0 // tc
    return pl.pallas_call(
        _cast_kernel,
        grid=(rows // tr, ncols // tc),
        in_specs=[pl.BlockSpec((tr, tc), lambda i, j: (i, c0 + j))],
        out_specs=pl.BlockSpec((tr, tc), lambda i, j: (i, j)),
        out_shape=jax.ShapeDtypeStruct((rows, ncols), BF16),
        compiler_params=_params(("parallel", "parallel"), 2 * tr * tc * 6 + tr * tc * 4),
        name="cast_bf16",
    )(w)


def _mm_scaled_kernel(x_ref, w_ref, s_ref, o_ref):
    acc = jnp.dot(x_ref[...], w_ref[...], preferred_element_type=F32)
    o_ref[...] = (acc * s_ref[...]).astype(o_ref.dtype)


def _mm_kernel(x_ref, w_ref, o_ref):
    o_ref[...] = jnp.dot(x_ref[...], w_ref[...], preferred_element_type=F32).astype(o_ref.dtype)


def _proj(x, w, col_scale, tn_pref=512):
    n, d = x.shape
    nw = w.shape[1]
    tm = _tile(1024, n)
    tn = _tile(tn_pref, nw)
    scaled = col_scale is not None
    in_specs = [pl.BlockSpec((tm, d), lambda i, j: (i, 0)),
                pl.BlockSpec((d, tn), lambda i, j: (0, j))]
    if scaled:
        in_specs.append(pl.BlockSpec((1, tn), lambda i, j: (0, j)))
    return pl.pallas_call(
        _mm_scaled_kernel if scaled else _mm_kernel,
        grid=(n // tm, nw // tn),
        in_specs=in_specs,
        out_specs=pl.BlockSpec((tm, tn), lambda i, j: (i, j)),
        out_shape=jax.ShapeDtypeStruct((n, nw), BF16),
        compiler_params=_params(("parallel", "arbitrary"),
                                2 * (tm * d * 2 + d * tn * 2 + tm * tn * 2) + 2 * tm * tn * 4),
        name="proj",
    )(*((x, w, col_scale) if scaled else (x, w)))


def _mm_t_kernel(wt_ref, x_ref, oa_ref, ob_ref):
    res = lax.dot_general(wt_ref[...], x_ref[...], (((1,), (1,)), ((), ())), preferred_element_type=F32)
    nb = ob_ref.shape[1]
    tm = x_ref.shape[0]
    for j in range(oa_ref.shape[0] // V_ROWS):
        oa_ref[j * V_ROWS:j * V_ROWS + HEAD_DIM, :] = res[j * HEAD_DIM:(j + 1) * HEAD_DIM, :].astype(oa_ref.dtype)
        oa_ref[j * V_ROWS + HEAD_DIM:(j + 1) * V_ROWS, :] = jnp.ones((ONES_ROWS, tm), oa_ref.dtype)
    for j in range(ob_ref.shape[0]):
        ob_ref[j] = res[res.shape[0] - nb:, j * Q_BLOCK:(j + 1) * Q_BLOCK].astype(ob_ref.dtype)


def _proj_t(x, w_t, n_a_heads):
    n, d = x.shape
    nw = w_t.shape[0]
    nb = nw - n_a_heads * HEAD_DIM
    tm = _tile(1024, n)
    return pl.pallas_call(
        _mm_t_kernel,
        grid=(n // tm,),
        in_specs=[pl.BlockSpec((nw, d), lambda i: (0, 0)),
                  pl.BlockSpec((tm, d), lambda i: (i, 0))],
        out_specs=[pl.BlockSpec((n_a_heads * V_ROWS, tm), lambda i: (0, i)),
                   pl.BlockSpec((tm // Q_BLOCK, nb, Q_BLOCK), lambda i: (i, 0, 0))],
        out_shape=[jax.ShapeDtypeStruct((n_a_heads * V_ROWS, n), BF16),
                   jax.ShapeDtypeStruct((n // Q_BLOCK, nb, Q_BLOCK), BF16)],
        compiler_params=_params(("parallel",),
                                2 * (tm * d * 2 + nw * d * 2 + 2 * nw * tm * 2) + 2 * nw * tm * 4),
        name="proj_t",
    )(w_t, x)


def _flash_kernel(q_ref, k_ref, vt_ref, *refs, tq, tk, nk, n_side):
    side_in = refs[:n_side]
    o_ref = refs[n_side]
    side_out = refs[n_side + 1:2 * n_side + 1]
    qs_ref, sa_ref, sb_ref, m_ref, acc_ref = refs[2 * n_side + 1:]
    for w_ref, wo_ref in zip(side_in, side_out):
        wo_ref[...] = w_ref[...].astype(wo_ref.dtype)
    for g in range(KV_GROUP):
        qs_ref[:, g * tq:(g + 1) * tq] = q_ref[:, g * HEAD_DIM:(g + 1) * HEAD_DIM].T
    m_ref[...] = jnp.full(m_ref.shape, -jnp.inf, F32)
    acc_ref[...] = jnp.zeros(acc_ref.shape, F32)

    def scores(idx, s_ref):
        start = pl.multiple_of(idx * tk, tk)
        s_ref[...] = jnp.dot(k_ref[0, pl.ds(start, tk), :], qs_ref[...],
                             preferred_element_type=F32)

    def update(idx, s_ref):
        start = pl.multiple_of(idx * tk, tk)
        s = s_ref[...]
        m_prev = m_ref[...]
        m_new = jnp.maximum(m_prev, jnp.max(s, axis=0, keepdims=True))
        alpha = jnp.exp2(m_prev - m_new)
        p = jnp.exp2(s - m_new).astype(BF16)
        acc_ref[...] = alpha * acc_ref[...] + jnp.dot(vt_ref[:, pl.ds(start, tk)], p,
                                                      preferred_element_type=F32)
        m_ref[...] = m_new

    scores(0, sa_ref)
    pairs = (nk - 1) // 2

    def body(j, carry):
        scores(2 * j + 1, sb_ref)
        update(2 * j, sa_ref)
        scores(2 * j + 2, sa_ref)
        update(2 * j + 1, sb_ref)
        return carry

    lax.fori_loop(0, pairs, body, 0)
    if nk - 2 * pairs == 2:
        scores(nk - 1, sb_ref)
        update(nk - 2, sa_ref)
        update(nk - 1, sb_ref)
    else:
        update(nk - 1, sa_ref)
    out = (acc_ref[:HEAD_DIM, :] / acc_ref[HEAD_DIM:HEAD_DIM + 1, :]).T
    for g in range(KV_GROUP):
        o_ref[:, g * HEAD_DIM:(g + 1) * HEAD_DIM] = out[g * tq:(g + 1) * tq, :].astype(o_ref.dtype)


def _side_rows(rows, steps):
    need = -(-rows // steps)
    for br in range(-(-need // ONES_ROWS) * ONES_ROWS, rows, ONES_ROWS):
        if rows % br == 0:
            return br
    return rows


def _global_attention(q, k_hm, v_t, batch, t_seq, n_q_heads, side_casts=()):
    n = q.shape[0]
    n_kv = n_q_heads // KV_GROUP
    tq = _tile(512 if side_casts else 1024, t_seq)
    tk = _tile(512, t_seq)
    nq = t_seq // tq
    nk = t_seq // tk
    gw = KV_GROUP * HEAD_DIM
    kern = functools.partial(_flash_kernel, tq=tq, tk=tk, nk=nk, n_side=len(side_casts))
    rows = KV_GROUP * tq
    steps = batch * n_kv * nq
    side_specs, side_shapes, side_bytes = [], [], 0
    for w in side_casts:
        br = _side_rows(w.shape[0], steps)
        last = pl.cdiv(w.shape[0], br) - 1
        index = functools.partial(lambda b, h, i, last: (jnp.minimum((b * n_kv + h) * nq + i, last), 0), last=last)
        side_specs.append(pl.BlockSpec((br, w.shape[1]), index))
        side_shapes.append(jax.ShapeDtypeStruct(w.shape, BF16))
        side_bytes += 2 * br * w.shape[1] * 6
    outs = pl.pallas_call(
        kern,
        grid=(batch, n_kv, nq),
        in_specs=[pl.BlockSpec((tq, gw), lambda b, h, i: (b * nq + i, h)),
                  pl.BlockSpec((1, t_seq, HEAD_DIM), lambda b, h, i: (h, b, 0)),
                  pl.BlockSpec((V_ROWS, t_seq), lambda b, h, i: (h, b))] + side_specs,
        out_specs=[pl.BlockSpec((tq, gw), lambda b, h, i: (b * nq + i, h))] + side_specs,
        out_shape=[jax.ShapeDtypeStruct((n, n_q_heads * HEAD_DIM), BF16)] + side_shapes,
        scratch_shapes=[pltpu.VMEM((HEAD_DIM, rows), BF16),
                        pltpu.VMEM((tk, rows), F32),
                        pltpu.VMEM((tk, rows), F32),
                        pltpu.VMEM((1, rows), F32),
                        pltpu.VMEM((V_ROWS, rows), F32)],
        compiler_params=_params(("arbitrary", "arbitrary", "arbitrary"),
                                2 * (2 * tq * gw * 2 + 2 * t_seq * HEAD_DIM * 2) + side_bytes
                                + rows * HEAD_DIM * 6 + 16 * rows * 4 + 8 * rows * tk * 4),
        name="global_attention",
    )(q, k_hm, v_t, *side_casts)
    return outs[0], tuple(outs[1:])


def _window_bias(n_heads):
    key = jnp.arange(3 * Q_BLOCK, dtype=jnp.int32)[:, None] - Q_BLOCK
    qpos = jnp.arange(Q_BLOCK, dtype=jnp.int32)[None, :]
    dist = jnp.abs(qpos - key)
    slopes = 2.0 ** (-8.0 * jnp.arange(1, n_heads + 1, dtype=F32) / n_heads)
    bias = -slopes[:, None, None] * dist.astype(F32)[None] * LOG2_E
    n_kv = n_heads // KV_GROUP
    bias = bias.reshape(n_kv, KV_GROUP, 3 * Q_BLOCK, Q_BLOCK).transpose(0, 2, 1, 3)
    bias = bias.reshape(n_kv, 3 * Q_BLOCK, KV_GROUP * Q_BLOCK)
    in_window = (dist <= WINDOW)
    in_window = jnp.tile(in_window, (1, KV_GROUP))[None]
    row = jnp.arange(3 * Q_BLOCK)[None, :, None]
    not_prev = row >= Q_BLOCK
    not_next = row < 2 * Q_BLOCK
    variants = [in_window, in_window & not_prev, in_window & not_next, in_window & not_prev & not_next]
    return jnp.stack([jnp.where(v, bias, MASK_VALUE) for v in variants])


def _window_kernel(q_ref, kp_ref, kc_ref, kn_ref, vp_ref, vc_ref, vn_ref, *refs, n_kv, n_sub):
    bias_refs = refs[:n_sub]
    sink_ref, o_ref = refs[n_sub:]
    gw = KV_GROUP * HEAD_DIM
    for kv in range(n_kv):
        ks = slice(kv * HEAD_DIM, (kv + 1) * HEAD_DIM)
        k_all = jnp.concatenate([kp_ref[:, ks], kc_ref[:, ks], kn_ref[:, ks]], axis=0)
        vt_all = jnp.concatenate([vp_ref[0, ks, :]] + [vc_ref[j, ks, :] for j in range(n_sub)]
                                 + [vn_ref[0, ks, :]], axis=1)
        sink = sink_ref[:, kv * gw:(kv + 1) * gw] * LOG2_E
        for sub in range(n_sub):
            rows = slice(sub * Q_BLOCK, (sub + 1) * Q_BLOCK)
            keys = slice(sub * Q_BLOCK, (sub + 3) * Q_BLOCK)
            qs = jnp.concatenate([q_ref[rows, (kv * KV_GROUP + g) * HEAD_DIM:(kv * KV_GROUP + g + 1) * HEAD_DIM]
                                  for g in range(KV_GROUP)], axis=0)
            s = lax.dot_general(k_all[keys], qs, (((1,), (1,)), ((), ())), preferred_element_type=F32)
            s = s + bias_refs[sub][0, kv]
            m = jnp.maximum(jnp.max(s, axis=0, keepdims=True), sink)
            p = jnp.exp2(s - m)
            den = jnp.sum(p, axis=0, keepdims=True) + jnp.exp2(sink - m)
            pv = jnp.dot(vt_all[:, keys], p.astype(BF16), preferred_element_type=F32)
            out = (pv / den).T
            for g in range(KV_GROUP):
                hd = kv * KV_GROUP + g
                o_ref[rows, hd * HEAD_DIM:(hd + 1) * HEAD_DIM] = (
                    out[g * Q_BLOCK:(g + 1) * Q_BLOCK, :].astype(o_ref.dtype))


def _window_attention(rest, v_t, bias, sink_row, batch, t_seq, n_heads, k_col_block):
    n = rest.shape[0]
    nb = t_seq // Q_BLOCK
    n_sub = max(s for s in (4, 2, 1) if nb % s == 0)
    steps = nb // n_sub
    n_kv = n_heads // KV_GROUP
    qw = n_heads * HEAD_DIM
    kw = n_kv * HEAD_DIM
    span = n_sub * Q_BLOCK

    def prev(b, i):
        return b * nb + jnp.maximum(i * n_sub - 1, 0)

    def cur(b, i):
        return b * steps + i

    def nxt(b, i):
        return b * nb + jnp.minimum((i + 1) * n_sub, nb - 1)

    def variant(b, i, sub):
        first = (i == 0).astype(jnp.int32) if sub == 0 else 0
        last = 2 * (i == steps - 1).astype(jnp.int32) if sub == n_sub - 1 else 0
        return first + last

    k_specs = [pl.BlockSpec((Q_BLOCK, kw), lambda b, i: (prev(b, i), k_col_block)),
               pl.BlockSpec((span, kw), lambda b, i: (cur(b, i), k_col_block)),
               pl.BlockSpec((Q_BLOCK, kw), lambda b, i: (nxt(b, i), k_col_block))]
    v_specs = [pl.BlockSpec((1, kw, Q_BLOCK), lambda b, i: (prev(b, i), 0, 0)),
               pl.BlockSpec((n_sub, kw, Q_BLOCK), lambda b, i: (cur(b, i), 0, 0)),
               pl.BlockSpec((1, kw, Q_BLOCK), lambda b, i: (nxt(b, i), 0, 0))]
    bias_specs = [pl.BlockSpec((1,) + bias.shape[1:],
                               functools.partial(lambda b, i, sub: (variant(b, i, sub), 0, 0, 0), sub=sub))
                  for sub in range(n_sub)]
    return pl.pallas_call(
        functools.partial(_window_kernel, n_kv=n_kv, n_sub=n_sub),
        grid=(batch, steps),
        in_specs=[pl.BlockSpec((span, qw), lambda b, i: (cur(b, i), 0))] + k_specs + v_specs + bias_specs
        + [pl.BlockSpec((1, qw), lambda b, i: (0, 0))],
        out_specs=pl.BlockSpec((span, qw), lambda b, i: (cur(b, i), 0)),
        out_shape=jax.ShapeDtypeStruct((n, qw), BF16),
        compiler_params=_params(("parallel", "arbitrary"),
                                2 * (n_sub * bias[0].size * 4 + 2 * span * qw * 2 + 4 * (span + 2 * Q_BLOCK) * kw)
                                + 16 * 3 * Q_BLOCK * KV_GROUP * Q_BLOCK * 4),
        name="window_attention",
    )(rest, rest, rest, rest, v_t, v_t, v_t, *([bias] * n_sub), sink_row)


def _merge_kernel(ya_ref, yb_ref, wa_ref, wb_ref, ga_ref, gb_ref, o_ref):
    a = jnp.dot(ya_ref[...], wa_ref[...], preferred_element_type=F32)
    b = jnp.dot(yb_ref[...], wb_ref[...], preferred_element_type=F32)
    ga = jax.nn.sigmoid(ga_ref[...].astype(F32))
    gb = jax.nn.sigmoid(gb_ref[...].astype(F32))
    o_ref[...] = (ga * a + gb * b).astype(o_ref.dtype)


def _merge(ya, yb, w_a, w_b, gates):
    n, ka = ya.shape
    kb = yb.shape[1]
    d = w_a.shape[1]
    tm = _tile(1024, n)
    tn = _tile(512, d)
    ga_blk = 0
    gb_blk = d // tn
    return pl.pallas_call(
        _merge_kernel,
        grid=(n // tm, d // tn),
        in_specs=[pl.BlockSpec((tm, ka), lambda i, j: (i, 0)),
                  pl.BlockSpec((tm, kb), lambda i, j: (i, 0)),
                  pl.BlockSpec((ka, tn), lambda i, j: (0, j)),
                  pl.BlockSpec((kb, tn), lambda i, j: (0, j)),
                  pl.BlockSpec((tm, tn), lambda i, j: (i, ga_blk + j)),
                  pl.BlockSpec((tm, tn), lambda i, j: (i, gb_blk + j))],
        out_specs=pl.BlockSpec((tm, tn), lambda i, j: (i, j)),
        out_shape=jax.ShapeDtypeStruct((n, d), BF16),
        compiler_params=_params(("parallel", "arbitrary"),
                                2 * (tm * (ka + kb) * 2 + (ka + kb) * tn * 2 + 3 * tm * tn * 2)
                                + 6 * tm * tn * 4),
        name="branch_merge",
    )(ya, yb, w_a, w_b, gates, gates)


def _resid_kernel(m_ref, w_ref, x_ref, g_ref, o_ref, *, alpha):
    acc = jnp.dot(m_ref[...], w_ref[...], preferred_element_type=F32)
    o_ref[...] = alpha * x_ref[...] + g_ref[0] * acc


def _out_proj(merged, w_o, x2, mod3, t_seq, i_gate, alpha):
    n, d = x2.shape
    k = merged.shape[1]
    tm = _tile(1024, t_seq)
    tn = _tile(1024, d)
    per = t_seq // tm
    return pl.pallas_call(
        functools.partial(_resid_kernel, alpha=alpha),
        grid=(n // tm, d // tn),
        in_specs=[pl.BlockSpec((tm, k), lambda i, j: (i, 0)),
                  pl.BlockSpec((k, tn), lambda i, j: (0, j)),
                  pl.BlockSpec((tm, tn), lambda i, j: (i, j)),
                  pl.BlockSpec((1, 1, tn), lambda i, j: ((i // per) * N_MOD + i_gate, 0, j))],
        out_specs=pl.BlockSpec((tm, tn), lambda i, j: (i, j)),
        out_shape=jax.ShapeDtypeStruct((n, d), F32),
        compiler_params=_params(("parallel", "arbitrary"),
                                2 * (tm * k * 2 + k * tn * 2 + 2 * tm * tn * 4) + 2 * tm * tn * 4),
        name="out_proj_residual",
    )(merged, w_o, x2, mod3)


def _ln_pair_kernel(z_ref, g_ref, b_ref, sc_ref, sh_ref, x1_ref, h_ref):
    x1 = _ln(z_ref[...]) * g_ref[...] + b_ref[...]
    x1_ref[...] = x1
    h_ref[...] = (_ln(x1) * (1.0 + sc_ref[0]) + sh_ref[0]).astype(h_ref.dtype)


def _ln_pair(z, ln_g, ln_b, mod3, t_seq, i_scale, i_shift):
    n, d = z.shape
    tm = _tile(256, t_seq)
    per = t_seq // tm
    return pl.pallas_call(
        _ln_pair_kernel,
        grid=(n // tm,),
        in_specs=[pl.BlockSpec((tm, d), lambda i: (i, 0)),
                  pl.BlockSpec((1, d), lambda i: (0, 0)),
                  pl.BlockSpec((1, d), lambda i: (0, 0)),
                  pl.BlockSpec((1, 1, d), lambda i: ((i // per) * N_MOD + i_scale, 0, 0)),
                  pl.BlockSpec((1, 1, d), lambda i: ((i // per) * N_MOD + i_shift, 0, 0))],
        out_specs=[pl.BlockSpec((tm, d), lambda i: (i, 0)),
                   pl.BlockSpec((tm, d), lambda i: (i, 0))],
        out_shape=[jax.ShapeDtypeStruct((n, d), F32), jax.ShapeDtypeStruct((n, d), BF16)],
        compiler_params=_params(("parallel",), 2 * tm * d * 10 + 6 * tm * d * 4),
        name="ln1_ln2mod",
    )(z, ln_g.reshape(1, d), ln_b.reshape(1, d), mod3, mod3)


def _ffn_up_kernel(h_ref, wg_ref, wu_ref, o_ref):
    h = h_ref[...]
    g = jnp.dot(h, wg_ref[...], preferred_element_type=F32)
    u = jnp.dot(h, wu_ref[...], preferred_element_type=F32)
    o_ref[...] = (g * jax.nn.sigmoid(g) * u).astype(o_ref.dtype)


def _ffn_up(h, w_g, w_u):
    n, d = h.shape
    ff = w_g.shape[1]
    tm = _tile(1024, n)
    tn = min(512, ff)
    return pl.pallas_call(
        _ffn_up_kernel,
        grid=(n // tm, pl.cdiv(ff, tn)),
        in_specs=[pl.BlockSpec((tm, d), lambda i, j: (i, 0)),
                  pl.BlockSpec((d, tn), lambda i, j: (0, j)),
                  pl.BlockSpec((d, tn), lambda i, j: (0, j))],
        out_specs=pl.BlockSpec((tm, tn), lambda i, j: (i, j)),
        out_shape=jax.ShapeDtypeStruct((n, ff), BF16),
        compiler_params=_params(("parallel", "arbitrary"),
                                2 * (tm * d * 2 + d * 2 * tn * 2 + tm * tn * 2) + 3 * tm * 2 * tn * 4),
        name="ffn_up",
    )(h, w_g, w_u)


def _ffn_down_kernel(a_ref, w_ref, x_ref, gate_ref, g_ref, b_ref, o_ref, *, alpha, nk, last, rows):
    k = pl.program_id(1)
    tk = a_ref.shape[1]

    @pl.when(k == 0)
    def _():
        o_ref[...] = jnp.zeros(o_ref.shape, F32)

    if last == tk:
        o_ref[...] += jnp.dot(a_ref[...], w_ref[...], preferred_element_type=F32)
    else:
        @pl.when(k < nk - 1)
        def _():
            o_ref[...] += jnp.dot(a_ref[...], w_ref[...], preferred_element_type=F32)

        @pl.when(k == nk - 1)
        def _():
            o_ref[...] += jnp.dot(a_ref[:, :last], w_ref[:last, :], preferred_element_type=F32)

    @pl.when(k == nk - 1)
    def _():
        def chunk(r, carry):
            sl = pl.ds(pl.multiple_of(r * rows, rows), rows)
            z = alpha * x_ref[sl, :] + gate_ref[0] * o_ref[sl, :]
            o_ref[sl, :] = _ln(z) * g_ref[...] + b_ref[...]
            return carry

        lax.fori_loop(0, o_ref.shape[0] // rows, chunk, 0)


def _ffn_down(a, w_d, x1, ln_g, ln_b, mod3, t_seq, i_gate, alpha):
    n, ff = a.shape
    d = w_d.shape[1]
    tm = _tile(512, t_seq)
    tk = min(1024, ff)
    nk = pl.cdiv(ff, tk)
    last = ff - (nk - 1) * tk
    assert last % LANE == 0, "the partial contraction block must stay lane-aligned"
    per = t_seq // tm
    return pl.pallas_call(
        functools.partial(_ffn_down_kernel, alpha=alpha, nk=nk, last=last, rows=_tile(LANE, tm)),
        grid=(n // tm, nk),
        in_specs=[pl.BlockSpec((tm, tk), lambda i, k: (i, k)),
                  pl.BlockSpec((tk, d), lambda i, k: (k, 0)),
                  pl.BlockSpec((tm, d), lambda i, k: (i, 0)),
                  pl.BlockSpec((1, 1, d), lambda i, k: ((i // per) * N_MOD + i_gate, 0, 0)),
                  pl.BlockSpec((1, d), lambda i, k: (0, 0)),
                  pl.BlockSpec((1, d), lambda i, k: (0, 0))],
        out_specs=pl.BlockSpec((tm, d), lambda i, k: (i, 0)),
        out_shape=jax.ShapeDtypeStruct((n, d), F32),
        compiler_params=_params(("parallel", "arbitrary"),
                                2 * (tm * tk * 2 + tk * d * 2 + 2 * tm * d * 4) + 2 * tm * d * 4),
        name="ffn_down_ln",
    )(a, w_d, x1, mod3, ln_g.reshape(1, d), ln_b.reshape(1, d))


def _rope_tables(t_max):
    rows = t_max // GRID_W
    row = jnp.repeat(jnp.arange(rows, dtype=F32), GRID_W)
    col = jnp.tile(jnp.arange(GRID_W, dtype=F32), rows)
    inv = 1.0 / (ROPE_THETA ** (jnp.arange(0, HALF_ROT, 2, dtype=F32) / HALF_ROT))
    ang_r = row[:, None] * inv[None, :]
    ang_c = col[:, None] * inv[None, :]
    cr, sr, cc, sc = jnp.cos(ang_r), jnp.sin(ang_r), jnp.cos(ang_c), jnp.sin(ang_c)
    cos_t = jnp.concatenate([cr, cc, cr, cc], axis=-1)
    sin_t = jnp.concatenate([-sr, -sc, sr, sc], axis=-1)
    return cos_t, sin_t


def _rot_layout(w):
    lead = w.shape[:-1]
    w = w.reshape(lead + (-1, 2, 2, HALF_ROT // 2))
    return jnp.swapaxes(w, -3, -2).reshape(lead + (-1,))


def _prep_layer(w_in, q_norm, k_norm):
    d = w_in.shape[0]
    n_heads = d // HEAD_DIM
    ha = n_heads // 2
    qa_w = ha * HEAD_DIM
    kv_w = (ha // KV_GROUP) * HEAD_DIM
    o_ka = qa_w
    o_va = o_ka + kv_w
    o_qb = o_va + kv_w
    o_kb = o_qb + qa_w
    o_vb = o_kb + kv_w
    o_ga = o_vb + kv_w
    w_qk = _rot_layout(_cast_cols(w_in, 0, o_va))
    w_v_t = jnp.concatenate([_cast_cols(w_in, o_va, kv_w), _cast_cols(w_in, o_vb, kv_w)], axis=1).T
    w_qkb = _cast_cols(w_in, o_qb, o_vb - o_qb)
    w_gates = _cast_cols(w_in, o_ga, w_in.shape[1] - o_ga)
    gain = jnp.concatenate([jnp.tile(_rot_layout(q_norm), ha),
                            jnp.tile(_rot_layout(k_norm), ha // KV_GROUP)]).reshape(1, -1)
    q_scale = jnp.full((qa_w,), LOG2_E / math.sqrt(HEAD_DIM), F32)
    scale = jnp.concatenate([q_scale, jnp.ones((kv_w,), F32)]).reshape(1, -1)
    return dict(w_qk=w_qk, w_v_t=w_v_t, w_qkb=w_qkb, w_gates=w_gates, gain=gain, scale=scale,
                ha=ha, kv_w=kv_w, qa_w=qa_w)


def _encoder_layer(x, mod, p, late_w, sink_row, win_bias, ln1_g, ln1_b, ln2_g, ln2_b, cos_t, sin_t, alpha):
    b, t, d = x.shape
    x2 = x.reshape(b * t, d)
    mod3 = mod.reshape(b * N_MOD, 1, d)
    ha, kv_w, qa_w = p["ha"], p["kv_w"], p["qa_w"]

    h = _ln_mod(x2, mod3, t, 1, 0)
    qa, ka = _qk_proj(h, p["w_qk"], p["gain"], p["scale"], cos_t, sin_t, t, ha)
    va_t, vb_t = _proj_t(h, p["w_v_t"], kv_w // HEAD_DIM)
    qkb = _proj(h, p["w_qkb"], p["scale"])
    gates = _proj(h, p["w_gates"], None, tn_pref=1024)
    if late_w[0].dtype == BF16:
        ya, _ = _global_attention(qa, ka, va_t, b, t, ha)
    else:
        ya, late_w = _global_attention(qa, ka, va_t, b, t, ha, side_casts=late_w)
    w_a, w_b, w_o, w_g, w_u, w_d = late_w
    yb = _window_attention(qkb, vb_t, win_bias, sink_row, b, t, ha, k_col_block=qa_w // kv_w)
    merged = _merge(ya, yb, w_a, w_b, gates)
    z = _out_proj(merged, w_o, x2, mod3, t, 2, alpha)
    x1, h2 = _ln_pair(z, ln1_g, ln1_b, mod3, t, 4, 3)
    a = _ffn_up(h2, w_g, w_u)
    y = _ffn_down(a, w_d, x1, ln2_g, ln2_b, mod3, t, 5, alpha)
    return y.reshape(b, t, d), late_w


def kernel(x_prompt, x_sample, c_prompt, c_sample, w_ada, b_ada, w_in, q_norm_a, k_norm_a, sink_b,
           w_br_a, w_br_b, w_o, ln1_g, ln1_b, w_ffn_gate, w_ffn_up, w_ffn_down, ln2_g, ln2_b):
    depth = w_ada.shape[0]
    alpha = float((2.0 * depth) ** 0.25)
    d = x_prompt.shape[-1]
    bp, bs = c_prompt.shape[0], c_sample.shape[0]
    rows = -(-(bp + bs) // 8) * 8
    cos_t, sin_t = _rope_tables(max(x_prompt.shape[1], x_sample.shape[1]))
    y_p, y_s = x_prompt, x_sample
    c_all = jnp.concatenate([c_prompt, c_sample, jnp.zeros((rows - bp - bs, d), F32)], axis=0)
    win_bias = _window_bias(sink_b.shape[1])
    for l in range(depth):
        mod = _ada(c_all, w_ada[l], b_ada[l]).reshape(rows, N_MOD, d)
        p = _prep_layer(w_in[l], q_norm_a[l], k_norm_a[l])
        late_w = (w_br_a[l], w_br_b[l], w_o[l], w_ffn_gate[l], w_ffn_up[l], w_ffn_down[l])
        sink_row = jnp.repeat(sink_b[l], HEAD_DIM).reshape(1, -1)
        args = (sink_row, win_bias, ln1_g[l], ln1_b[l], ln2_g[l], ln2_b[l], cos_t, sin_t, alpha)
        y_p, late_w = _encoder_layer(y_p, mod[:bp], p, late_w, *args)
        y_s, _ = _encoder_layer(y_s, mod[bp:bp + bs], p, late_w, *args)
    return (y_p, y_s)
```

```python
import functools
import math

import jax
import jax.numpy as jnp
from jax import lax
from jax.experimental import pallas as pl
from jax.experimental.pallas import tpu as pltpu

F32 = jnp.float32
BF16 = jnp.bfloat16

HEAD_DIM = 128
GRID_W = 64
Q_BLOCK = 128
WINDOW = 128
HALF_ROT = HEAD_DIM // 2
ROPE_THETA = 10000.0
N_MOD = 6
LN_EPS = 1e-5
RMS_EPS = 1e-6
KV_GROUP = 4
MASK_VALUE = -1e30
LOG2_E = 1.4426950408889634

ONES_ROWS = 16
V_ROWS = HEAD_DIM + ONES_ROWS
LANE = 128
VMEM_CAP_BYTES = 60 * 1024 * 1024
CAST_BLOCK_BYTES = 4 * 1024 * 1024


def _tile(pref, *dims):
    t = (min(pref, *dims) // LANE) * LANE
    while t >= LANE:
        if all(d % t == 0 for d in dims):
            return t
        t -= LANE
    return min(dims)


def _params(semantics, vmem_bytes):
    limit = int(min(max(vmem_bytes, 16 * 1024 * 1024), VMEM_CAP_BYTES))
    return pltpu.CompilerParams(dimension_semantics=semantics, vmem_limit_bytes=limit)


def _ln(x):
    mu = jnp.mean(x, axis=-1, keepdims=True)
    xc = x - mu
    return xc * lax.rsqrt(jnp.mean(xc * xc, axis=-1, keepdims=True) + LN_EPS)


def _ada_kernel(c_ref, w_ref, b_ref, o_ref):
    c = c_ref[...]
    a = (c * jax.nn.sigmoid(c)).astype(BF16)
    o_ref[...] = jnp.dot(a, w_ref[...].astype(BF16), preferred_element_type=F32) + b_ref[...]


def _ada(c_pad, w_ada, b_ada):
    rows, d = c_pad.shape
    n = w_ada.shape[1]
    tn = _tile(512, n)
    return pl.pallas_call(
        _ada_kernel,
        grid=(n // tn,),
        in_specs=[pl.BlockSpec((rows, d), lambda j: (0, 0)),
                  pl.BlockSpec((d, tn), lambda j: (0, j)),
                  pl.BlockSpec((1, tn), lambda j: (0, j))],
        out_specs=pl.BlockSpec((rows, tn), lambda j: (0, j)),
        out_shape=jax.ShapeDtypeStruct((rows, n), F32),
        compiler_params=_params(("parallel",), 2 * d * tn * 4 + d * tn * 2 + (4 << 20)),
        name="ada_mod",
    )(c_pad, w_ada, b_ada.reshape(1, n))


def _ln_mod_kernel(x_ref, sc_ref, sh_ref, o_ref):
    y = _ln(x_ref[...])
    o_ref[...] = (y * (1.0 + sc_ref[0]) + sh_ref[0]).astype(o_ref.dtype)


def _ln_mod(x2, mod3, t_seq, i_scale, i_shift):
    n, d = x2.shape
    tm = _tile(512, t_seq)
    per = t_seq // tm
    return pl.pallas_call(
        _ln_mod_kernel,
        grid=(n // tm,),
        in_specs=[pl.BlockSpec((tm, d), lambda i: (i, 0)),
                  pl.BlockSpec((1, 1, d), lambda i: ((i // per) * N_MOD + i_scale, 0, 0)),
                  pl.BlockSpec((1, 1, d), lambda i: ((i // per) * N_MOD + i_shift, 0, 0))],
        out_specs=pl.BlockSpec((tm, d), lambda i: (i, 0)),
        out_shape=jax.ShapeDtypeStruct((n, d), BF16),
        compiler_params=_params(("parallel",), 2 * tm * d * 6 + 4 * tm * d * 4),
        name="ln_mod",
    )(x2, mod3, mod3)


def _qk_kernel(h_ref, w_ref, g_ref, s_ref, cos_ref, sin_ref, q_ref, k_ref, *, chunk):
    h = h_ref[...]
    cos = cos_ref[...]
    sin = sin_ref[...]
    n_q = q_ref.shape[1] // HEAD_DIM
    for c in range(w_ref.shape[1] // chunk):
        acc = jnp.dot(h, w_ref[:, c * chunk:(c + 1) * chunk], preferred_element_type=F32)
        for hh in range(chunk // HEAD_DIM):
            head = c * (chunk // HEAD_DIM) + hh
            sl = slice(head * HEAD_DIM, (head + 1) * HEAD_DIM)
            x = acc[:, hh * HEAD_DIM:(hh + 1) * HEAD_DIM]
            y = x * lax.rsqrt(jnp.mean(x * x, axis=-1, keepdims=True) + RMS_EPS) * g_ref[:, sl]
            partner = pltpu.roll(y, HALF_ROT, 1)
            out = ((y * cos + partner * sin) * s_ref[:, sl]).astype(q_ref.dtype)
            if head < n_q:
                q_ref[:, sl] = out
            else:
                k_ref[head - n_q] = out


def _qk_proj(h, w_qk, gain, scale, cos_t, sin_t, t_seq, n_q_heads):
    n, d = h.shape
    nw = w_qk.shape[1]
    qw = n_q_heads * HEAD_DIM
    n_kv = (nw - qw) // HEAD_DIM
    tm = _tile(512, t_seq)
    per = t_seq // tm
    return pl.pallas_call(
        functools.partial(_qk_kernel, chunk=_tile(256, nw)),
        grid=(n // tm,),
        in_specs=[pl.BlockSpec((tm, d), lambda i: (i, 0)),
                  pl.BlockSpec((d, nw), lambda i: (0, 0), pipeline_mode=pl.Buffered(1)),
                  pl.BlockSpec((1, nw), lambda i: (0, 0)),
                  pl.BlockSpec((1, nw), lambda i: (0, 0)),
                  pl.BlockSpec((tm, HEAD_DIM), lambda i: (i % per, 0)),
                  pl.BlockSpec((tm, HEAD_DIM), lambda i: (i % per, 0))],
        out_specs=[pl.BlockSpec((tm, qw), lambda i: (i, 0)),
                   pl.BlockSpec((n_kv, tm, HEAD_DIM), lambda i: (0, i, 0))],
        out_shape=[jax.ShapeDtypeStruct((n, qw), BF16),
                   jax.ShapeDtypeStruct((n_kv, n, HEAD_DIM), BF16)],
        compiler_params=_params(("parallel",),
                                d * nw * 2 + 2 * (tm * d * 2 + tm * nw * 2 + 2 * tm * HEAD_DIM * 4)
                                + 16 * tm * 256 * 4),
        name="qk_proj_rope",
    )(h, w_qk, gain, scale, cos_t, sin_t)


def _cast_kernel(w_ref, o_ref):
    o_ref[...] = w_ref[...].astype(o_ref.dtype)


def _cast_cols(w, col0=0, ncols=None):
    rows, width = w.shape
    ncols = width if ncols is None else ncols
    if col0 == 0 and ncols == width:
        tc = width
    else:
        tc = _tile(2048, ncols, *((col0,) if col0 else ()))
    tr = max(8, min(rows, (CAST_BLOCK_BYTES // (4 * tc)) // 8 * 8))
    while rows % tr:
        tr -= 8
    c0 = col0 // tc
    return pl.pallas_call(
        _cast_kernel,
        grid=(rows // tr, ncols // tc),
        in_specs=[pl.BlockSpec((tr, tc), lambda i, j: (i, c0 + j))],
        out_specs=pl.BlockSpec((tr, tc), lambda i, j: (i, j)),
        out_shape=jax.ShapeDtypeStruct((rows, ncols), BF16),
        compiler_params=_params(("parallel", "parallel"), 2 * tr * tc * 6 + tr * tc * 4),
        name="cast_bf16",
    )(w)


def _mm_scaled_kernel(x_ref, w_ref, s_ref, o_ref):
    acc = jnp.dot(x_ref[...], w_ref[...], preferred_element_type=F32)
    o_ref[...] = (acc * s_ref[...]).astype(o_ref.dtype)


def _mm_kernel(x_ref, w_ref, o_ref):
    o_ref[...] = jnp.dot(x_ref[...], w_ref[...], preferred_element_type=F32).astype(o_ref.dtype)


def _proj(x, w, col_scale, tn_pref=512):
    n, d = x.shape
    nw = w.shape[1]
    tm = _tile(1024, n)
    tn = _tile(tn_pref, nw)
    scaled = col_scale is not None
    in_specs = [pl.BlockSpec((tm, d), lambda i, j: (i, 0)),
                pl.BlockSpec((d, tn), lambda i, j: (0, j))]
    if scaled:
        in_specs.append(pl.BlockSpec((1, tn), lambda i, j: (0, j)))
    return pl.pallas_call(
        _mm_scaled_kernel if scaled else _mm_kernel,
        grid=(n // tm, nw // tn),
        in_specs=in_specs,
        out_specs=pl.BlockSpec((tm, tn), lambda i, j: (i, j)),
        out_shape=jax.ShapeDtypeStruct((n, nw), BF16),
        compiler_params=_params(("parallel", "arbitrary"),
                                2 * (tm * d * 2 + d * tn * 2 + tm * tn * 2) + 2 * tm * tn * 4),
        name="proj",
    )(*((x, w, col_scale) if scaled else (x, w)))


def _mm_t_kernel(wt_ref, x_ref, oa_ref, ob_ref):
    res = lax.dot_general(wt_ref[...], x_ref[...], (((1,), (1,)), ((), ())), preferred_element_type=F32)
    nb = ob_ref.shape[1]
    tm = x_ref.shape[0]
    for j in range(oa_ref.shape[0] // V_ROWS):
        oa_ref[j * V_ROWS:j * V_ROWS + HEAD_DIM, :] = res[j * HEAD_DIM:(j + 1) * HEAD_DIM, :].astype(oa_ref.dtype)
        oa_ref[j * V_ROWS + HEAD_DIM:(j + 1) * V_ROWS, :] = jnp.ones((ONES_ROWS, tm), oa_ref.dtype)
    for j in range(ob_ref.shape[0]):
        ob_ref[j] = res[res.shape[0] - nb:, j * Q_BLOCK:(j + 1) * Q_BLOCK].astype(ob_ref.dtype)


def _proj_t(x, w_t, n_a_heads):
    n, d = x.shape
    nw = w_t.shape[0]
    nb = nw - n_a_heads * HEAD_DIM
    tm = _tile(1024, n)
    return pl.pallas_call(
        _mm_t_kernel,
        grid=(n // tm,),
        in_specs=[pl.BlockSpec((nw, d), lambda i: (0, 0)),
                  pl.BlockSpec((tm, d), lambda i: (i, 0))],
        out_specs=[pl.BlockSpec((n_a_heads * V_ROWS, tm), lambda i: (0, i)),
                   pl.BlockSpec((tm // Q_BLOCK, nb, Q_BLOCK), lambda i: (i, 0, 0))],
        out_shape=[jax.ShapeDtypeStruct((n_a_heads * V_ROWS, n), BF16),
                   jax.ShapeDtypeStruct((n // Q_BLOCK, nb, Q_BLOCK), BF16)],
        compiler_params=_params(("parallel",),
                                2 * (tm * d * 2 + nw * d * 2 + 2 * nw * tm * 2) + 2 * nw * tm * 4),
        name="proj_t",
    )(w_t, x)


def _flash_kernel(q_ref, k_ref, vt_ref, *refs, tq, tk, nk, n_side):
    side_in = refs[:n_side]
    o_ref = refs[n_side]
    side_out = refs[n_side + 1:2 * n_side + 1]
    qs_ref, sa_ref, sb_ref, m_ref, acc_ref = refs[2 * n_side + 1:]
    for w_ref, wo_ref in zip(side_in, side_out):
        wo_ref[...] = w_ref[...].astype(wo_ref.dtype)
    for g in range(KV_GROUP):
        qs_ref[g * tq:(g + 1) * tq, :] = q_ref[:, g * HEAD_DIM:(g + 1) * HEAD_DIM]
    m_ref[...] = jnp.full(m_ref.shape, -jnp.inf, F32)
    acc_ref[...] = jnp.zeros(acc_ref.shape, F32)

    def scores(idx, s_ref):
        start = pl.multiple_of(idx * tk, tk)
        s_ref[...] = lax.dot_general(k_ref[0, pl.ds(start, tk), :], qs_ref[...], (((1,), (1,)), ((), ())),
                                     preferred_element_type=F32)

    def update(idx, s_ref):
        start = pl.multiple_of(idx * tk, tk)
        s = s_ref[...]
        m_prev = m_ref[...]
        m_new = jnp.maximum(m_prev, jnp.max(s, axis=0, keepdims=True))
        alpha = jnp.exp2(m_prev - m_new)
        p = jnp.exp2(s - m_new).astype(BF16)
        acc_ref[...] = alpha * acc_ref[...] + jnp.dot(vt_ref[:, pl.ds(start, tk)], p,
                                                      preferred_element_type=F32)
        m_ref[...] = m_new

    scores(0, sa_ref)
    pairs = (nk - 1) // 2

    def body(j, carry):
        scores(2 * j + 1, sb_ref)
        update(2 * j, sa_ref)
        scores(2 * j + 2, sa_ref)
        update(2 * j + 1, sb_ref)
        return carry

    lax.fori_loop(0, pairs, body, 0)
    if nk - 2 * pairs == 2:
        scores(nk - 1, sb_ref)
        update(nk - 2, sa_ref)
        update(nk - 1, sb_ref)
    else:
        update(nk - 1, sa_ref)
    out = (acc_ref[:HEAD_DIM, :] / acc_ref[HEAD_DIM:HEAD_DIM + 1, :]).T
    for g in range(KV_GROUP):
        o_ref[:, g * HEAD_DIM:(g + 1) * HEAD_DIM] = out[g * tq:(g + 1) * tq, :].astype(o_ref.dtype)


def _side_rows(rows, steps):
    need = -(-rows // steps)
    for br in range(-(-need // ONES_ROWS) * ONES_ROWS, rows, ONES_ROWS):
        if rows % br == 0:
            return br
    return rows


def _global_attention(q, k_hm, v_t, batch, t_seq, n_q_heads, side_casts=()):
    n = q.shape[0]
    n_kv = n_q_heads // KV_GROUP
    tq = _tile(512 if side_casts else 1024, t_seq)
    tk = _tile(512, t_seq)
    nq = t_seq // tq
    nk = t_seq // tk
    gw = KV_GROUP * HEAD_DIM
    kern = functools.partial(_flash_kernel, tq=tq, tk=tk, nk=nk, n_side=len(side_casts))
    rows = KV_GROUP * tq
    steps = batch * n_kv * nq
    side_specs, side_shapes, side_bytes = [], [], 0
    for w in side_casts:
        br = _side_rows(w.shape[0], steps)
        last = pl.cdiv(w.shape[0], br) - 1
        index = functools.partial(lambda b, h, i, last: (jnp.minimum((b * n_kv + h) * nq + i, last), 0), last=last)
        side_specs.append(pl.BlockSpec((br, w.shape[1]), index))
        side_shapes.append(jax.ShapeDtypeStruct(w.shape, BF16))
        side_bytes += 2 * br * w.shape[1] * 6
    outs = pl.pallas_call(
        kern,
        grid=(batch, n_kv, nq),
        in_specs=[pl.BlockSpec((tq, gw), lambda b, h, i: (b * nq + i, h)),
                  pl.BlockSpec((1, t_seq, HEAD_DIM), lambda b, h, i: (h, b, 0)),
                  pl.BlockSpec((V_ROWS, t_seq), lambda b, h, i: (h, b))] + side_specs,
        out_specs=[pl.BlockSpec((tq, gw), lambda b, h, i: (b * nq + i, h))] + side_specs,
        out_shape=[jax.ShapeDtypeStruct((n, n_q_heads * HEAD_DIM), BF16)] + side_shapes,
        scratch_shapes=[pltpu.VMEM((rows, HEAD_DIM), BF16),
                        pltpu.VMEM((tk, rows), F32),
                        pltpu.VMEM((tk, rows), F32),
                        pltpu.VMEM((1, rows), F32),
                        pltpu.VMEM((V_ROWS, rows), F32)],
        compiler_params=_params(("arbitrary", "arbitrary", "arbitrary"),
                                2 * (2 * tq * gw * 2 + 2 * t_seq * HEAD_DIM * 2) + side_bytes
                                + rows * HEAD_DIM * 6 + 16 * rows * 4 + 8 * rows * tk * 4),
        name="global_attention",
    )(q, k_hm, v_t, *side_casts)
    return outs[0], tuple(outs[1:])


def _window_bias(n_heads):
    key = jnp.arange(3 * Q_BLOCK, dtype=jnp.int32)[:, None] - Q_BLOCK
    qpos = jnp.arange(Q_BLOCK, dtype=jnp.int32)[None, :]
    dist = jnp.abs(qpos - key)
    slopes = 2.0 ** (-8.0 * jnp.arange(1, n_heads + 1, dtype=F32) / n_heads)
    bias = -slopes[:, None, None] * dist.astype(F32)[None] * LOG2_E
    n_kv = n_heads // KV_GROUP
    bias = bias.reshape(n_kv, KV_GROUP, 3 * Q_BLOCK, Q_BLOCK).transpose(0, 2, 1, 3)
    bias = bias.reshape(n_kv, 3 * Q_BLOCK, KV_GROUP * Q_BLOCK)
    in_window = (dist <= WINDOW)
    in_window = jnp.tile(in_window, (1, KV_GROUP))[None]
    row = jnp.arange(3 * Q_BLOCK)[None, :, None]
    not_prev = row >= Q_BLOCK
    not_next = row < 2 * Q_BLOCK
    variants = [in_window, in_window & not_prev, in_window & not_next, in_window & not_prev & not_next]
    return jnp.stack([jnp.where(v, bias, MASK_VALUE) for v in variants])


def _window_kernel(q_ref, kp_ref, kc_ref, kn_ref, vp_ref, vc_ref, vn_ref, *refs, n_kv, n_sub):
    bias_refs = refs[:n_sub]
    sink_ref, o_ref = refs[n_sub:]
    gw = KV_GROUP * HEAD_DIM
    for kv in range(n_kv):
        ks = slice(kv * HEAD_DIM, (kv + 1) * HEAD_DIM)
        k_all = jnp.concatenate([kp_ref[:, ks], kc_ref[:, ks], kn_ref[:, ks]], axis=0)
        vt_all = jnp.concatenate([vp_ref[0, ks, :]] + [vc_ref[j, ks, :] for j in range(n_sub)]
                                 + [vn_ref[0, ks, :]], axis=1)
        sink = sink_ref[:, kv * gw:(kv + 1) * gw] * LOG2_E
        for sub in range(n_sub):
            rows = slice(sub * Q_BLOCK, (sub + 1) * Q_BLOCK)
            keys = slice(sub * Q_BLOCK, (sub + 3) * Q_BLOCK)
            qs = jnp.concatenate([q_ref[rows, (kv * KV_GROUP + g) * HEAD_DIM:(kv * KV_GROUP + g + 1) * HEAD_DIM]
                                  for g in range(KV_GROUP)], axis=0)
            s = lax.dot_general(k_all[keys], qs, (((1,), (1,)), ((), ())), preferred_element_type=F32)
            s = s + bias_refs[sub][0, kv]
            m = jnp.maximum(jnp.max(s, axis=0, keepdims=True), sink)
            p = jnp.exp2(s - m)
            den = jnp.sum(p, axis=0, keepdims=True) + jnp.exp2(sink - m)
            pv = jnp.dot(vt_all[:, keys], p.astype(BF16), preferred_element_type=F32)
            out = (pv / den).T
            for g in range(KV_GROUP):
                hd = kv * KV_GROUP + g
                o_ref[rows, hd * HEAD_DIM:(hd + 1) * HEAD_DIM] = (
                    out[g * Q_BLOCK:(g + 1) * Q_BLOCK, :].astype(o_ref.dtype))


def _window_attention(rest, v_t, bias, sink_row, batch, t_seq, n_heads, k_col_block):
    n = rest.shape[0]
    nb = t_seq // Q_BLOCK
    n_sub = max(s for s in (4, 2, 1) if nb % s == 0)
    steps = nb // n_sub
    n_kv = n_heads // KV_GROUP
    qw = n_heads * HEAD_DIM
    kw = n_kv * HEAD_DIM
    span = n_sub * Q_BLOCK

    def prev(b, i):
        return b * nb + jnp.maximum(i * n_sub - 1, 0)

    def cur(b, i):
        return b * steps + i

    def nxt(b, i):
        return b * nb + jnp.minimum((i + 1) * n_sub, nb - 1)

    def variant(b, i, sub):
        first = (i == 0).astype(jnp.int32) if sub == 0 else 0
        last = 2 * (i == steps - 1).astype(jnp.int32) if sub == n_sub - 1 else 0
        return first + last

    k_specs = [pl.BlockSpec((Q_BLOCK, kw), lambda b, i: (prev(b, i), k_col_block)),
               pl.BlockSpec((span, kw), lambda b, i: (cur(b, i), k_col_block)),
               pl.BlockSpec((Q_BLOCK, kw), lambda b, i: (nxt(b, i), k_col_block))]
    v_specs = [pl.BlockSpec((1, kw, Q_BLOCK), lambda b, i: (prev(b, i), 0, 0)),
               pl.BlockSpec((n_sub, kw, Q_BLOCK), lambda b, i: (cur(b, i), 0, 0)),
               pl.BlockSpec((1, kw, Q_BLOCK), lambda b, i: (nxt(b, i), 0, 0))]
    bias_specs = [pl.BlockSpec((1,) + bias.shape[1:],
                               functools.partial(lambda b, i, sub: (variant(b, i, sub), 0, 0, 0), sub=sub))
                  for sub in range(n_sub)]
    return pl.pallas_call(
        functools.partial(_window_kernel, n_kv=n_kv, n_sub=n_sub),
        grid=(batch, steps),
        in_specs=[pl.BlockSpec((span, qw), lambda b, i: (cur(b, i), 0))] + k_specs + v_specs + bias_specs
        + [pl.BlockSpec((1, qw), lambda b, i: (0, 0))],
        out_specs=pl.BlockSpec((span, qw), lambda b, i: (cur(b, i), 0)),
        out_shape=jax.ShapeDtypeStruct((n, qw), BF16),
        compiler_params=_params(("parallel", "arbitrary"),
                                2 * (n_sub * bias[0].size * 4 + 2 * span * qw * 2 + 4 * (span + 2 * Q_BLOCK) * kw)
                                + 16 * 3 * Q_BLOCK * KV_GROUP * Q_BLOCK * 4),
        name="window_attention",
    )(rest, rest, rest, rest, v_t, v_t, v_t, *([bias] * n_sub), sink_row)


def _merge_kernel(ya_ref, yb_ref, wa_ref, wb_ref, ga_ref, gb_ref, o_ref):
    a = jnp.dot(ya_ref[...], wa_ref[...], preferred_element_type=F32)
    b = jnp.dot(yb_ref[...], wb_ref[...], preferred_element_type=F32)
    ga = jax.nn.sigmoid(ga_ref[...].astype(F32))
    gb = jax.nn.sigmoid(gb_ref[...].astype(F32))
    o_ref[...] = (ga * a + gb * b).astype(o_ref.dtype)


def _merge(ya, yb, w_a, w_b, gates):
    n, ka = ya.shape
    kb = yb.shape[1]
    d = w_a.shape[1]
    tm = _tile(1024, n)
    tn = _tile(512, d)
    ga_blk = 0
    gb_blk = d // tn
    return pl.pallas_call(
        _merge_kernel,
        grid=(n // tm, d // tn),
        in_specs=[pl.BlockSpec((tm, ka), lambda i, j: (i, 0)),
                  pl.BlockSpec((tm, kb), lambda i, j: (i, 0)),
                  pl.BlockSpec((ka, tn), lambda i, j: (0, j)),
                  pl.BlockSpec((kb, tn), lambda i, j: (0, j)),
                  pl.BlockSpec((tm, tn), lambda i, j: (i, ga_blk + j)),
                  pl.BlockSpec((tm, tn), lambda i, j: (i, gb_blk + j))],
        out_specs=pl.BlockSpec((tm, tn), lambda i, j: (i, j)),
        out_shape=jax.ShapeDtypeStruct((n, d), BF16),
        compiler_params=_params(("parallel", "arbitrary"),
                                2 * (tm * (ka + kb) * 2 + (ka + kb) * tn * 2 + 3 * tm * tn * 2)
                                + 6 * tm * tn * 4),
        name="branch_merge",
    )(ya, yb, w_a, w_b, gates, gates)


def _resid_kernel(m_ref, w_ref, x_ref, g_ref, o_ref, *, alpha):
    acc = jnp.dot(m_ref[...], w_ref[...], preferred_element_type=F32)
    o_ref[...] = alpha * x_ref[...] + g_ref[0] * acc


def _out_proj(merged, w_o, x2, mod3, t_seq, i_gate, alpha):
    n, d = x2.shape
    k = merged.shape[1]
    tm = _tile(1024, t_seq)
    tn = _tile(1024, d)
    per = t_seq // tm
    return pl.pallas_call(
        functools.partial(_resid_kernel, alpha=alpha),
        grid=(n // tm, d // tn),
        in_specs=[pl.BlockSpec((tm, k), lambda i, j: (i, 0)),
                  pl.BlockSpec((k, tn), lambda i, j: (0, j)),
                  pl.BlockSpec((tm, tn), lambda i, j: (i, j)),
                  pl.BlockSpec((1, 1, tn), lambda i, j: ((i // per) * N_MOD + i_gate, 0, j))],
        out_specs=pl.BlockSpec((tm, tn), lambda i, j: (i, j)),
        out_shape=jax.ShapeDtypeStruct((n, d), F32),
        compiler_params=_params(("parallel", "arbitrary"),
                                2 * (tm * k * 2 + k * tn * 2 + 2 * tm * tn * 4) + 2 * tm * tn * 4),
        name="out_proj_residual",
    )(merged, w_o, x2, mod3)


def _ln_pair_kernel(z_ref, g_ref, b_ref, sc_ref, sh_ref, x1_ref, h_ref, *, rows):
    def chunk(r, carry):
        sl = pl.ds(pl.multiple_of(r * rows, rows), rows)
        x1 = _ln(z_ref[sl, :]) * g_ref[...] + b_ref[...]
        x1_ref[sl, :] = x1
        h_ref[sl, :] = (_ln(x1) * (1.0 + sc_ref[0]) + sh_ref[0]).astype(h_ref.dtype)
        return carry

    lax.fori_loop(0, z_ref.shape[0] // rows, chunk, 0)


def _ln_pair(z, ln_g, ln_b, mod3, t_seq, i_scale, i_shift):
    n, d = z.shape
    tm = _tile(512, t_seq)
    per = t_seq // tm
    return pl.pallas_call(
        functools.partial(_ln_pair_kernel, rows=_tile(LANE, tm)),
        grid=(n // tm,),
        in_specs=[pl.BlockSpec((tm, d), lambda i: (i, 0)),
                  pl.BlockSpec((1, d), lambda i: (0, 0)),
                  pl.BlockSpec((1, d), lambda i: (0, 0)),
                  pl.BlockSpec((1, 1, d), lambda i: ((i // per) * N_MOD + i_scale, 0, 0)),
                  pl.BlockSpec((1, 1, d), lambda i: ((i // per) * N_MOD + i_shift, 0, 0))],
        out_specs=[pl.BlockSpec((tm, d), lambda i: (i, 0)),
                   pl.BlockSpec((tm, d), lambda i: (i, 0))],
        out_shape=[jax.ShapeDtypeStruct((n, d), F32), jax.ShapeDtypeStruct((n, d), BF16)],
        compiler_params=_params(("parallel",), 2 * tm * d * 10 + 6 * tm * d * 4),
        name="ln1_ln2mod",
    )(z, ln_g.reshape(1, d), ln_b.reshape(1, d), mod3, mod3)


def _ffn_up_kernel(h_ref, wg_ref, wu_ref, o_ref):
    h = h_ref[...]
    g = jnp.dot(h, wg_ref[...], preferred_element_type=F32)
    u = jnp.dot(h, wu_ref[...], preferred_element_type=F32)
    o_ref[...] = (g * jax.nn.sigmoid(g) * u).astype(o_ref.dtype)


def _ffn_up(h, w_g, w_u):
    n, d = h.shape
    ff = w_g.shape[1]
    tm = _tile(1024, n)
    tn = min(512, ff)
    return pl.pallas_call(
        _ffn_up_kernel,
        grid=(n // tm, pl.cdiv(ff, tn)),
        in_specs=[pl.BlockSpec((tm, d), lambda i, j: (i, 0)),
                  pl.BlockSpec((d, tn), lambda i, j: (0, j)),
                  pl.BlockSpec((d, tn), lambda i, j: (0, j))],
        out_specs=pl.BlockSpec((tm, tn), lambda i, j: (i, j)),
        out_shape=jax.ShapeDtypeStruct((n, ff), BF16),
        compiler_params=_params(("parallel", "arbitrary"),
                                2 * (tm * d * 2 + d * 2 * tn * 2 + tm * tn * 2) + 3 * tm * 2 * tn * 4),
        name="ffn_up",
    )(h, w_g, w_u)


def _ffn_down_kernel(a_ref, w_ref, x_ref, gate_ref, g_ref, b_ref, o_ref, *, alpha, nk, last, rows):
    k = pl.program_id(1)
    tk = a_ref.shape[1]

    @pl.when(k == 0)
    def _():
        o_ref[...] = jnp.zeros(o_ref.shape, F32)

    if last == tk:
        o_ref[...] += jnp.dot(a_ref[...], w_ref[...], preferred_element_type=F32)
    else:
        @pl.when(k < nk - 1)
        def _():
            o_ref[...] += jnp.dot(a_ref[...], w_ref[...], preferred_element_type=F32)

        @pl.when(k == nk - 1)
        def _():
            o_ref[...] += jnp.dot(a_ref[:, :last], w_ref[:last, :], preferred_element_type=F32)

    @pl.when(k == nk - 1)
    def _():
        def chunk(r, carry):
            sl = pl.ds(pl.multiple_of(r * rows, rows), rows)
            z = alpha * x_ref[sl, :] + gate_ref[0] * o_ref[sl, :]
            o_ref[sl, :] = _ln(z) * g_ref[...] + b_ref[...]
            return carry

        lax.fori_loop(0, o_ref.shape[0] // rows, chunk, 0)


def _ffn_down(a, w_d, x1, ln_g, ln_b, mod3, t_seq, i_gate, alpha):
    n, ff = a.shape
    d = w_d.shape[1]
    tm = _tile(512, t_seq)
    tk = min(1024, ff)
    nk = pl.cdiv(ff, tk)
    last = ff - (nk - 1) * tk
    assert last % LANE == 0, "the partial contraction block must stay lane-aligned"
    per = t_seq // tm
    return pl.pallas_call(
        functools.partial(_ffn_down_kernel, alpha=alpha, nk=nk, last=last, rows=_tile(LANE, tm)),
        grid=(n // tm, nk),
        in_specs=[pl.BlockSpec((tm, tk), lambda i, k: (i, k)),
                  pl.BlockSpec((tk, d), lambda i, k: (k, 0)),
                  pl.BlockSpec((tm, d), lambda i, k: (i, 0)),
                  pl.BlockSpec((1, 1, d), lambda i, k: ((i // per) * N_MOD + i_gate, 0, 0)),
                  pl.BlockSpec((1, d), lambda i, k: (0, 0)),
                  pl.BlockSpec((1, d), lambda i, k: (0, 0))],
        out_specs=pl.BlockSpec((tm, d), lambda i, k: (i, 0)),
        out_shape=jax.ShapeDtypeStruct((n, d), F32),
        compiler_params=_params(("parallel", "arbitrary"),
                                2 * (tm * tk * 2 + tk * d * 2 + 2 * tm * d * 4) + 2 * tm * d * 4),
        name="ffn_down_ln",
    )(a, w_d, x1, mod3, ln_g.reshape(1, d), ln_b.reshape(1, d))


def _rope_tables(t_max):
    rows = t_max // GRID_W
    row = jnp.repeat(jnp.arange(rows, dtype=F32), GRID_W)
    col = jnp.tile(jnp.arange(GRID_W, dtype=F32), rows)
    inv = 1.0 / (ROPE_THETA ** (jnp.arange(0, HALF_ROT, 2, dtype=F32) / HALF_ROT))
    ang_r = row[:, None] * inv[None, :]
    ang_c = col[:, None] * inv[None, :]
    cr, sr, cc, sc = jnp.cos(ang_r), jnp.sin(ang_r), jnp.cos(ang_c), jnp.sin(ang_c)
    cos_t = jnp.concatenate([cr, cc, cr, cc], axis=-1)
    sin_t = jnp.concatenate([-sr, -sc, sr, sc], axis=-1)
    return cos_t, sin_t


def _rot_layout(w):
    lead = w.shape[:-1]
    w = w.reshape(lead + (-1, 2, 2, HALF_ROT // 2))
    return jnp.swapaxes(w, -3, -2).reshape(lead + (-1,))


def _prep_layer(w_in, q_norm, k_norm):
    d = w_in.shape[0]
    n_heads = d // HEAD_DIM
    ha = n_heads // 2
    qa_w = ha * HEAD_DIM
    kv_w = (ha // KV_GROUP) * HEAD_DIM
    o_ka = qa_w
    o_va = o_ka + kv_w
    o_qb = o_va + kv_w
    o_kb = o_qb + qa_w
    o_vb = o_kb + kv_w
    o_ga = o_vb + kv_w
    w_qk = _rot_layout(_cast_cols(w_in, 0, o_va))
    w_v_t = jnp.concatenate([_cast_cols(w_in, o_va, kv_w), _cast_cols(w_in, o_vb, kv_w)], axis=1).T
    w_qkb = _cast_cols(w_in, o_qb, o_vb - o_qb)
    w_gates = _cast_cols(w_in, o_ga, w_in.shape[1] - o_ga)
    gain = jnp.concatenate([jnp.tile(_rot_layout(q_norm), ha),
                            jnp.tile(_rot_layout(k_norm), ha // KV_GROUP)]).reshape(1, -1)
    q_scale = jnp.full((qa_w,), LOG2_E / math.sqrt(HEAD_DIM), F32)
    scale = jnp.concatenate([q_scale, jnp.ones((kv_w,), F32)]).reshape(1, -1)
    return dict(w_qk=w_qk, w_v_t=w_v_t, w_qkb=w_qkb, w_gates=w_gates, gain=gain, scale=scale,
                ha=ha, kv_w=kv_w, qa_w=qa_w)


def _encoder_layer(x, mod, p, late_w, sink_row, win_bias, ln1_g, ln1_b, ln2_g, ln2_b, cos_t, sin_t, alpha):
    b, t, d = x.shape
    x2 = x.reshape(b * t, d)
    mod3 = mod.reshape(b * N_MOD, 1, d)
    ha, kv_w, qa_w = p["ha"], p["kv_w"], p["qa_w"]

    h = _ln_mod(x2, mod3, t, 1, 0)
    qa, ka = _qk_proj(h, p["w_qk"], p["gain"], p["scale"], cos_t, sin_t, t, ha)
    va_t, vb_t = _proj_t(h, p["w_v_t"], kv_w // HEAD_DIM)
    qkb = _proj(h, p["w_qkb"], p["scale"], tn_pref=1280)
    gates = _proj(h, p["w_gates"], None, tn_pref=1024)
    if late_w[0].dtype == BF16:
        ya, _ = _global_attention(qa, ka, va_t, b, t, ha)
    else:
        ya, late_w = _global_attention(qa, ka, va_t, b, t, ha, side_casts=late_w)
    w_a, w_b, w_o, w_g, w_u, w_d = late_w
    yb = _window_attention(qkb, vb_t, win_bias, sink_row, b, t, ha, k_col_block=qa_w // kv_w)
    merged = _merge(ya, yb, w_a, w_b, gates)
    z = _out_proj(merged, w_o, x2, mod3, t, 2, alpha)
    x1, h2 = _ln_pair(z, ln1_g, ln1_b, mod3, t, 4, 3)
    a = _ffn_up(h2, w_g, w_u)
    y = _ffn_down(a, w_d, x1, ln2_g, ln2_b, mod3, t, 5, alpha)
    return y.reshape(b, t, d), late_w


def kernel(x_prompt, x_sample, c_prompt, c_sample, w_ada, b_ada, w_in, q_norm_a, k_norm_a, sink_b,
           w_br_a, w_br_b, w_o, ln1_g, ln1_b, w_ffn_gate, w_ffn_up, w_ffn_down, ln2_g, ln2_b):
    depth = w_ada.shape[0]
    alpha = float((2.0 * depth) ** 0.25)
    d = x_prompt.shape[-1]
    bp, bs = c_prompt.shape[0], c_sample.shape[0]
    rows = -(-(bp + bs) // 8) * 8
    cos_t, sin_t = _rope_tables(max(x_prompt.shape[1], x_sample.shape[1]))
    y_p, y_s = x_prompt, x_sample
    c_all = jnp.concatenate([c_prompt, c_sample, jnp.zeros((rows - bp - bs, d), F32)], axis=0)
    win_bias = _window_bias(sink_b.shape[1])
    for l in range(depth):
        mod = _ada(c_all, w_ada[l], b_ada[l]).reshape(rows, N_MOD, d)
        p = _prep_layer(w_in[l], q_norm_a[l], k_norm_a[l])
        late_w = (w_br_a[l], w_br_b[l], w_o[l], w_ffn_gate[l], w_ffn_up[l], w_ffn_down[l])
        sink_row = jnp.repeat(sink_b[l], HEAD_DIM).reshape(1, -1)
        args = (sink_row, win_bias, ln1_g[l], ln1_b[l], ln2_g[l], ln2_b[l], cos_t, sin_t, alpha)
        y_p, late_w = _encoder_layer(y_p, mod[:bp], p, late_w, *args)
        y_s, _ = _encoder_layer(y_s, mod[bp:bp + bs], p, late_w, *args)
    return (y_p, y_s)
```

```python
import functools
import math

import jax
import jax.numpy as jnp
from jax import lax
from jax.experimental import pallas as pl
from jax.experimental.pallas import tpu as pltpu

F32 = jnp.float32
BF16 = jnp.bfloat16

HEAD_DIM = 128
GRID_W = 64
Q_BLOCK = 128
WINDOW = 128
HALF_ROT = HEAD_DIM // 2
ROPE_THETA = 10000.0
N_MOD = 6
LN_EPS = 1e-5
RMS_EPS = 1e-6
KV_GROUP = 4
MASK_VALUE = -1e30
LOG2_E = 1.4426950408889634

ONES_ROWS = 16
V_ROWS = HEAD_DIM + ONES_ROWS
LANE = 128
VMEM_CAP_BYTES = 60 * 1024 * 1024
CAST_BLOCK_BYTES = 4 * 1024 * 1024


def _tile(pref, *dims):
    t = (min(pref, *dims) // LANE) * LANE
    while t >= LANE:
        if all(d % t == 0 for d in dims):
            return t
        t -= LANE
    return min(dims)


def _params(semantics, vmem_bytes):
    limit = int(min(max(vmem_bytes, 16 * 1024 * 1024), VMEM_CAP_BYTES))
    return pltpu.CompilerParams(dimension_semantics=semantics, vmem_limit_bytes=limit)


def _ln(x):
    mu = jnp.mean(x, axis=-1, keepdims=True)
    xc = x - mu
    return xc * lax.rsqrt(jnp.mean(xc * xc, axis=-1, keepdims=True) + LN_EPS)


def _ada_kernel(c_ref, w_ref, b_ref, o_ref):
    c = c_ref[...]
    a = (c * jax.nn.sigmoid(c)).astype(BF16)
    o_ref[...] = jnp.dot(a, w_ref[...].astype(BF16), preferred_element_type=F32) + b_ref[...]


def _ada(c_pad, w_ada, b_ada):
    rows, d = c_pad.shape
    n = w_ada.shape[1]
    tn = _tile(512, n)
    return pl.pallas_call(
        _ada_kernel,
        grid=(n // tn,),
        in_specs=[pl.BlockSpec((rows, d), lambda j: (0, 0)),
                  pl.BlockSpec((d, tn), lambda j: (0, j)),
                  pl.BlockSpec((1, tn), lambda j: (0, j))],
        out_specs=pl.BlockSpec((rows, tn), lambda j: (0, j)),
        out_shape=jax.ShapeDtypeStruct((rows, n), F32),
        compiler_params=_params(("parallel",), 2 * d * tn * 4 + d * tn * 2 + (4 << 20)),
        name="ada_mod",
    )(c_pad, w_ada, b_ada.reshape(1, n))


def _ln_mod_kernel(x_ref, sc_ref, sh_ref, o_ref):
    y = _ln(x_ref[...])
    o_ref[...] = (y * (1.0 + sc_ref[0]) + sh_ref[0]).astype(o_ref.dtype)


def _ln_mod(x2, mod3, t_seq, i_scale, i_shift):
    n, d = x2.shape
    tm = _tile(512, t_seq)
    per = t_seq // tm
    return pl.pallas_call(
        _ln_mod_kernel,
        grid=(n // tm,),
        in_specs=[pl.BlockSpec((tm, d), lambda i: (i, 0)),
                  pl.BlockSpec((1, 1, d), lambda i: ((i // per) * N_MOD + i_scale, 0, 0)),
                  pl.BlockSpec((1, 1, d), lambda i: ((i // per) * N_MOD + i_shift, 0, 0))],
        out_specs=pl.BlockSpec((tm, d), lambda i: (i, 0)),
        out_shape=jax.ShapeDtypeStruct((n, d), BF16),
        compiler_params=_params(("parallel",), 2 * tm * d * 6 + 4 * tm * d * 4),
        name="ln_mod",
    )(x2, mod3, mod3)


def _qk_kernel(h_ref, w_ref, g_ref, s_ref, cos_ref, sin_ref, q_ref, k_ref, *, chunk):
    h = h_ref[...]
    cos = cos_ref[...]
    sin = sin_ref[...]
    n_q = q_ref.shape[1] // HEAD_DIM
    for c in range(w_ref.shape[1] // chunk):
        acc = jnp.dot(h, w_ref[:, c * chunk:(c + 1) * chunk], preferred_element_type=F32)
        for hh in range(chunk // HEAD_DIM):
            head = c * (chunk // HEAD_DIM) + hh
            sl = slice(head * HEAD_DIM, (head + 1) * HEAD_DIM)
            x = acc[:, hh * HEAD_DIM:(hh + 1) * HEAD_DIM]
            y = x * lax.rsqrt(jnp.mean(x * x, axis=-1, keepdims=True) + RMS_EPS) * g_ref[:, sl]
            partner = pltpu.roll(y, HALF_ROT, 1)
            out = ((y * cos + partner * sin) * s_ref[:, sl]).astype(q_ref.dtype)
            if head < n_q:
                q_ref[:, sl] = out
            else:
                k_ref[head - n_q] = out


def _qk_proj(h, w_qk, gain, scale, cos_t, sin_t, t_seq, n_q_heads):
    n, d = h.shape
    nw = w_qk.shape[1]
    qw = n_q_heads * HEAD_DIM
    n_kv = (nw - qw) // HEAD_DIM
    tm = _tile(512, t_seq)
    per = t_seq // tm
    return pl.pallas_call(
        functools.partial(_qk_kernel, chunk=_tile(256, nw)),
        grid=(n // tm,),
        in_specs=[pl.BlockSpec((tm, d), lambda i: (i, 0)),
                  pl.BlockSpec((d, nw), lambda i: (0, 0), pipeline_mode=pl.Buffered(1)),
                  pl.BlockSpec((1, nw), lambda i: (0, 0)),
                  pl.BlockSpec((1, nw), lambda i: (0, 0)),
                  pl.BlockSpec((tm, HEAD_DIM), lambda i: (i % per, 0)),
                  pl.BlockSpec((tm, HEAD_DIM), lambda i: (i % per, 0))],
        out_specs=[pl.BlockSpec((tm, qw), lambda i: (i, 0)),
                   pl.BlockSpec((n_kv, tm, HEAD_DIM), lambda i: (0, i, 0))],
        out_shape=[jax.ShapeDtypeStruct((n, qw), BF16),
                   jax.ShapeDtypeStruct((n_kv, n, HEAD_DIM), BF16)],
        compiler_params=_params(("parallel",),
                                d * nw * 2 + 2 * (tm * d * 2 + tm * nw * 2 + 2 * tm * HEAD_DIM * 4)
                                + 16 * tm * 256 * 4),
        name="qk_proj_rope",
    )(h, w_qk, gain, scale, cos_t, sin_t)


def _cast_kernel(w_ref, o_ref):
    o_ref[...] = w_ref[...].astype(o_ref.dtype)


def _cast_cols(w, col0=0, ncols=None):
    rows, width = w.shape
    ncols = width if ncols is None else ncols
    if col0 == 0 and ncols == width:
        tc = width
    else:
        tc = _tile(2048, ncols, *((col0,) if col0 else ()))
    tr = max(8, min(rows, (CAST_BLOCK_BYTES // (4 * tc)) // 8 * 8))
    while rows % tr:
        tr -= 8
    c0 = col0 // tc
    return pl.pallas_call(
        _cast_kernel,
        grid=(rows // tr, ncols // tc),
        in_specs=[pl.BlockSpec((tr, tc), lambda i, j: (i, c0 + j))],
        out_specs=pl.BlockSpec((tr, tc), lambda i, j: (i, j)),
        out_shape=jax.ShapeDtypeStruct((rows, ncols), BF16),
        compiler_params=_params(("parallel", "parallel"), 2 * tr * tc * 6 + tr * tc * 4),
        name="cast_bf16",
    )(w)


def _mm_scaled_kernel(x_ref, w_ref, s_ref, o_ref):
    acc = jnp.dot(x_ref[...], w_ref[...], preferred_element_type=F32)
    o_ref[...] = (acc * s_ref[...]).astype(o_ref.dtype)


def _mm_kernel(x_ref, w_ref, o_ref):
    o_ref[...] = jnp.dot(x_ref[...], w_ref[...], preferred_element_type=F32).astype(o_ref.dtype)


def _proj(x, w, col_scale, tn_pref=512):
    n, d = x.shape
    nw = w.shape[1]
    tm = _tile(1024, n)
    tn = _tile(tn_pref, nw)
    scaled = col_scale is not None
    in_specs = [pl.BlockSpec((tm, d), lambda i, j: (i, 0)),
                pl.BlockSpec((d, tn), lambda i, j: (0, j))]
    if scaled:
        in_specs.append(pl.BlockSpec((1, tn), lambda i, j: (0, j)))
    return pl.pallas_call(
        _mm_scaled_kernel if scaled else _mm_kernel,
        grid=(n // tm, nw // tn),
        in_specs=in_specs,
        out_specs=pl.BlockSpec((tm, tn), lambda i, j: (i, j)),
        out_shape=jax.ShapeDtypeStruct((n, nw), BF16),
        compiler_params=_params(("parallel", "arbitrary"),
                                2 * (tm * d * 2 + d * tn * 2 + tm * tn * 2) + 2 * tm * tn * 4),
        name="proj",
    )(*((x, w, col_scale) if scaled else (x, w)))


def _mm_t_kernel(wt_ref, x_ref, oa_ref, ob_ref):
    res = lax.dot_general(wt_ref[...], x_ref[...], (((1,), (1,)), ((), ())), preferred_element_type=F32)
    nb = ob_ref.shape[1]
    tm = x_ref.shape[0]
    for j in range(oa_ref.shape[0] // V_ROWS):
        oa_ref[j * V_ROWS:j * V_ROWS + HEAD_DIM, :] = res[j * HEAD_DIM:(j + 1) * HEAD_DIM, :].astype(oa_ref.dtype)
        oa_ref[j * V_ROWS + HEAD_DIM:(j + 1) * V_ROWS, :] = jnp.ones((ONES_ROWS, tm), oa_ref.dtype)
    for j in range(ob_ref.shape[0]):
        ob_ref[j] = res[res.shape[0] - nb:, j * Q_BLOCK:(j + 1) * Q_BLOCK].astype(ob_ref.dtype)


def _proj_t(x, w_t, n_a_heads):
    n, d = x.shape
    nw = w_t.shape[0]
    nb = nw - n_a_heads * HEAD_DIM
    tm = _tile(1024, n)
    return pl.pallas_call(
        _mm_t_kernel,
        grid=(n // tm,),
        in_specs=[pl.BlockSpec((nw, d), lambda i: (0, 0)),
                  pl.BlockSpec((tm, d), lambda i: (i, 0))],
        out_specs=[pl.BlockSpec((n_a_heads * V_ROWS, tm), lambda i: (0, i)),
                   pl.BlockSpec((tm // Q_BLOCK, nb, Q_BLOCK), lambda i: (i, 0, 0))],
        out_shape=[jax.ShapeDtypeStruct((n_a_heads * V_ROWS, n), BF16),
                   jax.ShapeDtypeStruct((n // Q_BLOCK, nb, Q_BLOCK), BF16)],
        compiler_params=_params(("parallel",),
                                2 * (tm * d * 2 + nw * d * 2 + 2 * nw * tm * 2) + 2 * nw * tm * 4),
        name="proj_t",
    )(w_t, x)


def _flash_kernel(q_ref, k_ref, vt_ref, *refs, tq, tk, nk, n_side):
    side_in = refs[:n_side]
    o_ref = refs[n_side]
    side_out = refs[n_side + 1:2 * n_side + 1]
    qs_ref, sa_ref, sb_ref, m_ref, acc_ref = refs[2 * n_side + 1:]
    for w_ref, wo_ref in zip(side_in, side_out):
        wo_ref[...] = w_ref[...].astype(wo_ref.dtype)
    for g in range(KV_GROUP):
        qs_ref[g * tq:(g + 1) * tq, :] = q_ref[:, g * HEAD_DIM:(g + 1) * HEAD_DIM]
    m_ref[...] = jnp.full(m_ref.shape, -jnp.inf, F32)
    acc_ref[...] = jnp.zeros(acc_ref.shape, F32)

    def scores(idx, s_ref):
        start = pl.multiple_of(idx * tk, tk)
        s_ref[...] = lax.dot_general(k_ref[0, pl.ds(start, tk), :], qs_ref[...], (((1,), (1,)), ((), ())),
                                     preferred_element_type=F32)

    def update(idx, s_ref):
        start = pl.multiple_of(idx * tk, tk)
        s = s_ref[...]
        m_prev = m_ref[...]
        m_new = jnp.maximum(m_prev, jnp.max(s, axis=0, keepdims=True))
        alpha = jnp.exp2(m_prev - m_new)
        p = jnp.exp2(s - m_new).astype(BF16)
        acc_ref[...] = alpha * acc_ref[...] + jnp.dot(vt_ref[:, pl.ds(start, tk)], p,
                                                      preferred_element_type=F32)
        m_ref[...] = m_new

    scores(0, sa_ref)
    pairs = (nk - 1) // 2

    def body(j, carry):
        scores(2 * j + 1, sb_ref)
        update(2 * j, sa_ref)
        scores(2 * j + 2, sa_ref)
        update(2 * j + 1, sb_ref)
        return carry

    lax.fori_loop(0, pairs, body, 0)
    if nk - 2 * pairs == 2:
        scores(nk - 1, sb_ref)
        update(nk - 2, sa_ref)
        update(nk - 1, sb_ref)
    else:
        update(nk - 1, sa_ref)
    out = (acc_ref[:HEAD_DIM, :] / acc_ref[HEAD_DIM:HEAD_DIM + 1, :]).T
    for g in range(KV_GROUP):
        o_ref[:, g * HEAD_DIM:(g + 1) * HEAD_DIM] = out[g * tq:(g + 1) * tq, :].astype(o_ref.dtype)


def _side_rows(rows, steps):
    need = -(-rows // steps)
    for br in range(-(-need // ONES_ROWS) * ONES_ROWS, rows, ONES_ROWS):
        if rows % br == 0:
            return br
    return rows


def _global_attention(q, k_hm, v_t, batch, t_seq, n_q_heads, side_casts=()):
    n = q.shape[0]
    n_kv = n_q_heads // KV_GROUP
    tq = _tile(512 if side_casts else 1024, t_seq)
    tk = _tile(512, t_seq)
    nq = t_seq // tq
    nk = t_seq // tk
    gw = KV_GROUP * HEAD_DIM
    kern = functools.partial(_flash_kernel, tq=tq, tk=tk, nk=nk, n_side=len(side_casts))
    rows = KV_GROUP * tq
    steps = batch * n_kv * nq
    side_specs, side_shapes, side_bytes = [], [], 0
    for w in side_casts:
        br = _side_rows(w.shape[0], steps)
        last = pl.cdiv(w.shape[0], br) - 1
        index = functools.partial(lambda b, h, i, last: (jnp.minimum((b * n_kv + h) * nq + i, last), 0), last=last)
        side_specs.append(pl.BlockSpec((br, w.shape[1]), index))
        side_shapes.append(jax.ShapeDtypeStruct(w.shape, BF16))
        side_bytes += 2 * br * w.shape[1] * 6
    outs = pl.pallas_call(
        kern,
        grid=(batch, n_kv, nq),
        in_specs=[pl.BlockSpec((tq, gw), lambda b, h, i: (b * nq + i, h)),
                  pl.BlockSpec((1, t_seq, HEAD_DIM), lambda b, h, i: (h, b, 0)),
                  pl.BlockSpec((V_ROWS, t_seq), lambda b, h, i: (h, b))] + side_specs,
        out_specs=[pl.BlockSpec((tq, gw), lambda b, h, i: (b * nq + i, h))] + side_specs,
        out_shape=[jax.ShapeDtypeStruct((n, n_q_heads * HEAD_DIM), BF16)] + side_shapes,
        scratch_shapes=[pltpu.VMEM((rows, HEAD_DIM), BF16),
                        pltpu.VMEM((tk, rows), F32),
                        pltpu.VMEM((tk, rows), F32),
                        pltpu.VMEM((1, rows), F32),
                        pltpu.VMEM((V_ROWS, rows), F32)],
        compiler_params=_params(("arbitrary", "arbitrary", "arbitrary"),
                                2 * (2 * tq * gw * 2 + 2 * t_seq * HEAD_DIM * 2) + side_bytes
                                + rows * HEAD_DIM * 6 + 16 * rows * 4 + 8 * rows * tk * 4),
        name="global_attention",
    )(q, k_hm, v_t, *side_casts)
    return outs[0], tuple(outs[1:])


def _window_bias(n_heads):
    key = jnp.arange(3 * Q_BLOCK, dtype=jnp.int32)[:, None] - Q_BLOCK
    qpos = jnp.arange(Q_BLOCK, dtype=jnp.int32)[None, :]
    dist = jnp.abs(qpos - key)
    slopes = 2.0 ** (-8.0 * jnp.arange(1, n_heads + 1, dtype=F32) / n_heads)
    bias = -slopes[:, None, None] * dist.astype(F32)[None] * LOG2_E
    n_kv = n_heads // KV_GROUP
    bias = bias.reshape(n_kv, KV_GROUP, 3 * Q_BLOCK, Q_BLOCK).transpose(0, 2, 1, 3)
    bias = bias.reshape(n_kv, 3 * Q_BLOCK, KV_GROUP * Q_BLOCK)
    in_window = (dist <= WINDOW)
    in_window = jnp.tile(in_window, (1, KV_GROUP))[None]
    row = jnp.arange(3 * Q_BLOCK)[None, :, None]
    not_prev = row >= Q_BLOCK
    not_next = row < 2 * Q_BLOCK
    variants = [in_window, in_window & not_prev, in_window & not_next, in_window & not_prev & not_next]
    return jnp.stack([jnp.where(v, bias, MASK_VALUE) for v in variants])


def _window_kernel(q_ref, kp_ref, kc_ref, kn_ref, vp_ref, vc_ref, vn_ref, *refs, n_kv, n_sub, bias_of):
    bias_refs = refs[:-2]
    sink_ref, o_ref = refs[-2:]
    gw = KV_GROUP * HEAD_DIM
    for kv in range(n_kv):
        ks = slice(kv * HEAD_DIM, (kv + 1) * HEAD_DIM)
        k_all = jnp.concatenate([kp_ref[:, ks], kc_ref[:, ks], kn_ref[:, ks]], axis=0)
        vt_all = jnp.concatenate([vp_ref[0, ks, :]] + [vc_ref[j, ks, :] for j in range(n_sub)]
                                 + [vn_ref[0, ks, :]], axis=1)
        sink = sink_ref[:, kv * gw:(kv + 1) * gw] * LOG2_E
        for sub in range(n_sub):
            rows = slice(sub * Q_BLOCK, (sub + 1) * Q_BLOCK)
            keys = slice(sub * Q_BLOCK, (sub + 3) * Q_BLOCK)
            qs = jnp.concatenate([q_ref[rows, (kv * KV_GROUP + g) * HEAD_DIM:(kv * KV_GROUP + g + 1) * HEAD_DIM]
                                  for g in range(KV_GROUP)], axis=0)
            s = lax.dot_general(k_all[keys], qs, (((1,), (1,)), ((), ())), preferred_element_type=F32)
            s = s + bias_refs[bias_of[sub]][0, kv]
            m = jnp.maximum(jnp.max(s, axis=0, keepdims=True), sink)
            p = jnp.exp2(s - m)
            den = jnp.sum(p, axis=0, keepdims=True) + jnp.exp2(sink - m)
            pv = jnp.dot(vt_all[:, keys], p.astype(BF16), preferred_element_type=F32)
            out = (pv / den).T
            for g in range(KV_GROUP):
                hd = kv * KV_GROUP + g
                o_ref[rows, hd * HEAD_DIM:(hd + 1) * HEAD_DIM] = (
                    out[g * Q_BLOCK:(g + 1) * Q_BLOCK, :].astype(o_ref.dtype))


def _window_attention(rest, v_t, bias, sink_row, batch, t_seq, n_heads, k_col_block):
    n = rest.shape[0]
    nb = t_seq // Q_BLOCK
    n_sub = max(s for s in (8, 4, 2, 1) if nb % s == 0)
    rep_subs = [0] + ([1] if n_sub > 2 else []) + ([n_sub - 1] if n_sub > 1 else [])
    bias_of = tuple(0 if s == 0 else (len(rep_subs) - 1 if s == n_sub - 1 else 1) for s in range(n_sub))
    steps = nb // n_sub
    n_kv = n_heads // KV_GROUP
    qw = n_heads * HEAD_DIM
    kw = n_kv * HEAD_DIM
    span = n_sub * Q_BLOCK

    def prev(b, i):
        return b * nb + jnp.maximum(i * n_sub - 1, 0)

    def cur(b, i):
        return b * steps + i

    def nxt(b, i):
        return b * nb + jnp.minimum((i + 1) * n_sub, nb - 1)

    def variant(b, i, sub):
        first = (i == 0).astype(jnp.int32) if sub == 0 else 0
        last = 2 * (i == steps - 1).astype(jnp.int32) if sub == n_sub - 1 else 0
        return first + last

    k_specs = [pl.BlockSpec((Q_BLOCK, kw), lambda b, i: (prev(b, i), k_col_block)),
               pl.BlockSpec((span, kw), lambda b, i: (cur(b, i), k_col_block)),
               pl.BlockSpec((Q_BLOCK, kw), lambda b, i: (nxt(b, i), k_col_block))]
    v_specs = [pl.BlockSpec((1, kw, Q_BLOCK), lambda b, i: (prev(b, i), 0, 0)),
               pl.BlockSpec((n_sub, kw, Q_BLOCK), lambda b, i: (cur(b, i), 0, 0)),
               pl.BlockSpec((1, kw, Q_BLOCK), lambda b, i: (nxt(b, i), 0, 0))]
    bias_specs = [pl.BlockSpec((1,) + bias.shape[1:],
                               functools.partial(lambda b, i, sub: (variant(b, i, sub), 0, 0, 0), sub=sub))
                  for sub in rep_subs]
    return pl.pallas_call(
        functools.partial(_window_kernel, n_kv=n_kv, n_sub=n_sub, bias_of=bias_of),
        grid=(batch, steps),
        in_specs=[pl.BlockSpec((span, qw), lambda b, i: (cur(b, i), 0))] + k_specs + v_specs + bias_specs
        + [pl.BlockSpec((1, qw), lambda b, i: (0, 0))],
        out_specs=pl.BlockSpec((span, qw), lambda b, i: (cur(b, i), 0)),
        out_shape=jax.ShapeDtypeStruct((n, qw), BF16),
        compiler_params=_params(("parallel", "arbitrary"),
                                2 * (len(rep_subs) * bias[0].size * 4 + 2 * span * qw * 2
                                     + 4 * (span + 2 * Q_BLOCK) * kw)
                                + 16 * 3 * Q_BLOCK * KV_GROUP * Q_BLOCK * 4),
        name="window_attention",
    )(rest, rest, rest, rest, v_t, v_t, v_t, *([bias] * len(rep_subs)), sink_row)


def _merge_kernel(ya_ref, yb_ref, wa_ref, wb_ref, ga_ref, gb_ref, o_ref):
    a = jnp.dot(ya_ref[...], wa_ref[...], preferred_element_type=F32)
    b = jnp.dot(yb_ref[...], wb_ref[...], preferred_element_type=F32)
    ga = jax.nn.sigmoid(ga_ref[...].astype(F32))
    gb = jax.nn.sigmoid(gb_ref[...].astype(F32))
    o_ref[...] = (ga * a + gb * b).astype(o_ref.dtype)


def _merge(ya, yb, w_a, w_b, gates):
    n, ka = ya.shape
    kb = yb.shape[1]
    d = w_a.shape[1]
    tm = _tile(1024, n)
    tn = _tile(512, d)
    ga_blk = 0
    gb_blk = d // tn
    return pl.pallas_call(
        _merge_kernel,
        grid=(n // tm, d // tn),
        in_specs=[pl.BlockSpec((tm, ka), lambda i, j: (i, 0)),
                  pl.BlockSpec((tm, kb), lambda i, j: (i, 0)),
                  pl.BlockSpec((ka, tn), lambda i, j: (0, j)),
                  pl.BlockSpec((kb, tn), lambda i, j: (0, j)),
                  pl.BlockSpec((tm, tn), lambda i, j: (i, ga_blk + j)),
                  pl.BlockSpec((tm, tn), lambda i, j: (i, gb_blk + j))],
        out_specs=pl.BlockSpec((tm, tn), lambda i, j: (i, j)),
        out_shape=jax.ShapeDtypeStruct((n, d), BF16),
        compiler_params=_params(("parallel", "arbitrary"),
                                2 * (tm * (ka + kb) * 2 + (ka + kb) * tn * 2 + 3 * tm * tn * 2)
                                + 6 * tm * tn * 4),
        name="branch_merge",
    )(ya, yb, w_a, w_b, gates, gates)


def _resid_kernel(m_ref, w_ref, x_ref, g_ref, o_ref, *, alpha):
    acc = jnp.dot(m_ref[...], w_ref[...], preferred_element_type=F32)
    o_ref[...] = alpha * x_ref[...] + g_ref[0] * acc


def _out_proj(merged, w_o, x2, mod3, t_seq, i_gate, alpha):
    n, d = x2.shape
    k = merged.shape[1]
    tm = _tile(1024, t_seq)
    tn = _tile(1024, d)
    per = t_seq // tm
    return pl.pallas_call(
        functools.partial(_resid_kernel, alpha=alpha),
        grid=(n // tm, d // tn),
        in_specs=[pl.BlockSpec((tm, k), lambda i, j: (i, 0)),
                  pl.BlockSpec((k, tn), lambda i, j: (0, j)),
                  pl.BlockSpec((tm, tn), lambda i, j: (i, j)),
                  pl.BlockSpec((1, 1, tn), lambda i, j: ((i // per) * N_MOD + i_gate, 0, j))],
        out_specs=pl.BlockSpec((tm, tn), lambda i, j: (i, j)),
        out_shape=jax.ShapeDtypeStruct((n, d), F32),
        compiler_params=_params(("parallel", "arbitrary"),
                                2 * (tm * k * 2 + k * tn * 2 + 2 * tm * tn * 4) + 2 * tm * tn * 4),
        name="out_proj_residual",
    )(merged, w_o, x2, mod3)


def _ln_pair_kernel(z_ref, g_ref, b_ref, sc_ref, sh_ref, x1_ref, h_ref, *, rows):
    def chunk(r, carry):
        sl = pl.ds(pl.multiple_of(r * rows, rows), rows)
        x1 = _ln(z_ref[sl, :]) * g_ref[...] + b_ref[...]
        x1_ref[sl, :] = x1
        h_ref[sl, :] = (_ln(x1) * (1.0 + sc_ref[0]) + sh_ref[0]).astype(h_ref.dtype)
        return carry

    lax.fori_loop(0, z_ref.shape[0] // rows, chunk, 0)


def _ln_pair(z, ln_g, ln_b, mod3, t_seq, i_scale, i_shift):
    n, d = z.shape
    tm = _tile(512, t_seq)
    per = t_seq // tm
    return pl.pallas_call(
        functools.partial(_ln_pair_kernel, rows=_tile(LANE, tm)),
        grid=(n // tm,),
        in_specs=[pl.BlockSpec((tm, d), lambda i: (i, 0)),
                  pl.BlockSpec((1, d), lambda i: (0, 0)),
                  pl.BlockSpec((1, d), lambda i: (0, 0)),
                  pl.BlockSpec((1, 1, d), lambda i: ((i // per) * N_MOD + i_scale, 0, 0)),
                  pl.BlockSpec((1, 1, d), lambda i: ((i // per) * N_MOD + i_shift, 0, 0))],
        out_specs=[pl.BlockSpec((tm, d), lambda i: (i, 0)),
                   pl.BlockSpec((tm, d), lambda i: (i, 0))],
        out_shape=[jax.ShapeDtypeStruct((n, d), F32), jax.ShapeDtypeStruct((n, d), BF16)],
        compiler_params=_params(("parallel",), 2 * tm * d * 10 + 6 * tm * d * 4),
        name="ln1_ln2mod",
    )(z, ln_g.reshape(1, d), ln_b.reshape(1, d), mod3, mod3)


def _ffn_up_kernel(h_ref, wg_ref, wu_ref, o_ref):
    h = h_ref[...]
    g = jnp.dot(h, wg_ref[...], preferred_element_type=F32)
    u = jnp.dot(h, wu_ref[...], preferred_element_type=F32)
    o_ref[...] = (g * jax.nn.sigmoid(g) * u).astype(o_ref.dtype)


def _ffn_up(h, w_g, w_u):
    n, d = h.shape
    ff = w_g.shape[1]
    tm = _tile(1024, n)
    tn = min(512, ff)
    return pl.pallas_call(
        _ffn_up_kernel,
        grid=(n // tm, pl.cdiv(ff, tn)),
        in_specs=[pl.BlockSpec((tm, d), lambda i, j: (i, 0)),
                  pl.BlockSpec((d, tn), lambda i, j: (0, j)),
                  pl.BlockSpec((d, tn), lambda i, j: (0, j))],
        out_specs=pl.BlockSpec((tm, tn), lambda i, j: (i, j)),
        out_shape=jax.ShapeDtypeStruct((n, ff), BF16),
        compiler_params=_params(("parallel", "arbitrary"),
                                2 * (tm * d * 2 + d * 2 * tn * 2 + tm * tn * 2) + 3 * tm * 2 * tn * 4),
        name="ffn_up",
    )(h, w_g, w_u)


def _ffn_down_kernel(a_ref, w_ref, x_ref, gate_ref, g_ref, b_ref, o_ref, *, alpha, nk, last, rows):
    k = pl.program_id(1)
    tk = a_ref.shape[1]

    @pl.when(k == 0)
    def _():
        o_ref[...] = jnp.zeros(o_ref.shape, F32)

    if last == tk:
        o_ref[...] += jnp.dot(a_ref[...], w_ref[...], preferred_element_type=F32)
    else:
        @pl.when(k < nk - 1)
        def _():
            o_ref[...] += jnp.dot(a_ref[...], w_ref[...], preferred_element_type=F32)

        @pl.when(k == nk - 1)
        def _():
            o_ref[...] += jnp.dot(a_ref[:, :last], w_ref[:last, :], preferred_element_type=F32)

    @pl.when(k == nk - 1)
    def _():
        def chunk(r, carry):
            sl = pl.ds(pl.multiple_of(r * rows, rows), rows)
            z = alpha * x_ref[sl, :] + gate_ref[0] * o_ref[sl, :]
            o_ref[sl, :] = _ln(z) * g_ref[...] + b_ref[...]
            return carry

        lax.fori_loop(0, o_ref.shape[0] // rows, chunk, 0)


def _ffn_down(a, w_d, x1, ln_g, ln_b, mod3, t_seq, i_gate, alpha):
    n, ff = a.shape
    d = w_d.shape[1]
    tm = _tile(512, t_seq)
    tk = min(1024, ff)
    nk = pl.cdiv(ff, tk)
    last = ff - (nk - 1) * tk
    assert last % LANE == 0, "the partial contraction block must stay lane-aligned"
    per = t_seq // tm
    return pl.pallas_call(
        functools.partial(_ffn_down_kernel, alpha=alpha, nk=nk, last=last, rows=_tile(LANE, tm)),
        grid=(n // tm, nk),
        in_specs=[pl.BlockSpec((tm, tk), lambda i, k: (i, k)),
                  pl.BlockSpec((tk, d), lambda i, k: (k, 0)),
                  pl.BlockSpec((tm, d), lambda i, k: (i, 0)),
                  pl.BlockSpec((1, 1, d), lambda i, k: ((i // per) * N_MOD + i_gate, 0, 0)),
                  pl.BlockSpec((1, d), lambda i, k: (0, 0)),
                  pl.BlockSpec((1, d), lambda i, k: (0, 0))],
        out_specs=pl.BlockSpec((tm, d), lambda i, k: (i, 0)),
        out_shape=jax.ShapeDtypeStruct((n, d), F32),
        compiler_params=_params(("parallel", "arbitrary"),
                                2 * (tm * tk * 2 + tk * d * 2 + 2 * tm * d * 4) + 2 * tm * d * 4),
        name="ffn_down_ln",
    )(a, w_d, x1, mod3, ln_g.reshape(1, d), ln_b.reshape(1, d))


def _rope_tables(t_max):
    rows = t_max // GRID_W
    row = jnp.repeat(jnp.arange(rows, dtype=F32), GRID_W)
    col = jnp.tile(jnp.arange(GRID_W, dtype=F32), rows)
    inv = 1.0 / (ROPE_THETA ** (jnp.arange(0, HALF_ROT, 2, dtype=F32) / HALF_ROT))
    ang_r = row[:, None] * inv[None, :]
    ang_c = col[:, None] * inv[None, :]
    cr, sr, cc, sc = jnp.cos(ang_r), jnp.sin(ang_r), jnp.cos(ang_c), jnp.sin(ang_c)
    cos_t = jnp.concatenate([cr, cc, cr, cc], axis=-1)
    sin_t = jnp.concatenate([-sr, -sc, sr, sc], axis=-1)
    return cos_t, sin_t


def _rot_layout(w):
    lead = w.shape[:-1]
    w = w.reshape(lead + (-1, 2, 2, HALF_ROT // 2))
    return jnp.swapaxes(w, -3, -2).reshape(lead + (-1,))


def _prep_layer(w_in, q_norm, k_norm):
    d = w_in.shape[0]
    n_heads = d // HEAD_DIM
    ha = n_heads // 2
    qa_w = ha * HEAD_DIM
    kv_w = (ha // KV_GROUP) * HEAD_DIM
    o_ka = qa_w
    o_va = o_ka + kv_w
    o_qb = o_va + kv_w
    o_kb = o_qb + qa_w
    o_vb = o_kb + kv_w
    o_ga = o_vb + kv_w
    w_qk = _rot_layout(_cast_cols(w_in, 0, o_va))
    w_v_t = jnp.concatenate([_cast_cols(w_in, o_va, kv_w), _cast_cols(w_in, o_vb, kv_w)], axis=1).T
    w_qkb = _cast_cols(w_in, o_qb, o_vb - o_qb)
    w_gates = _cast_cols(w_in, o_ga, w_in.shape[1] - o_ga)
    gain = jnp.concatenate([jnp.tile(_rot_layout(q_norm), ha),
                            jnp.tile(_rot_layout(k_norm), ha // KV_GROUP)]).reshape(1, -1)
    q_scale = jnp.full((qa_w,), LOG2_E / math.sqrt(HEAD_DIM), F32)
    scale = jnp.concatenate([q_scale, jnp.ones((kv_w,), F32)]).reshape(1, -1)
    return dict(w_qk=w_qk, w_v_t=w_v_t, w_qkb=w_qkb, w_gates=w_gates, gain=gain, scale=scale,
                ha=ha, kv_w=kv_w, qa_w=qa_w)


def _encoder_layer(x, mod, p, late_w, sink_row, win_bias, ln1_g, ln1_b, ln2_g, ln2_b, cos_t, sin_t, alpha):
    b, t, d = x.shape
    x2 = x.reshape(b * t, d)
    mod3 = mod.reshape(b * N_MOD, 1, d)
    ha, kv_w, qa_w = p["ha"], p["kv_w"], p["qa_w"]

    h = _ln_mod(x2, mod3, t, 1, 0)
    qa, ka = _qk_proj(h, p["w_qk"], p["gain"], p["scale"], cos_t, sin_t, t, ha)
    va_t, vb_t = _proj_t(h, p["w_v_t"], kv_w // HEAD_DIM)
    qkb = _proj(h, p["w_qkb"], p["scale"], tn_pref=1280)
    gates = _proj(h, p["w_gates"], None, tn_pref=1024)
    if late_w[0].dtype == BF16:
        ya, _ = _global_attention(qa, ka, va_t, b, t, ha)
    else:
        ya, late_w = _global_attention(qa, ka, va_t, b, t, ha, side_casts=late_w)
    w_a, w_b, w_o, w_g, w_u, w_d = late_w
    yb = _window_attention(qkb, vb_t, win_bias, sink_row, b, t, ha, k_col_block=qa_w // kv_w)
    merged = _merge(ya, yb, w_a, w_b, gates)
    z = _out_proj(merged, w_o, x2, mod3, t, 2, alpha)
    x1, h2 = _ln_pair(z, ln1_g, ln1_b, mod3, t, 4, 3)
    a = _ffn_up(h2, w_g, w_u)
    y = _ffn_down(a, w_d, x1, ln2_g, ln2_b, mod3, t, 5, alpha)
    return y.reshape(b, t, d), late_w


def kernel(x_prompt, x_sample, c_prompt, c_sample, w_ada, b_ada, w_in, q_norm_a, k_norm_a, sink_b,
           w_br_a, w_br_b, w_o, ln1_g, ln1_b, w_ffn_gate, w_ffn_up, w_ffn_down, ln2_g, ln2_b):
    depth = w_ada.shape[0]
    alpha = float((2.0 * depth) ** 0.25)
    d = x_prompt.shape[-1]
    bp, bs = c_prompt.shape[0], c_sample.shape[0]
    rows = -(-(bp + bs) // 8) * 8
    cos_t, sin_t = _rope_tables(max(x_prompt.shape[1], x_sample.shape[1]))
    y_p, y_s = x_prompt, x_sample
    c_all = jnp.concatenate([c_prompt, c_sample, jnp.zeros((rows - bp - bs, d), F32)], axis=0)
    win_bias = _window_bias(sink_b.shape[1])
    for l in range(depth):
        mod = _ada(c_all, w_ada[l], b_ada[l]).reshape(rows, N_MOD, d)
        p = _prep_layer(w_in[l], q_norm_a[l], k_norm_a[l])
        late_w = (w_br_a[l], w_br_b[l], w_o[l], w_ffn_gate[l], w_ffn_up[l], w_ffn_down[l])
        sink_row = jnp.repeat(sink_b[l], HEAD_DIM).reshape(1, -1)
        args = (sink_row, win_bias, ln1_g[l], ln1_b[l], ln2_g[l], ln2_b[l], cos_t, sin_t, alpha)
        y_p, late_w = _encoder_layer(y_p, mod[:bp], p, late_w, *args)
        y_s, _ = _encoder_layer(y_s, mod[bp:bp + bs], p, late_w, *args)
    return (y_p, y_s)
```

```python
import functools
import math

import jax
import jax.numpy as jnp
from jax import lax
from jax.experimental import pallas as pl
from jax.experimental.pallas import tpu as pltpu

F32 = jnp.float32
BF16 = jnp.bfloat16

HEAD_DIM = 128
GRID_W = 64
Q_BLOCK = 128
WINDOW = 128
HALF_ROT = HEAD_DIM // 2
ROPE_THETA = 10000.0
N_MOD = 6
LN_EPS = 1e-5
RMS_EPS = 1e-6
KV_GROUP = 4
MASK_VALUE = -1e30
LOG2_E = 1.4426950408889634

ONES_ROWS = 16
V_ROWS = HEAD_DIM + ONES_ROWS
LANE = 128
VMEM_CAP_BYTES = 60 * 1024 * 1024
CAST_BLOCK_BYTES = 4 * 1024 * 1024


def _tile(pref, *dims):
    t = (min(pref, *dims) // LANE) * LANE
    while t >= LANE:
        if all(d % t == 0 for d in dims):
            return t
        t -= LANE
    return min(dims)


def _params(semantics, vmem_bytes):
    limit = int(min(max(vmem_bytes, 16 * 1024 * 1024), VMEM_CAP_BYTES))
    return pltpu.CompilerParams(dimension_semantics=semantics, vmem_limit_bytes=limit)


def _ln(x):
    mu = jnp.mean(x, axis=-1, keepdims=True)
    xc = x - mu
    return xc * lax.rsqrt(jnp.mean(xc * xc, axis=-1, keepdims=True) + LN_EPS)


def _ada_kernel(c_ref, w_ref, b_ref, o_ref):
    c = c_ref[...]
    a = (c * jax.nn.sigmoid(c)).astype(BF16)
    o_ref[...] = jnp.dot(a, w_ref[...].astype(BF16), preferred_element_type=F32) + b_ref[...]


def _ada(c_pad, w_ada, b_ada):
    rows, d = c_pad.shape
    n = w_ada.shape[1]
    tn = _tile(512, n)
    return pl.pallas_call(
        _ada_kernel,
        grid=(n // tn,),
        in_specs=[pl.BlockSpec((rows, d), lambda j: (0, 0)),
                  pl.BlockSpec((d, tn), lambda j: (0, j)),
                  pl.BlockSpec((1, tn), lambda j: (0, j))],
        out_specs=pl.BlockSpec((rows, tn), lambda j: (0, j)),
        out_shape=jax.ShapeDtypeStruct((rows, n), F32),
        compiler_params=_params(("parallel",), 2 * d * tn * 4 + d * tn * 2 + (4 << 20)),
        name="ada_mod",
    )(c_pad, w_ada, b_ada.reshape(1, n))


def _ln_mod_kernel(x_ref, sc_ref, sh_ref, o_ref):
    y = _ln(x_ref[...])
    o_ref[...] = (y * (1.0 + sc_ref[0]) + sh_ref[0]).astype(o_ref.dtype)


def _ln_mod(x2, mod3, t_seq, i_scale, i_shift):
    n, d = x2.shape
    tm = _tile(512, t_seq)
    per = t_seq // tm
    return pl.pallas_call(
        _ln_mod_kernel,
        grid=(n // tm,),
        in_specs=[pl.BlockSpec((tm, d), lambda i: (i, 0)),
                  pl.BlockSpec((1, 1, d), lambda i: ((i // per) * N_MOD + i_scale, 0, 0)),
                  pl.BlockSpec((1, 1, d), lambda i: ((i // per) * N_MOD + i_shift, 0, 0))],
        out_specs=pl.BlockSpec((tm, d), lambda i: (i, 0)),
        out_shape=jax.ShapeDtypeStruct((n, d), BF16),
        compiler_params=_params(("parallel",), 2 * tm * d * 6 + 4 * tm * d * 4),
        name="ln_mod",
    )(x2, mod3, mod3)


def _qk_kernel(h_ref, w_ref, g_ref, s_ref, cos_ref, sin_ref, q_ref, k_ref, *, chunk):
    h = h_ref[...]
    cos = cos_ref[...]
    sin = sin_ref[...]
    n_q = q_ref.shape[1] // HEAD_DIM
    for c in range(w_ref.shape[1] // chunk):
        acc = jnp.dot(h, w_ref[:, c * chunk:(c + 1) * chunk], preferred_element_type=F32)
        for hh in range(chunk // HEAD_DIM):
            head = c * (chunk // HEAD_DIM) + hh
            sl = slice(head * HEAD_DIM, (head + 1) * HEAD_DIM)
            x = acc[:, hh * HEAD_DIM:(hh + 1) * HEAD_DIM]
            y = x * lax.rsqrt(jnp.mean(x * x, axis=-1, keepdims=True) + RMS_EPS) * g_ref[:, sl]
            partner = pltpu.roll(y, HALF_ROT, 1)
            out = ((y * cos + partner * sin) * s_ref[:, sl]).astype(q_ref.dtype)
            if head < n_q:
                q_ref[:, sl] = out
            else:
                k_ref[head - n_q] = out


def _qk_proj(h, w_qk, gain, scale, cos_t, sin_t, t_seq, n_q_heads):
    n, d = h.shape
    nw = w_qk.shape[1]
    qw = n_q_heads * HEAD_DIM
    n_kv = (nw - qw) // HEAD_DIM
    tm = _tile(512, t_seq)
    per = t_seq // tm
    return pl.pallas_call(
        functools.partial(_qk_kernel, chunk=_tile(256, nw)),
        grid=(n // tm,),
        in_specs=[pl.BlockSpec((tm, d), lambda i: (i, 0)),
                  pl.BlockSpec((d, nw), lambda i: (0, 0), pipeline_mode=pl.Buffered(1)),
                  pl.BlockSpec((1, nw), lambda i: (0, 0)),
                  pl.BlockSpec((1, nw), lambda i: (0, 0)),
                  pl.BlockSpec((tm, HEAD_DIM), lambda i: (i % per, 0)),
                  pl.BlockSpec((tm, HEAD_DIM), lambda i: (i % per, 0))],
        out_specs=[pl.BlockSpec((tm, qw), lambda i: (i, 0)),
                   pl.BlockSpec((n_kv, tm, HEAD_DIM), lambda i: (0, i, 0))],
        out_shape=[jax.ShapeDtypeStruct((n, qw), BF16),
                   jax.ShapeDtypeStruct((n_kv, n, HEAD_DIM), BF16)],
        compiler_params=_params(("parallel",),
                                d * nw * 2 + 2 * (tm * d * 2 + tm * nw * 2 + 2 * tm * HEAD_DIM * 4)
                                + 16 * tm * 256 * 4),
        name="qk_proj_rope",
    )(h, w_qk, gain, scale, cos_t, sin_t)


def _cast_kernel(w_ref, o_ref):
    o_ref[...] = w_ref[...].astype(o_ref.dtype)


def _cast_cols(w, col0=0, ncols=None):
    rows, width = w.shape
    ncols = width if ncols is None else ncols
    if col0 == 0 and ncols == width:
        tc = width
    else:
        tc = _tile(2048, ncols, *((col0,) if col0 else ()))
    tr = max(8, min(rows, (CAST_BLOCK_BYTES // (4 * tc)) // 8 * 8))
    while rows % tr:
        tr -= 8
    c0 = col0 // tc
    return pl.pallas_call(
        _cast_kernel,
        grid=(rows // tr, ncols // tc),
        in_specs=[pl.BlockSpec((tr, tc), lambda i, j: (i, c0 + j))],
        out_specs=pl.BlockSpec((tr, tc), lambda i, j: (i, j)),
        out_shape=jax.ShapeDtypeStruct((rows, ncols), BF16),
        compiler_params=_params(("parallel", "parallel"), 2 * tr * tc * 6 + tr * tc * 4),
        name="cast_bf16",
    )(w)


def _mm_scaled_kernel(x_ref, w_ref, s_ref, o_ref):
    acc = jnp.dot(x_ref[...], w_ref[...], preferred_element_type=F32)
    o_ref[...] = (acc * s_ref[...]).astype(o_ref.dtype)


def _mm_kernel(x_ref, w_ref, o_ref):
    o_ref[...] = jnp.dot(x_ref[...], w_ref[...], preferred_element_type=F32).astype(o_ref.dtype)


def _proj(x, w, col_scale, tn_pref=512):
    n, d = x.shape
    nw = w.shape[1]
    tm = _tile(1024, n)
    tn = _tile(tn_pref, nw)
    scaled = col_scale is not None
    in_specs = [pl.BlockSpec((tm, d), lambda i, j: (i, 0)),
                pl.BlockSpec((d, tn), lambda i, j: (0, j))]
    if scaled:
        in_specs.append(pl.BlockSpec((1, tn), lambda i, j: (0, j)))
    return pl.pallas_call(
        _mm_scaled_kernel if scaled else _mm_kernel,
        grid=(n // tm, nw // tn),
        in_specs=in_specs,
        out_specs=pl.BlockSpec((tm, tn), lambda i, j: (i, j)),
        out_shape=jax.ShapeDtypeStruct((n, nw), BF16),
        compiler_params=_params(("parallel", "arbitrary"),
                                2 * (tm * d * 2 + d * tn * 2 + tm * tn * 2) + 2 * tm * tn * 4),
        name="proj",
    )(*((x, w, col_scale) if scaled else (x, w)))


def _mm_t_kernel(wt_ref, x_ref, oa_ref, ob_ref):
    res = lax.dot_general(wt_ref[...], x_ref[...], (((1,), (1,)), ((), ())), preferred_element_type=F32)
    nb = ob_ref.shape[1]
    tm = x_ref.shape[0]
    for j in range(oa_ref.shape[0] // V_ROWS):
        oa_ref[j * V_ROWS:j * V_ROWS + HEAD_DIM, :] = res[j * HEAD_DIM:(j + 1) * HEAD_DIM, :].astype(oa_ref.dtype)
        oa_ref[j * V_ROWS + HEAD_DIM:(j + 1) * V_ROWS, :] = jnp.ones((ONES_ROWS, tm), oa_ref.dtype)
    for j in range(ob_ref.shape[0]):
        ob_ref[j] = res[res.shape[0] - nb:, j * Q_BLOCK:(j + 1) * Q_BLOCK].astype(ob_ref.dtype)


def _proj_t(x, w_t, n_a_heads):
    n, d = x.shape
    nw = w_t.shape[0]
    nb = nw - n_a_heads * HEAD_DIM
    tm = _tile(1024, n)
    return pl.pallas_call(
        _mm_t_kernel,
        grid=(n // tm,),
        in_specs=[pl.BlockSpec((nw, d), lambda i: (0, 0)),
                  pl.BlockSpec((tm, d), lambda i: (i, 0))],
        out_specs=[pl.BlockSpec((n_a_heads * V_ROWS, tm), lambda i: (0, i)),
                   pl.BlockSpec((tm // Q_BLOCK, nb, Q_BLOCK), lambda i: (i, 0, 0))],
        out_shape=[jax.ShapeDtypeStruct((n_a_heads * V_ROWS, n), BF16),
                   jax.ShapeDtypeStruct((n // Q_BLOCK, nb, Q_BLOCK), BF16)],
        compiler_params=_params(("parallel",),
                                2 * (tm * d * 2 + nw * d * 2 + 2 * nw * tm * 2) + 2 * nw * tm * 4),
        name="proj_t",
    )(w_t, x)


def _flash_kernel(q_ref, k_ref, vt_ref, *refs, tq, tk, nk, n_side):
    side_in = refs[:n_side]
    o_ref = refs[n_side]
    side_out = refs[n_side + 1:2 * n_side + 1]
    qs_ref, sa_ref, sb_ref, m_ref, acc_ref = refs[2 * n_side + 1:]
    for w_ref, wo_ref in zip(side_in, side_out):
        wo_ref[...] = w_ref[...].astype(wo_ref.dtype)
    for g in range(KV_GROUP):
        qs_ref[g * tq:(g + 1) * tq, :] = q_ref[:, g * HEAD_DIM:(g + 1) * HEAD_DIM]
    m_ref[...] = jnp.full(m_ref.shape, -jnp.inf, F32)
    acc_ref[...] = jnp.zeros(acc_ref.shape, F32)

    def scores(idx, s_ref):
        start = pl.multiple_of(idx * tk, tk)
        s_ref[...] = lax.dot_general(k_ref[0, pl.ds(start, tk), :], qs_ref[...], (((1,), (1,)), ((), ())),
                                     preferred_element_type=F32)

    def update(idx, s_ref):
        start = pl.multiple_of(idx * tk, tk)
        s = s_ref[...]
        m_prev = m_ref[...]
        m_new = jnp.maximum(m_prev, jnp.max(s, axis=0, keepdims=True))
        alpha = jnp.exp2(m_prev - m_new)
        p = jnp.exp2(s - m_new).astype(BF16)
        acc_ref[...] = alpha * acc_ref[...] + jnp.dot(vt_ref[:, pl.ds(start, tk)], p,
                                                      preferred_element_type=F32)
        m_ref[...] = m_new

    scores(0, sa_ref)
    pairs = (nk - 1) // 2

    def body(j, carry):
        scores(2 * j + 1, sb_ref)
        update(2 * j, sa_ref)
        scores(2 * j + 2, sa_ref)
        update(2 * j + 1, sb_ref)
        return carry

    lax.fori_loop(0, pairs, body, 0)
    if nk - 2 * pairs == 2:
        scores(nk - 1, sb_ref)
        update(nk - 2, sa_ref)
        update(nk - 1, sb_ref)
    else:
        update(nk - 1, sa_ref)
    out = (acc_ref[:HEAD_DIM, :] / acc_ref[HEAD_DIM:HEAD_DIM + 1, :]).T
    for g in range(KV_GROUP):
        o_ref[:, g * HEAD_DIM:(g + 1) * HEAD_DIM] = out[g * tq:(g + 1) * tq, :].astype(o_ref.dtype)


def _side_rows(rows, steps):
    need = -(-rows // steps)
    for br in range(-(-need // ONES_ROWS) * ONES_ROWS, rows, ONES_ROWS):
        if rows % br == 0:
            return br
    return rows


def _global_attention(q, k_hm, v_t, batch, t_seq, n_q_heads, side_casts=()):
    n = q.shape[0]
    n_kv = n_q_heads // KV_GROUP
    tq = _tile(512 if side_casts else 1024, t_seq)
    tk = _tile(512, t_seq)
    nq = t_seq // tq
    nk = t_seq // tk
    gw = KV_GROUP * HEAD_DIM
    kern = functools.partial(_flash_kernel, tq=tq, tk=tk, nk=nk, n_side=len(side_casts))
    rows = KV_GROUP * tq
    steps = batch * n_kv * nq
    side_specs, side_shapes, side_bytes = [], [], 0
    for w in side_casts:
        br = _side_rows(w.shape[0], steps)
        last = pl.cdiv(w.shape[0], br) - 1
        index = functools.partial(lambda b, h, i, last: (jnp.minimum((b * n_kv + h) * nq + i, last), 0), last=last)
        side_specs.append(pl.BlockSpec((br, w.shape[1]), index))
        side_shapes.append(jax.ShapeDtypeStruct(w.shape, BF16))
        side_bytes += 2 * br * w.shape[1] * 6
    outs = pl.pallas_call(
        kern,
        grid=(batch, n_kv, nq),
        in_specs=[pl.BlockSpec((tq, gw), lambda b, h, i: (b * nq + i, h)),
                  pl.BlockSpec((1, t_seq, HEAD_DIM), lambda b, h, i: (h, b, 0)),
                  pl.BlockSpec((V_ROWS, t_seq), lambda b, h, i: (h, b))] + side_specs,
        out_specs=[pl.BlockSpec((tq, gw), lambda b, h, i: (b * nq + i, h))] + side_specs,
        out_shape=[jax.ShapeDtypeStruct((n, n_q_heads * HEAD_DIM), BF16)] + side_shapes,
        scratch_shapes=[pltpu.VMEM((rows, HEAD_DIM), BF16),
                        pltpu.VMEM((tk, rows), F32),
                        pltpu.VMEM((tk, rows), F32),
                        pltpu.VMEM((1, rows), F32),
                        pltpu.VMEM((V_ROWS, rows), F32)],
        compiler_params=_params(("arbitrary", "arbitrary", "arbitrary"),
                                2 * (2 * tq * gw * 2 + 2 * t_seq * HEAD_DIM * 2) + side_bytes
                                + rows * HEAD_DIM * 6 + 16 * rows * 4 + 8 * rows * tk * 4),
        name="global_attention",
    )(q, k_hm, v_t, *side_casts)
    return outs[0], tuple(outs[1:])


def _window_bias(n_heads):
    key = jnp.arange(3 * Q_BLOCK, dtype=jnp.int32)[:, None] - Q_BLOCK
    qpos = jnp.arange(Q_BLOCK, dtype=jnp.int32)[None, :]
    dist = jnp.abs(qpos - key)
    slopes = 2.0 ** (-8.0 * jnp.arange(1, n_heads + 1, dtype=F32) / n_heads)
    bias = -slopes[:, None, None] * dist.astype(F32)[None] * LOG2_E
    n_kv = n_heads // KV_GROUP
    bias = bias.reshape(n_kv, KV_GROUP, 3 * Q_BLOCK, Q_BLOCK).transpose(0, 2, 1, 3)
    bias = bias.reshape(n_kv, 3 * Q_BLOCK, KV_GROUP * Q_BLOCK)
    in_window = (dist <= WINDOW)
    in_window = jnp.tile(in_window, (1, KV_GROUP))[None]
    row = jnp.arange(3 * Q_BLOCK)[None, :, None]
    not_prev = row >= Q_BLOCK
    not_next = row < 2 * Q_BLOCK
    variants = [in_window, in_window & not_prev, in_window & not_next, in_window & not_prev & not_next]
    return jnp.stack([jnp.where(v, bias, MASK_VALUE) for v in variants])


def _window_kernel(q_ref, kp_ref, kc_ref, kn_ref, vp_ref, vc_ref, vn_ref, *refs, n_kv, n_sub, bias_of):
    bias_refs = refs[:-2]
    sink_ref, o_ref = refs[-2:]
    gw = KV_GROUP * HEAD_DIM
    for kv in range(n_kv):
        ks = slice(kv * HEAD_DIM, (kv + 1) * HEAD_DIM)
        k_all = jnp.concatenate([kp_ref[:, ks], kc_ref[:, ks], kn_ref[:, ks]], axis=0)
        vt_all = jnp.concatenate([vp_ref[0, ks, :]] + [vc_ref[j, ks, :] for j in range(n_sub)]
                                 + [vn_ref[0, ks, :]], axis=1)
        sink = sink_ref[:, kv * gw:(kv + 1) * gw] * LOG2_E
        for sub in range(n_sub):
            rows = slice(sub * Q_BLOCK, (sub + 1) * Q_BLOCK)
            keys = slice(sub * Q_BLOCK, (sub + 3) * Q_BLOCK)
            qs = jnp.concatenate([q_ref[rows, (kv * KV_GROUP + g) * HEAD_DIM:(kv * KV_GROUP + g + 1) * HEAD_DIM]
                                  for g in range(KV_GROUP)], axis=0)
            s = lax.dot_general(k_all[keys], qs, (((1,), (1,)), ((), ())), preferred_element_type=F32)
            s = s + bias_refs[bias_of[sub]][0, kv]
            m = jnp.maximum(jnp.max(s, axis=0, keepdims=True), sink)
            p = jnp.exp2(s - m)
            den = jnp.sum(p, axis=0, keepdims=True) + jnp.exp2(sink - m)
            pv = jnp.dot(vt_all[:, keys], p.astype(BF16), preferred_element_type=F32)
            out = (pv / den).T
            for g in range(KV_GROUP):
                hd = kv * KV_GROUP + g
                o_ref[rows, hd * HEAD_DIM:(hd + 1) * HEAD_DIM] = (
                    out[g * Q_BLOCK:(g + 1) * Q_BLOCK, :].astype(o_ref.dtype))


def _window_attention(rest, v_t, bias, sink_row, batch, t_seq, n_heads, k_col_block):
    n = rest.shape[0]
    nb = t_seq // Q_BLOCK
    n_sub = max(s for s in (8, 4, 2, 1) if nb % s == 0)
    rep_subs = [0] + ([1] if n_sub > 2 else []) + ([n_sub - 1] if n_sub > 1 else [])
    bias_of = tuple(0 if s == 0 else (len(rep_subs) - 1 if s == n_sub - 1 else 1) for s in range(n_sub))
    steps = nb // n_sub
    n_kv = n_heads // KV_GROUP
    qw = n_heads * HEAD_DIM
    kw = n_kv * HEAD_DIM
    span = n_sub * Q_BLOCK

    def prev(b, i):
        return b * nb + jnp.maximum(i * n_sub - 1, 0)

    def cur(b, i):
        return b * steps + i

    def nxt(b, i):
        return b * nb + jnp.minimum((i + 1) * n_sub, nb - 1)

    def variant(b, i, sub):
        first = (i == 0).astype(jnp.int32) if sub == 0 else 0
        last = 2 * (i == steps - 1).astype(jnp.int32) if sub == n_sub - 1 else 0
        return first + last

    k_specs = [pl.BlockSpec((Q_BLOCK, kw), lambda b, i: (prev(b, i), k_col_block)),
               pl.BlockSpec((span, kw), lambda b, i: (cur(b, i), k_col_block)),
               pl.BlockSpec((Q_BLOCK, kw), lambda b, i: (nxt(b, i), k_col_block))]
    v_specs = [pl.BlockSpec((1, kw, Q_BLOCK), lambda b, i: (prev(b, i), 0, 0)),
               pl.BlockSpec((n_sub, kw, Q_BLOCK), lambda b, i: (cur(b, i), 0, 0)),
               pl.BlockSpec((1, kw, Q_BLOCK), lambda b, i: (nxt(b, i), 0, 0))]
    bias_specs = [pl.BlockSpec((1,) + bias.shape[1:],
                               functools.partial(lambda b, i, sub: (variant(b, i, sub), 0, 0, 0), sub=sub))
                  for sub in rep_subs]
    return pl.pallas_call(
        functools.partial(_window_kernel, n_kv=n_kv, n_sub=n_sub, bias_of=bias_of),
        grid=(batch, steps),
        in_specs=[pl.BlockSpec((span, qw), lambda b, i: (cur(b, i), 0))] + k_specs + v_specs + bias_specs
        + [pl.BlockSpec((1, qw), lambda b, i: (0, 0))],
        out_specs=pl.BlockSpec((span, qw), lambda b, i: (cur(b, i), 0)),
        out_shape=jax.ShapeDtypeStruct((n, qw), BF16),
        compiler_params=_params(("parallel", "arbitrary"),
                                2 * (len(rep_subs) * bias[0].size * 4 + 2 * span * qw * 2
                                     + 4 * (span + 2 * Q_BLOCK) * kw)
                                + 16 * 3 * Q_BLOCK * KV_GROUP * Q_BLOCK * 4),
        name="window_attention",
    )(rest, rest, rest, rest, v_t, v_t, v_t, *([bias] * len(rep_subs)), sink_row)


def _merge_kernel(ya_ref, yb_ref, wa_ref, wb_ref, ga_ref, gb_ref, o_ref):
    a = jnp.dot(ya_ref[...], wa_ref[...], preferred_element_type=F32)
    b = jnp.dot(yb_ref[...], wb_ref[...], preferred_element_type=F32)
    ga = jax.nn.sigmoid(ga_ref[...].astype(F32))
    gb = jax.nn.sigmoid(gb_ref[...].astype(F32))
    o_ref[...] = (ga * a + gb * b).astype(o_ref.dtype)


def _merge(ya, yb, w_a, w_b, gates):
    n, ka = ya.shape
    kb = yb.shape[1]
    d = w_a.shape[1]
    tm = _tile(1024, n)
    tn = _tile(512, d)
    ga_blk = 0
    gb_blk = d // tn
    return pl.pallas_call(
        _merge_kernel,
        grid=(n // tm, d // tn),
        in_specs=[pl.BlockSpec((tm, ka), lambda i, j: (i, 0)),
                  pl.BlockSpec((tm, kb), lambda i, j: (i, 0)),
                  pl.BlockSpec((ka, tn), lambda i, j: (0, j)),
                  pl.BlockSpec((kb, tn), lambda i, j: (0, j)),
                  pl.BlockSpec((tm, tn), lambda i, j: (i, ga_blk + j)),
                  pl.BlockSpec((tm, tn), lambda i, j: (i, gb_blk + j))],
        out_specs=pl.BlockSpec((tm, tn), lambda i, j: (i, j)),
        out_shape=jax.ShapeDtypeStruct((n, d), BF16),
        compiler_params=_params(("parallel", "arbitrary"),
                                2 * (tm * (ka + kb) * 2 + (ka + kb) * tn * 2 + 3 * tm * tn * 2)
                                + 6 * tm * tn * 4),
        name="branch_merge",
    )(ya, yb, w_a, w_b, gates, gates)


def _resid_kernel(m_ref, w_ref, x_ref, g_ref, o_ref, *, alpha):
    acc = jnp.dot(m_ref[...], w_ref[...], preferred_element_type=F32)
    o_ref[...] = alpha * x_ref[...] + g_ref[0] * acc


def _out_proj(merged, w_o, x2, mod3, t_seq, i_gate, alpha):
    n, d = x2.shape
    k = merged.shape[1]
    tm = _tile(1024, t_seq)
    tn = _tile(1024, d)
    per = t_seq // tm
    return pl.pallas_call(
        functools.partial(_resid_kernel, alpha=alpha),
        grid=(n // tm, d // tn),
        in_specs=[pl.BlockSpec((tm, k), lambda i, j: (i, 0)),
                  pl.BlockSpec((k, tn), lambda i, j: (0, j)),
                  pl.BlockSpec((tm, tn), lambda i, j: (i, j)),
                  pl.BlockSpec((1, 1, tn), lambda i, j: ((i // per) * N_MOD + i_gate, 0, j))],
        out_specs=pl.BlockSpec((tm, tn), lambda i, j: (i, j)),
        out_shape=jax.ShapeDtypeStruct((n, d), F32),
        compiler_params=_params(("parallel", "arbitrary"),
                                2 * (tm * k * 2 + k * tn * 2 + 2 * tm * tn * 4) + 2 * tm * tn * 4),
        name="out_proj_residual",
    )(merged, w_o, x2, mod3)


def _ln_pair_kernel(z_ref, g_ref, b_ref, sc_ref, sh_ref, x1_ref, h_ref, *, rows):
    def chunk(r, carry):
        sl = pl.ds(pl.multiple_of(r * rows, rows), rows)
        x1 = _ln(z_ref[sl, :]) * g_ref[...] + b_ref[...]
        x1_ref[sl, :] = x1
        h_ref[sl, :] = (_ln(x1) * (1.0 + sc_ref[0]) + sh_ref[0]).astype(h_ref.dtype)
        return carry

    lax.fori_loop(0, z_ref.shape[0] // rows, chunk, 0)


def _ln_pair(z, ln_g, ln_b, mod3, t_seq, i_scale, i_shift):
    n, d = z.shape
    tm = _tile(512, t_seq)
    per = t_seq // tm
    return pl.pallas_call(
        functools.partial(_ln_pair_kernel, rows=_tile(LANE, tm)),
        grid=(n // tm,),
        in_specs=[pl.BlockSpec((tm, d), lambda i: (i, 0)),
                  pl.BlockSpec((1, d), lambda i: (0, 0)),
                  pl.BlockSpec((1, d), lambda i: (0, 0)),
                  pl.BlockSpec((1, 1, d), lambda i: ((i // per) * N_MOD + i_scale, 0, 0)),
                  pl.BlockSpec((1, 1, d), lambda i: ((i // per) * N_MOD + i_shift, 0, 0))],
        out_specs=[pl.BlockSpec((tm, d), lambda i: (i, 0)),
                   pl.BlockSpec((tm, d), lambda i: (i, 0))],
        out_shape=[jax.ShapeDtypeStruct((n, d), F32), jax.ShapeDtypeStruct((n, d), BF16)],
        compiler_params=_params(("parallel",), 2 * tm * d * 10 + 6 * tm * d * 4),
        name="ln1_ln2mod",
    )(z, ln_g.reshape(1, d), ln_b.reshape(1, d), mod3, mod3)


def _ffn_up_kernel(h_ref, wg_ref, wu_ref, o_ref):
    h = h_ref[...]
    g = jnp.dot(h, wg_ref[...], preferred_element_type=F32)
    u = jnp.dot(h, wu_ref[...], preferred_element_type=F32)
    o_ref[...] = (g * jax.nn.sigmoid(g) * u).astype(o_ref.dtype)


def _ffn_up(h, w_g, w_u):
    n, d = h.shape
    ff = w_g.shape[1]
    tm = _tile(1024, n)
    tn = min(512, ff)
    return pl.pallas_call(
        _ffn_up_kernel,
        grid=(n // tm, pl.cdiv(ff, tn)),
        in_specs=[pl.BlockSpec((tm, d), lambda i, j: (i, 0)),
                  pl.BlockSpec((d, tn), lambda i, j: (0, j)),
                  pl.BlockSpec((d, tn), lambda i, j: (0, j))],
        out_specs=pl.BlockSpec((tm, tn), lambda i, j: (i, j)),
        out_shape=jax.ShapeDtypeStruct((n, ff), BF16),
        compiler_params=_params(("parallel", "arbitrary"),
                                2 * (tm * d * 2 + d * 2 * tn * 2 + tm * tn * 2) + 3 * tm * 2 * tn * 4),
        name="ffn_up",
    )(h, w_g, w_u)


def _ffn_down_kernel(a_ref, w_ref, x_ref, gate_ref, g_ref, b_ref, o_ref, *, alpha, nk, last, rows):
    k = pl.program_id(1)

    if nk == 1:
        o_ref[...] = jnp.dot(a_ref[:, :last], w_ref[:last, :], preferred_element_type=F32)
    else:
        @pl.when(k == 0)
        def _():
            o_ref[...] = jnp.dot(a_ref[...], w_ref[...], preferred_element_type=F32)

        @pl.when((k > 0) & (k < nk - 1))
        def _():
            o_ref[...] += jnp.dot(a_ref[...], w_ref[...], preferred_element_type=F32)

        @pl.when(k == nk - 1)
        def _():
            o_ref[...] += jnp.dot(a_ref[:, :last], w_ref[:last, :], preferred_element_type=F32)

    @pl.when(k == nk - 1)
    def _():
        def chunk(r, carry):
            sl = pl.ds(pl.multiple_of(r * rows, rows), rows)
            z = alpha * x_ref[sl, :] + gate_ref[0] * o_ref[sl, :]
            o_ref[sl, :] = _ln(z) * g_ref[...] + b_ref[...]
            return carry

        lax.fori_loop(0, o_ref.shape[0] // rows, chunk, 0)


def _ffn_down(a, w_d, x1, ln_g, ln_b, mod3, t_seq, i_gate, alpha):
    n, ff = a.shape
    d = w_d.shape[1]
    tm = _tile(512, t_seq)
    tk = min(1024, ff)
    nk = pl.cdiv(ff, tk)
    last = ff - (nk - 1) * tk
    assert last % LANE == 0, "the partial contraction block must stay lane-aligned"
    per = t_seq // tm
    return pl.pallas_call(
        functools.partial(_ffn_down_kernel, alpha=alpha, nk=nk, last=last, rows=_tile(LANE, tm)),
        grid=(n // tm, nk),
        in_specs=[pl.BlockSpec((tm, tk), lambda i, k: (i, k)),
                  pl.BlockSpec((tk, d), lambda i, k: (k, 0)),
                  pl.BlockSpec((tm, d), lambda i, k: (i, 0)),
                  pl.BlockSpec((1, 1, d), lambda i, k: ((i // per) * N_MOD + i_gate, 0, 0)),
                  pl.BlockSpec((1, d), lambda i, k: (0, 0)),
                  pl.BlockSpec((1, d), lambda i, k: (0, 0))],
        out_specs=pl.BlockSpec((tm, d), lambda i, k: (i, 0)),
        out_shape=jax.ShapeDtypeStruct((n, d), F32),
        compiler_params=_params(("parallel", "arbitrary"),
                                2 * (tm * tk * 2 + tk * d * 2 + 2 * tm * d * 4) + 2 * tm * d * 4),
        name="ffn_down_ln",
    )(a, w_d, x1, mod3, ln_g.reshape(1, d), ln_b.reshape(1, d))


def _rope_tables(t_max):
    rows = t_max // GRID_W
    row = jnp.repeat(jnp.arange(rows, dtype=F32), GRID_W)
    col = jnp.tile(jnp.arange(GRID_W, dtype=F32), rows)
    inv = 1.0 / (ROPE_THETA ** (jnp.arange(0, HALF_ROT, 2, dtype=F32) / HALF_ROT))
    ang_r = row[:, None] * inv[None, :]
    ang_c = col[:, None] * inv[None, :]
    cr, sr, cc, sc = jnp.cos(ang_r), jnp.sin(ang_r), jnp.cos(ang_c), jnp.sin(ang_c)
    cos_t = jnp.concatenate([cr, cc, cr, cc], axis=-1)
    sin_t = jnp.concatenate([-sr, -sc, sr, sc], axis=-1)
    return cos_t, sin_t


def _rot_layout(w):
    lead = w.shape[:-1]
    w = w.reshape(lead + (-1, 2, 2, HALF_ROT // 2))
    return jnp.swapaxes(w, -3, -2).reshape(lead + (-1,))


def _prep_layer(w_in, q_norm, k_norm):
    d = w_in.shape[0]
    n_heads = d // HEAD_DIM
    ha = n_heads // 2
    qa_w = ha * HEAD_DIM
    kv_w = (ha // KV_GROUP) * HEAD_DIM
    o_ka = qa_w
    o_va = o_ka + kv_w
    o_qb = o_va + kv_w
    o_kb = o_qb + qa_w
    o_vb = o_kb + kv_w
    o_ga = o_vb + kv_w
    w_qk = _rot_layout(_cast_cols(w_in, 0, o_va))
    w_v_t = jnp.concatenate([_cast_cols(w_in, o_va, kv_w), _cast_cols(w_in, o_vb, kv_w)], axis=1).T
    w_qkb = _cast_cols(w_in, o_qb, o_vb - o_qb)
    w_gates = _cast_cols(w_in, o_ga, w_in.shape[1] - o_ga)
    gain = jnp.concatenate([jnp.tile(_rot_layout(q_norm), ha),
                            jnp.tile(_rot_layout(k_norm), ha // KV_GROUP)]).reshape(1, -1)
    q_scale = jnp.full((qa_w,), LOG2_E / math.sqrt(HEAD_DIM), F32)
    scale = jnp.concatenate([q_scale, jnp.ones((kv_w,), F32)]).reshape(1, -1)
    return dict(w_qk=w_qk, w_v_t=w_v_t, w_qkb=w_qkb, w_gates=w_gates, gain=gain, scale=scale,
                ha=ha, kv_w=kv_w, qa_w=qa_w)


def _encoder_layer(x, mod, p, late_w, sink_row, win_bias, ln1_g, ln1_b, ln2_g, ln2_b, cos_t, sin_t, alpha):
    b, t, d = x.shape
    x2 = x.reshape(b * t, d)
    mod3 = mod.reshape(b * N_MOD, 1, d)
    ha, kv_w, qa_w = p["ha"], p["kv_w"], p["qa_w"]

    h = _ln_mod(x2, mod3, t, 1, 0)
    qa, ka = _qk_proj(h, p["w_qk"], p["gain"], p["scale"], cos_t, sin_t, t, ha)
    va_t, vb_t = _proj_t(h, p["w_v_t"], kv_w // HEAD_DIM)
    qkb = _proj(h, p["w_qkb"], p["scale"], tn_pref=1280)
    gates = _proj(h, p["w_gates"], None, tn_pref=1024)
    if late_w[0].dtype == BF16:
        ya, _ = _global_attention(qa, ka, va_t, b, t, ha)
    else:
        ya, late_w = _global_attention(qa, ka, va_t, b, t, ha, side_casts=late_w)
    w_a, w_b, w_o, w_g, w_u, w_d = late_w
    yb = _window_attention(qkb, vb_t, win_bias, sink_row, b, t, ha, k_col_block=qa_w // kv_w)
    merged = _merge(ya, yb, w_a, w_b, gates)
    z = _out_proj(merged, w_o, x2, mod3, t, 2, alpha)
    x1, h2 = _ln_pair(z, ln1_g, ln1_b, mod3, t, 4, 3)
    a = _ffn_up(h2, w_g, w_u)
    y = _ffn_down(a, w_d, x1, ln2_g, ln2_b, mod3, t, 5, alpha)
    return y.reshape(b, t, d), late_w


def kernel(x_prompt, x_sample, c_prompt, c_sample, w_ada, b_ada, w_in, q_norm_a, k_norm_a, sink_b,
           w_br_a, w_br_b, w_o, ln1_g, ln1_b, w_ffn_gate, w_ffn_up, w_ffn_down, ln2_g, ln2_b):
    depth = w_ada.shape[0]
    alpha = float((2.0 * depth) ** 0.25)
    d = x_prompt.shape[-1]
    bp, bs = c_prompt.shape[0], c_sample.shape[0]
    rows = -(-(bp + bs) // 8) * 8
    cos_t, sin_t = _rope_tables(max(x_prompt.shape[1], x_sample.shape[1]))
    y_p, y_s = x_prompt, x_sample
    c_all = jnp.concatenate([c_prompt, c_sample, jnp.zeros((rows - bp - bs, d), F32)], axis=0)
    win_bias = _window_bias(sink_b.shape[1])
    for l in range(depth):
        mod = _ada(c_all, w_ada[l], b_ada[l]).reshape(rows, N_MOD, d)
        p = _prep_layer(w_in[l], q_norm_a[l], k_norm_a[l])
        late_w = (w_br_a[l], w_br_b[l], w_o[l], w_ffn_gate[l], w_ffn_up[l], w_ffn_down[l])
        sink_row = jnp.repeat(sink_b[l], HEAD_DIM).reshape(1, -1)
        args = (sink_row, win_bias, ln1_g[l], ln1_b[l], ln2_g[l], ln2_b[l], cos_t, sin_t, alpha)
        y_p, late_w = _encoder_layer(y_p, mod[:bp], p, late_w, *args)
        y_s, _ = _encoder_layer(y_s, mod[bp:bp + bs], p, late_w, *args)
    return (y_p, y_s)
```
